```python
import math
import jax, jax.numpy as jnp
from jax import lax
import numpy as np

D_MODEL = 1024
BATCH = 2
SEQ = 8192
DEPTH = 1
DEC_BATCH = 128
DEC_SEQ = 4
PAST_LEN = 2048
PAGE_SIZE = 128

MIX_WIDTH = D_MODEL
DIFF_WIDTH = MIX_WIDTH // 2
RET_WIDTH = MIX_WIDTH - DIFF_WIDTH
DIFF_V_DIM = 128
N_DIFF_HEADS = DIFF_WIDTH // DIFF_V_DIM
DIFF_QK_DIM = DIFF_V_DIM // 2
ROT_DIM = DIFF_QK_DIM // 4
ROPE_THETA = 500000.0
RET_V_DIM = 128
N_RET_HEADS = RET_WIDTH // RET_V_DIM
RET_QK_DIM = RET_V_DIM // 2
RET_THETA = 10000.0
RET_CHUNK = 128
Q_BLOCK = 128
D_FF = -(-8 * D_MODEL // (3 * 256)) * 256
NORM_EPS = 1e-6
SUBLN_EPS = 1e-5

SPLIT_SIZES = (
    N_DIFF_HEADS * 2 * DIFF_QK_DIM,
    N_DIFF_HEADS * 2 * DIFF_QK_DIM,
    N_DIFF_HEADS * DIFF_V_DIM,
    N_RET_HEADS * RET_QK_DIM,
    N_RET_HEADS * RET_QK_DIM,
    N_RET_HEADS * RET_V_DIM,
    RET_WIDTH,
)
IN_WIDTH = sum(SPLIT_SIZES)
SPLIT_POINTS = tuple(int(s) for s in np.cumsum(SPLIT_SIZES)[:-1])

kernel_name = 'hymba_diffattn_retnet_step'


def rms_norm(x, g, eps):
    xf = x.astype(jnp.float32)
    y = xf * lax.rsqrt(jnp.mean(xf * xf, axis=-1, keepdims=True) + eps)
    if g is not None:
        y = y * g.astype(jnp.float32)
    return y.astype(x.dtype)


def rope_table(pos, dim, theta):
    inv = 1.0 / (jnp.float32(theta) ** (jnp.arange(0, dim, 2, dtype=jnp.float32) / dim))
    ang = pos.astype(jnp.float32)[:, None] * inv[None, :]
    return jnp.cos(ang), jnp.sin(ang)


def rotate(x, cos, sin):
    xf = x.astype(jnp.float32)
    x1, x2 = jnp.split(xf, 2, axis=-1)
    return jnp.concatenate([x1 * cos - x2 * sin, x2 * cos + x1 * sin], axis=-1).astype(x.dtype)


def partial_rotary(x, cos, sin):
    return jnp.concatenate([rotate(x[..., :ROT_DIM], cos, sin), x[..., ROT_DIM:]], axis=-1)


def project_mixer_inputs(h, w_in_l, cos_d, sin_d, cos_r, sin_r):
    B, T, _ = h.shape
    z = h @ w_in_l
    dq, dk, dv, rq, rk, rv, rg = jnp.split(z, SPLIT_POINTS, axis=-1)
    dq = partial_rotary(dq.reshape(B, T, N_DIFF_HEADS, 2, DIFF_QK_DIM), cos_d, sin_d) * DIFF_QK_DIM ** -0.5
    dk = partial_rotary(dk.reshape(B, T, N_DIFF_HEADS, 2, DIFF_QK_DIM), cos_d, sin_d)
    dv = dv.reshape(B, T, N_DIFF_HEADS, DIFF_V_DIM)
    rq = rotate(rq.reshape(B, T, N_RET_HEADS, RET_QK_DIM), cos_r, sin_r)
    rk = rotate(rk.reshape(B, T, N_RET_HEADS, RET_QK_DIM), cos_r, sin_r) * RET_QK_DIM ** -0.5
    rv = rv.reshape(B, T, N_RET_HEADS, RET_V_DIM)
    return dq, dk, dv, rq, rk, rv, rg


def diff_attend(q, k, v, mask, lam):
    s = jnp.einsum('bqhmd,bkhmd->bhmqk', q, k).astype(jnp.float32)
    s = jnp.where(mask, s, -1e30)
    p = jax.nn.softmax(s, axis=-1)
    a = p[:, :, 0] - lam * p[:, :, 1]
    return jnp.einsum('bhqk,bkhe->bqhe', a.astype(v.dtype), v)


def prompt_diff_attention(q, k, v, lam):
    B, S = q.shape[:2]
    n_blk = S // Q_BLOCK
    qb = q.reshape(B, n_blk, Q_BLOCK, *q.shape[2:]).swapaxes(0, 1)
    kpos = jnp.arange(S)

    def one_block(args):
        q_blk, i = args
        qpos = i * Q_BLOCK + jnp.arange(Q_BLOCK)
        return diff_attend(q_blk, k, v, kpos[None, :] <= qpos[:, None], lam)

    out = lax.map(one_block, (qb, jnp.arange(n_blk)))
    return out.swapaxes(0, 1).reshape(B, S, *out.shape[3:])


def retention_chunk(q, k, v, state, log_gamma):
    C = q.shape[1]
    qf, kf, vf = (t.astype(jnp.float32) for t in (q, k, v))
    st = state.astype(jnp.float32)
    idx = jnp.arange(C, dtype=jnp.float32)
    rel = idx[:, None] - idx[None, :]
    decay = jnp.where(rel >= 0, jnp.exp(log_gamma[:, None, None] * jnp.maximum(rel, 0.0)), 0.0)
    scores = jnp.einsum('bihd,bjhd->bhij', qf, kf) * decay
    inner = jnp.einsum('bhij,bjhe->bihe', scores, vf)
    q_decay = jnp.exp(log_gamma[None, :] * (idx[:, None] + 1.0))
    cross = jnp.einsum('bihd,bhde->bihe', qf, st) * q_decay[None, :, :, None]
    k_decay = jnp.exp(log_gamma[None, :] * (C - 1.0 - idx[:, None]))
    new_state = (jnp.exp(log_gamma * C)[None, :, None, None] * st
                 + jnp.einsum('bjhd,bjhe->bhde', kf * k_decay[None, :, :, None], vf))
    return inner + cross, new_state


def prompt_retention(q, k, v, log_gamma):
    B, S, H, dk = q.shape
    dv = v.shape[-1]
    n_chunks = S // RET_CHUNK

    def chunks(t):
        return t.reshape(B, n_chunks, RET_CHUNK, *t.shape[2:]).swapaxes(0, 1)

    def step(state, qkv):
        out, state = retention_chunk(*qkv, state, log_gamma)
        return state, out

    state0 = jnp.zeros((B, H, dk, dv), jnp.float32)
    state, out = lax.scan(step, state0, (chunks(q), chunks(k), chunks(v)))
    return out.swapaxes(0, 1).reshape(B, S, H, dv), state


def merge_head_groups(diff_o, ret_o, ret_gate, subln_g, lam_init, w_out_l):
    B, T = diff_o.shape[:2]
    d = rms_norm(diff_o, subln_g, SUBLN_EPS) * (1.0 - lam_init)
    r = rms_norm(ret_o, None, NORM_EPS).reshape(B, T, RET_WIDTH).astype(ret_gate.dtype)
    r = jax.nn.silu(ret_gate) * r
    cat = jnp.concatenate([d.reshape(B, T, DIFF_WIDTH).astype(r.dtype), r], axis=-1)
    return cat @ w_out_l


def swiglu(h, w_in_l, w_out_l):
    g, u = jnp.split(h @ w_in_l, 2, axis=-1)
    return (jax.nn.silu(g) * u) @ w_out_l


def setup_inputs(seed: int = 0) -> dict:
    key = jax.random.key(seed)
    ks = jax.random.split(key, 16)
    n_pages = PAST_LEN // PAGE_SIZE
    n_used = DEC_BATCH * n_pages
    n_phys = n_used + max(1, n_used // 4)

    def normal(k, shape, scale):
        return jax.random.normal(k, shape, jnp.float32) * scale

    page_table = jax.random.permutation(ks[0], n_phys)[:n_used].reshape(DEC_BATCH, n_pages).astype(jnp.int32)
    return {
        'x_prompt': normal(ks[1], (BATCH, SEQ, D_MODEL), 1.0),
        'x_sample': normal(ks[2], (DEC_BATCH, DEC_SEQ, D_MODEL), 1.0),
        'cache_diff_k': normal(ks[3], (DEPTH, n_phys, PAGE_SIZE, N_DIFF_HEADS, 2, DIFF_QK_DIM), 1.0),
        'cache_diff_v': normal(ks[4], (DEPTH, n_phys, PAGE_SIZE, N_DIFF_HEADS, DIFF_V_DIM), 1.0),
        'state_ret': normal(ks[5], (DEPTH, DEC_BATCH, N_RET_HEADS, RET_QK_DIM, RET_V_DIM), 0.5),
        'page_table': page_table,
        'norm_mix_g': 1.0 + normal(ks[6], (DEPTH, D_MODEL), 0.02),
        'w_in': normal(ks[7], (DEPTH, D_MODEL, IN_WIDTH), D_MODEL ** -0.5),
        'diff_lambda': normal(ks[8], (DEPTH, 4, DIFF_QK_DIM), 0.1),
        'diff_subln_g': 1.0 + normal(ks[9], (DEPTH, DIFF_V_DIM), 0.02),
        'w_out': normal(ks[10], (DEPTH, MIX_WIDTH, D_MODEL), MIX_WIDTH ** -0.5),
        'norm_ffn_g': 1.0 + normal(ks[11], (DEPTH, D_MODEL), 0.02),
        'w_ffn_in': normal(ks[12], (DEPTH, D_MODEL, 2 * D_FF), D_MODEL ** -0.5),
        'w_ffn_out': normal(ks[13], (DEPTH, D_FF, D_MODEL), D_FF ** -0.5),
        'norm_final_g': 1.0 + normal(ks[14], (D_MODEL,), 0.02),
    }


def reference(x_prompt, x_sample, cache_diff_k, cache_diff_v, state_ret, page_table,
              norm_mix_g, w_in, diff_lambda, diff_subln_g, w_out, norm_ffn_g,
              w_ffn_in, w_ffn_out, norm_final_g):
    S = x_prompt.shape[1]
    DB, T = x_sample.shape[:2]
    past = page_table.shape[1] * cache_diff_k.shape[2]
    pos_p = jnp.arange(S)
    pos_s = past + jnp.arange(T)

    def tables(pos):
        cd, sd = rope_table(pos, ROT_DIM, ROPE_THETA)
        cr, sr = rope_table(pos, RET_QK_DIM, RET_THETA)
        return cd[:, None, None, :], sd[:, None, None, :], cr[:, None, :], sr[:, None, :]

    tab_p = tables(pos_p)
    tab_s = tables(pos_s)
    mask_s = jnp.arange(past + T)[None, :] <= pos_s[:, None]
    log_gamma = jnp.log(1.0 - 2.0 ** (-5.0 - jnp.arange(N_RET_HEADS, dtype=jnp.float32)))

    xp, xs = x_prompt, x_sample
    k_p, v_p, r_p, k_s, v_s, r_s = [], [], [], [], [], []
    for l in range(DEPTH):
        lam_init = 0.8 - 0.6 * math.exp(-0.3 * l)
        lp = diff_lambda[l].astype(jnp.float32)
        lam = jnp.exp(jnp.sum(lp[0] * lp[1])) - jnp.exp(jnp.sum(lp[2] * lp[3])) + lam_init

        hp = rms_norm(xp, norm_mix_g[l], NORM_EPS)
        dq, dk, dv, rq, rk, rv, rg = project_mixer_inputs(hp, w_in[l], *tab_p)
        diff_o = prompt_diff_attention(dq, dk, dv, lam)
        ret_o, ret_state_p = prompt_retention(rq, rk, rv, log_gamma)
        xp = xp + merge_head_groups(diff_o, ret_o, rg, diff_subln_g[l], lam_init, w_out[l])
        xp = xp + swiglu(rms_norm(xp, norm_ffn_g[l], NORM_EPS), w_ffn_in[l], w_ffn_out[l])
        k_p.append(dk)
        v_p.append(dv)
        r_p.append(ret_state_p.astype(xp.dtype))

        hs = rms_norm(xs, norm_mix_g[l], NORM_EPS)
        sq, sk, sv, tq, tk, tv, tg = project_mixer_inputs(hs, w_in[l], *tab_s)
        k_past = cache_diff_k[l][page_table].reshape(DB, past, N_DIFF_HEADS, 2, DIFF_QK_DIM)
        v_past = cache_diff_v[l][page_table].reshape(DB, past, N_DIFF_HEADS, DIFF_V_DIM)
        k_all = jnp.concatenate([k_past.astype(sk.dtype), sk], axis=1)
        v_all = jnp.concatenate([v_past.astype(sv.dtype), sv], axis=1)
        diff_os = diff_attend(sq, k_all, v_all, mask_s, lam)
        ret_os, ret_state_s = retention_chunk(tq, tk, tv, state_ret[l], log_gamma)
        xs = xs + merge_head_groups(diff_os, ret_os, tg, diff_subln_g[l], lam_init, w_out[l])
        xs = xs + swiglu(rms_norm(xs, norm_ffn_g[l], NORM_EPS), w_ffn_in[l], w_ffn_out[l])
        k_s.append(sk)
        v_s.append(sv)
        r_s.append(ret_state_s.astype(xs.dtype))

    y_prompt = rms_norm(xp, norm_final_g, NORM_EPS)
    y_sample = rms_norm(xs, norm_final_g, NORM_EPS)
    k_prompt = jnp.stack(k_p)
    v_prompt = jnp.stack(v_p)
    ret_prompt = jnp.stack(r_p)
    k_sample = jnp.stack(k_s)
    v_sample = jnp.stack(v_s)
    ret_sample = jnp.stack(r_s)
    return (y_prompt, y_sample, k_prompt, v_prompt, ret_prompt, k_sample, v_sample, ret_sample)
```

```python
import functools
import math

import jax
import jax.numpy as jnp
import numpy as np
from jax import lax
from jax.experimental import pallas as pl
from jax.experimental.pallas import tpu as pltpu

F32 = jnp.float32
BF16 = jnp.bfloat16

N_DIFF_HEADS = 4
DIFF_QK_DIM = 64
DIFF_V_DIM = 128
ROT_DIM = 16
ROPE_THETA = 500000.0
N_RET_HEADS = 4
RET_QK_DIM = 64
RET_V_DIM = 128
RET_THETA = 10000.0
RET_CHUNK = 128
NORM_EPS = 1e-6
SUBLN_EPS = 1e-5
LAM_INIT = 0.8 - 0.6 * math.exp(-0.3 * 0)
MASK_VALUE = -1e30

DIFF_W = N_DIFF_HEADS * 2 * DIFF_QK_DIM
DIFF_VW = N_DIFF_HEADS * DIFF_V_DIM
RET_QW = N_RET_HEADS * RET_QK_DIM
RET_VW = N_RET_HEADS * RET_V_DIM

LANES = 128
VMEM_LIMIT = 56 * 1024 * 1024


def _params(*sem):
    return pltpu.CompilerParams(dimension_semantics=sem, vmem_limit_bytes=VMEM_LIMIT)


def _const_spec(shape):
    nd = len(shape)
    return pl.BlockSpec(shape, lambda *_: (0,) * nd, pipeline_mode=pl.Buffered(1))


def _rms(x, eps):
    return x * lax.rsqrt(jnp.mean(x * x, axis=-1, keepdims=True) + eps)


def _rope_cos_sin(pos, dim, theta):
    inv = 1.0 / (jnp.float32(theta) ** (jnp.arange(0, dim, 2, dtype=F32) / dim))
    ang = pos.astype(F32)[:, None] * inv[None, :]
    return jnp.cos(ang), jnp.sin(ang)


def _rotary_tables(pos):
    t = pos.shape[0]
    cd, sd = _rope_cos_sin(pos, ROT_DIM, ROPE_THETA)
    cr, sr = _rope_cos_sin(pos, RET_QK_DIM, RET_THETA)
    one = jnp.ones((t, DIFF_QK_DIM - ROT_DIM), F32)
    zero = lambda n: jnp.zeros((t, n), F32)
    qs = DIFF_QK_DIM ** -0.5
    c64 = jnp.concatenate([cd, cd, one], axis=1) * qs
    s1 = jnp.concatenate([-sd, zero(DIFF_QK_DIM - ROT_DIM // 2)], axis=1) * qs
    s2 = jnp.concatenate([zero(ROT_DIM // 2), sd, zero(DIFF_QK_DIM - ROT_DIM)], axis=1) * qs
    dq_tabs = tuple(jnp.concatenate([a, a], axis=1) for a in (c64, s1, s2))
    half = RET_QK_DIM // 2
    rc64 = jnp.concatenate([cr, cr], axis=1)
    rs1 = jnp.concatenate([-sr, zero(half)], axis=1)
    rs2 = jnp.concatenate([zero(half), sr], axis=1)
    rq_tabs = tuple(jnp.concatenate([a, a], axis=1) for a in (rc64, rs1, rs2))
    ks = RET_QK_DIM ** -0.5
    t_tabs = (cd.T, sd.T, cr.T * ks, sr.T * ks)
    return dq_tabs + rq_tabs + t_tabs


def _log_gamma():
    return jnp.log(1.0 - 2.0 ** (-5.0 - jnp.arange(N_RET_HEADS, dtype=F32)))


def _retention_tables(chunk, n_tok):
    lg = _log_gamma()
    idx = jnp.arange(n_tok)
    loc = (idx % chunk).astype(F32)
    rel = loc[:, None] - loc[None, :]
    same = (idx[:, None] // chunk) == (idx[None, :] // chunk)
    decay = jnp.where(same[None] & (rel >= 0)[None],
                      jnp.exp(lg[:, None, None] * jnp.maximum(rel, 0.0)[None]), 0.0)
    qdec = jnp.exp(lg[:, None] * (loc[None, :] + 1.0))[:, :, None]
    kdec = jnp.exp(lg[:, None] * (chunk - 1.0 - loc[None, :]))[:, None, :]
    gc = jnp.broadcast_to(jnp.exp(lg * chunk)[:, None, None], (N_RET_HEADS, 1, LANES))
    return decay, qdec, kdec, gc


def _inproj_kernel(x_ref, g_ref, w_ref, wt_ref,
                   cq_ref, s1q_ref, s2q_ref, cr_ref, s1r_ref, s2r_ref,
                   ckt_ref, skt_ref, crt_ref, srt_ref,
                   q_ref, kt32_ref, kt16_ref, v32_ref, v16_ref,
                   rq_ref, rkt_ref, rv_ref, rg_ref):
    x = x_ref[0]
    hb = (_rms(x, NORM_EPS) * g_ref[...]).astype(BF16)

    def mm(lo, hi):
        return jnp.dot(hb, w_ref[:, lo:hi], preferred_element_type=F32)

    def rot_store(z, c, s1, s2, half, out_ref):
        for j in range(z.shape[1] // LANES):
            blk = z[:, j * LANES:(j + 1) * LANES]
            out = (blk * c + pltpu.roll(blk, LANES - half, 1) * s1
                   + pltpu.roll(blk, half, 1) * s2)
            out_ref[0, :, j * LANES:(j + 1) * LANES] = out.astype(out_ref.dtype)

    o = 0
    rot_store(mm(o, o + DIFF_W), cq_ref[...], s1q_ref[...], s2q_ref[...], ROT_DIM // 2, q_ref)
    o += DIFF_W
    zv = mm(o, o + DIFF_VW)
    v32_ref[0] = zv
    v16_ref[0] = zv.astype(BF16)
    o += DIFF_VW
    rot_store(mm(o, o + RET_QW), cr_ref[...], s1r_ref[...], s2r_ref[...], RET_QK_DIM // 2, rq_ref)
    o += RET_QW
    rv_ref[0] = mm(o, o + RET_VW).astype(BF16)
    o += RET_VW
    rg_ref[0] = mm(o, o + RET_VW)

    zt = lax.dot_general(wt_ref[...], hb, (((1,), (1,)), ((), ())),
                         preferred_element_type=F32)
    c, s = ckt_ref[...], skt_ref[...]
    hr = ROT_DIM // 2
    for g in range(N_DIFF_HEADS * 2):
        b = g * DIFF_QK_DIM
        x1, x2 = zt[b:b + hr], zt[b + hr:b + 2 * hr]
        blk = jnp.concatenate(
            [x1 * c - x2 * s, x2 * c + x1 * s, zt[b + 2 * hr:b + DIFF_QK_DIM]], axis=0)
        kt32_ref[0, b:b + DIFF_QK_DIM, :] = blk
        kt16_ref[0, b:b + DIFF_QK_DIM, :] = blk.astype(BF16)
    c, s = crt_ref[...], srt_ref[...]
    hr = RET_QK_DIM // 2
    for h in range(N_RET_HEADS):
        b = DIFF_W + h * RET_QK_DIM
        x1, x2 = zt[b:b + hr], zt[b + hr:b + 2 * hr]
        rkt_ref[0, h * RET_QK_DIM:h * RET_QK_DIM + hr, :] = x1 * c - x2 * s
        rkt_ref[0, h * RET_QK_DIM + hr:(h + 1) * RET_QK_DIM, :] = x2 * c + x1 * s


def _inproj(x, g, w_main, w_t, tables, tm):
    b, s, d = x.shape
    n_main = w_main.shape[1]
    row = lambda width: pl.BlockSpec((1, tm, width), lambda bi, i: (bi, i, 0))
    col = lambda height: pl.BlockSpec((1, height, tm), lambda bi, i: (bi, 0, i))
    ltab = pl.BlockSpec((tm, LANES), lambda bi, i: (i, 0))
    ttab = lambda height: pl.BlockSpec((height, tm), lambda bi, i: (0, i))
    out_shape = (
        jax.ShapeDtypeStruct((b, s, DIFF_W), BF16),
        jax.ShapeDtypeStruct((b, DIFF_W, s), F32),
        jax.ShapeDtypeStruct((b, DIFF_W, s), BF16),
        jax.ShapeDtypeStruct((b, s, DIFF_VW), F32),
        jax.ShapeDtypeStruct((b, s, DIFF_VW), BF16),
        jax.ShapeDtypeStruct((b, s, RET_QW), BF16),
        jax.ShapeDtypeStruct((b, RET_QW, s), F32),
        jax.ShapeDtypeStruct((b, s, RET_VW), BF16),
        jax.ShapeDtypeStruct((b, s, RET_VW), F32),
    )
    out_specs = (row(DIFF_W), col(DIFF_W), col(DIFF_W), row(DIFF_VW), row(DIFF_VW),
                 row(RET_QW), col(RET_QW), row(RET_VW), row(RET_VW))
    return pl.pallas_call(
        _inproj_kernel,
        grid=(b, s // tm),
        in_specs=[row(d), _const_spec((1, d)), _const_spec((d, n_main)),
                  _const_spec((DIFF_W + RET_QW, d))]
                 + [ltab] * 6
                 + [ttab(ROT_DIM // 2)] * 2 + [ttab(RET_QK_DIM // 2)] * 2,
        out_specs=out_specs,
        out_shape=out_shape,
        compiler_params=_params("parallel", "parallel"),
        name="inproj",
    )(x, g, w_main, w_t, *tables)


def _lambda_full(dl_ref):
    lp = dl_ref[...]
    a = jnp.sum(lp[0:1] * lp[1:2], axis=-1, keepdims=True)
    b = jnp.sum(lp[2:3] * lp[3:4], axis=-1, keepdims=True)
    return jnp.exp(a) - jnp.exp(b) + LAM_INIT


def _subln(o, g):
    return _rms(o, SUBLN_EPS) * g * (1.0 - LAM_INIT)


def _attn_kernel(dl_ref, g_ref, q_ref, kt_ref, v_ref, o_ref, acc0, acc1, *, t):
    i = pl.program_id(2)
    q = q_ref[0]
    lane = lax.broadcasted_iota(jnp.int32, q.shape, 1)
    zero = jnp.zeros_like(q)
    q0 = jnp.where(lane < DIFF_QK_DIM, q, zero)
    q1 = jnp.where(lane >= DIFF_QK_DIM, q, zero)
    acc0[...] = jnp.zeros_like(acc0)
    acc1[...] = jnp.zeros_like(acc1)

    def step(j, carry, masked):
        m0, l0, m1, l1 = carry
        off = pl.multiple_of(j * t, t)
        kt = kt_ref[0, :, pl.ds(off, t)]
        v = v_ref[0, pl.ds(off, t), :]
        if masked:
            r = lax.broadcasted_iota(jnp.int32, (t, t), 0)
            c = lax.broadcasted_iota(jnp.int32, (t, t), 1)
            keep = c <= r

        def one(qm, m, l, acc):
            s = jnp.dot(qm, kt, preferred_element_type=F32)
            if masked:
                s = jnp.where(keep, s, MASK_VALUE)
            m_new = jnp.maximum(m, jnp.max(s, axis=-1, keepdims=True))
            alpha = jnp.exp(m - m_new)
            p = jnp.exp(s - m_new)
            l_new = alpha * l + jnp.sum(p, axis=-1, keepdims=True)
            acc[...] = alpha * acc[...] + jnp.dot(p.astype(BF16), v, preferred_element_type=F32)
            return m_new, l_new

        m0, l0 = one(q0, m0, l0, acc0)
        m1, l1 = one(q1, m1, l1, acc1)
        return m0, l0, m1, l1

    init_m = jnp.full((t, 1), MASK_VALUE, F32)
    init_l = jnp.zeros((t, 1), F32)
    carry = (init_m, init_l, init_m, init_l)
    carry = lax.fori_loop(0, i, functools.partial(step, masked=False), carry)
    _, l0, _, l1 = step(i, carry, True)

    lam = _lambda_full(dl_ref)
    o = acc0[...] / l0 - lam * (acc1[...] / l1)
    o_ref[0] = _subln(o, g_ref[...]).astype(o_ref.dtype)


def _prompt_attention(dl, subln_g, q, kt16, v16, t):
    b, s, _ = q.shape
    kernel = functools.partial(_attn_kernel, t=t)
    return pl.pallas_call(
        kernel,
        grid=(b, N_DIFF_HEADS, s // t),
        in_specs=[
            _const_spec(dl.shape),
            _const_spec(subln_g.shape),
            pl.BlockSpec((1, t, LANES), lambda bi, h, i: (bi, i, h)),
            pl.BlockSpec((1, LANES, s), lambda bi, h, i: (bi, h, 0)),
            pl.BlockSpec((1, s, LANES), lambda bi, h, i: (bi, 0, h)),
        ],
        out_specs=pl.BlockSpec((1, t, LANES), lambda bi, h, i: (bi, i, h)),
        out_shape=jax.ShapeDtypeStruct((b, s, DIFF_VW), BF16),
        scratch_shapes=[pltpu.VMEM((t, DIFF_V_DIM), F32), pltpu.VMEM((t, DIFF_V_DIM), F32)],
        compiler_params=_params("parallel", "parallel", "arbitrary"),
        name="prompt_attn",
    )(dl, subln_g, q, kt16, v16)


def _gated_norm(o, gate):
    return jax.nn.silu(gate) * _rms(o, NORM_EPS)


def _ret_kernel(rq_ref, rkt_ref, rv_ref, rg_ref, dec_ref, qdec_ref, kdec_ref, gc_ref,
                r_ref, st_ref, state, *, n_chunks):
    j = pl.program_id(1)

    @pl.when(j == 0)
    def _():
        state[...] = jnp.zeros_like(state)

    c = RET_CHUNK
    lane = lax.broadcasted_iota(jnp.int32, (c, LANES), 1)
    for ci in range(n_chunks):
        rows = slice(ci * c, (ci + 1) * c)
        for hp in range(N_RET_HEADS // 2):
            cols = slice(hp * LANES, (hp + 1) * LANES)
            qp = rq_ref[0, rows, cols]
            ktp = rkt_ref[0, cols, rows]
            ktp16 = ktp.astype(BF16)
            stp = state[hp]
            stp16 = stp.astype(BF16)
            for e in range(2):
                h = hp * 2 + e
                hs = slice(e * RET_QK_DIM, (e + 1) * RET_QK_DIM)
                hv = slice(h * RET_V_DIM, (h + 1) * RET_V_DIM)
                in_head = (lane >= e * RET_QK_DIM) & (lane < (e + 1) * RET_QK_DIM)
                qh = jnp.where(in_head, qp, jnp.zeros_like(qp))
                sc = jnp.dot(qh, ktp16, preferred_element_type=F32) * dec_ref[h]
                vh = rv_ref[0, rows, hv]
                inner = jnp.dot(sc.astype(BF16), vh, preferred_element_type=F32)
                cross = jnp.dot(qh, stp16, preferred_element_type=F32) * qdec_ref[h]
                r_ref[0, rows, hv] = _gated_norm(inner + cross, rg_ref[0, rows, hv]).astype(r_ref.dtype)
                kd = (ktp[hs] * kdec_ref[h]).astype(BF16)
                state[hp, hs, :] = gc_ref[h] * stp[hs] + jnp.dot(kd, vh, preferred_element_type=F32)

    @pl.when(j == pl.num_programs(1) - 1)
    def _():
        st_ref[0] = state[...]


def _prompt_retention(rq, rkt, rv, rg, ts):
    b, s, _ = rq.shape
    tables = _retention_tables(RET_CHUNK, RET_CHUNK)
    kernel = functools.partial(_ret_kernel, n_chunks=ts // RET_CHUNK)
    row = lambda width: pl.BlockSpec((1, ts, width), lambda bi, j: (bi, j, 0))
    n_pair = N_RET_HEADS // 2
    r, st = pl.pallas_call(
        kernel,
        grid=(b, s // ts),
        in_specs=[row(RET_QW), pl.BlockSpec((1, RET_QW, ts), lambda bi, j: (bi, 0, j)),
                  row(RET_VW), row(RET_VW)] + [_const_spec(t.shape) for t in tables],
        out_specs=(row(RET_VW),
                   pl.BlockSpec((1, n_pair, LANES, RET_V_DIM), lambda bi, j: (bi, 0, 0, 0))),
        out_shape=(jax.ShapeDtypeStruct((b, s, RET_VW), BF16),
                   jax.ShapeDtypeStruct((b, n_pair, LANES, RET_V_DIM), F32)),
        scratch_shapes=[pltpu.VMEM((n_pair, LANES, RET_V_DIM), F32)],
        compiler_params=_params("parallel", "arbitrary"),
        name="prompt_ret",
    )(rq, rkt, rv, rg, *tables)
    return r, st.reshape(b, N_RET_HEADS, RET_QK_DIM, RET_V_DIM)


def _sret_kernel(rq_ref, rkt_ref, rv_ref, rg_ref, st_in_ref, dec_ref, qdec_ref, kdec_ref, gc_ref,
                 r_ref, st_out_ref, *, t_seq):
    n = LANES
    nb = n // t_seq
    lane = lax.broadcasted_iota(jnp.int32, (n, LANES), 1)
    tok_r = lax.broadcasted_iota(jnp.int32, (nb, n, LANES), 1) // t_seq
    seq_r = lax.broadcasted_iota(jnp.int32, (nb, n, LANES), 0)
    row_in_seq = tok_r == seq_r
    tok_l = lax.broadcasted_iota(jnp.int32, (nb, RET_QK_DIM, n), 2) // t_seq
    seq_l = lax.broadcasted_iota(jnp.int32, (nb, RET_QK_DIM, n), 0)
    lane_in_seq = tok_l == seq_l
    for hp in range(N_RET_HEADS // 2):
        cols = slice(hp * LANES, (hp + 1) * LANES)
        qp = rq_ref[0, :, cols]
        ktp = rkt_ref[0, cols, :]
        ktp16 = ktp.astype(BF16)
        stp = st_in_ref[:, hp]
        st16 = stp.astype(BF16).reshape(nb * LANES, RET_V_DIM)
        for e in range(2):
            h = hp * 2 + e
            hs = slice(e * RET_QK_DIM, (e + 1) * RET_QK_DIM)
            hv = slice(h * RET_V_DIM, (h + 1) * RET_V_DIM)
            in_head = (lane >= e * RET_QK_DIM) & (lane < (e + 1) * RET_QK_DIM)
            qh = jnp.where(in_head, qp, jnp.zeros_like(qp))
            sc = jnp.dot(qh, ktp16, preferred_element_type=F32) * dec_ref[h]
            vh = rv_ref[0, :, hv]
            inner = jnp.dot(sc.astype(BF16), vh, preferred_element_type=F32)
            qbd = jnp.where(row_in_seq, qh[None], jnp.zeros_like(qh)[None])
            qbd = jnp.concatenate([qbd[b] for b in range(nb)], axis=1)
            cross = jnp.dot(qbd, st16, preferred_element_type=F32) * qdec_ref[h]
            r_ref[0, :, hv] = _gated_norm(inner + cross, rg_ref[0, :, hv]).astype(r_ref.dtype)
            kd = (ktp[hs] * kdec_ref[h]).astype(BF16)
            kds = jnp.where(lane_in_seq, kd[None], jnp.zeros_like(kd)[None])
            upd = jnp.dot(kds.reshape(nb * RET_QK_DIM, n), vh, preferred_element_type=F32)
            st_out_ref[:, hp, hs, :] = (gc_ref[h] * stp[:, hs, :]
                                        + upd.reshape(nb, RET_QK_DIM, RET_V_DIM))


def _sample_retention(rq, rkt, rv, rg, state, t_seq):
    n_tok = rq.shape[1]
    db = state.shape[0]
    nb = LANES // t_seq
    n_pair = N_RET_HEADS // 2
    st_pairs = state.reshape(db, n_pair, LANES, RET_V_DIM)
    tables = _retention_tables(t_seq, LANES)
    row = lambda width: pl.BlockSpec((1, LANES, width), lambda j: (0, j, 0))
    st_spec = pl.BlockSpec((nb, n_pair, LANES, RET_V_DIM), lambda j: (j, 0, 0, 0))
    r, st = pl.pallas_call(
        functools.partial(_sret_kernel, t_seq=t_seq),
        grid=(n_tok // LANES,),
        in_specs=[row(RET_QW), pl.BlockSpec((1, RET_QW, LANES), lambda j: (0, 0, j)),
                  row(RET_VW), row(RET_VW), st_spec] + [_const_spec(t.shape) for t in tables],
        out_specs=(row(RET_VW), st_spec),
        out_shape=(jax.ShapeDtypeStruct((1, n_tok, RET_VW), BF16),
                   jax.ShapeDtypeStruct(st_pairs.shape, F32)),
        compiler_params=_params("parallel"),
        name="sample_ret",
    )(rq, rkt, rv, rg, st_pairs, *tables)
    return r, st.reshape(state.shape)


def _sattn_kernel(pt_ref, dl_ref, g_ref, qb_ref, kn_ref, vn_ref, *refs, n_pages, t_seq):
    del pt_ref
    k_refs = refs[:n_pages]
    v_refs = refs[n_pages:2 * n_pages]
    o_ref = refs[2 * n_pages]
    nh = N_DIFF_HEADS
    half = nh * t_seq
    qb = qb_ref[0].astype(F32)
    s = jnp.concatenate(
        [jnp.dot(qb, k_refs[p][0], preferred_element_type=F32) for p in range(n_pages)],
        axis=1)
    kn = kn_ref[0].astype(F32)
    t_row = lax.broadcasted_iota(jnp.int32, (2 * half, 1), 0) % t_seq
    s_new = []
    for tk in range(t_seq):
        col = jnp.sum(qb * kn[tk:tk + 1, :], axis=-1, keepdims=True)
        s_new.append(jnp.where(tk <= t_row, col, MASK_VALUE))
    m = jnp.max(s, axis=-1, keepdims=True)
    for col in s_new:
        m = jnp.maximum(m, col)
    p = jnp.exp(s - m)
    p_new = [jnp.exp(col - m) for col in s_new]
    l = jnp.sum(p, axis=-1, keepdims=True)
    for col in p_new:
        l = l + col
    lam = _lambda_full(dl_ref)
    a = (p[:half] / l[:half] - lam * (p[half:] / l[half:])).astype(BF16).astype(F32)
    a_new = [(c[:half] / l[:half] - lam * (c[half:] / l[half:])).astype(BF16).astype(F32)
             for c in p_new]
    page = k_refs[0].shape[2]
    vn = vn_ref[0].astype(F32)
    for h in range(nh):
        hv = slice(h * DIFF_V_DIM, (h + 1) * DIFF_V_DIM)
        acc = jnp.zeros((half, DIFF_V_DIM), F32)
        for pg in range(n_pages):
            acc = acc + jnp.dot(a[:, pg * page:(pg + 1) * page], v_refs[pg][0, :, h, :],
                                preferred_element_type=F32)
        for tk in range(t_seq):
            acc = acc + a_new[tk] * vn[tk:tk + 1, hv]
        o = acc[h * t_seq:(h + 1) * t_seq]
        o_ref[0, :, hv] = _subln(o, g_ref[...]).astype(o_ref.dtype)


def _sample_attention(page_table, dl, subln_g, qblk, k_new16, v_new16, kc, vc):
    db, n_pages = page_table.shape
    t_seq = k_new16.shape[1]
    page = kc.shape[2]
    kernel = functools.partial(_sattn_kernel, n_pages=n_pages, t_seq=t_seq)

    def k_spec(p):
        return pl.BlockSpec((1, DIFF_W, page), lambda b, pt: (pt[b * n_pages + p], 0, 0))

    def v_spec(p):
        return pl.BlockSpec((1, page, N_DIFF_HEADS, DIFF_V_DIM),
                            lambda b, pt: (pt[b * n_pages + p], 0, 0, 0))

    per_b = lambda shape: pl.BlockSpec((1,) + shape, lambda b, pt: (b, 0, 0))
    grid_spec = pltpu.PrefetchScalarGridSpec(
        num_scalar_prefetch=1,
        grid=(db,),
        in_specs=[pl.BlockSpec(dl.shape, lambda b, pt: (0, 0)),
                  pl.BlockSpec(subln_g.shape, lambda b, pt: (0, 0)),
                  per_b(qblk.shape[1:]), per_b(k_new16.shape[1:]), per_b(v_new16.shape[1:])]
                 + [k_spec(p) for p in range(n_pages)]
                 + [v_spec(p) for p in range(n_pages)],
        out_specs=per_b((t_seq, DIFF_VW)),
    )
    return pl.pallas_call(
        kernel,
        grid_spec=grid_spec,
        out_shape=jax.ShapeDtypeStruct((db, t_seq, DIFF_VW), BF16),
        compiler_params=_params("parallel"),
        name="sample_attn",
    )(page_table.reshape(-1), dl, subln_g, qblk, k_new16, v_new16,
      *([kc] * n_pages), *([vc] * n_pages))


def _ffn_kernel(x_ref, d_ref, r_ref, wo_ref, gf_ref, wi_ref, wo2_ref, gl_ref, y_ref, act, *, fc):
    d_ff = wo2_ref.shape[0]
    mix = (jnp.dot(d_ref[...], wo_ref[0:DIFF_VW, :], preferred_element_type=F32)
           + jnp.dot(r_ref[...], wo_ref[DIFF_VW:, :], preferred_element_type=F32))
    x1 = x_ref[...] + mix
    hb = (_rms(x1, NORM_EPS) * gf_ref[...]).astype(BF16)
    for c in range(d_ff // fc):
        g = jnp.dot(hb, wi_ref[:, c * fc:(c + 1) * fc], preferred_element_type=F32)
        u = jnp.dot(hb, wi_ref[:, d_ff + c * fc:d_ff + (c + 1) * fc], preferred_element_type=F32)
        act[:, c * fc:(c + 1) * fc] = (jax.nn.silu(g) * u).astype(BF16)
    x2 = x1 + jnp.dot(act[...], wo2_ref[...], preferred_element_type=F32)
    y_ref[...] = _rms(x2, NORM_EPS) * gl_ref[...]


def _merge_ffn(x, d, r, w_out, g_ffn, w_ffn_in, w_ffn_out, g_final, tm, fc):
    n, dm = x.shape
    d_ff = w_ffn_out.shape[0]
    row = lambda width: pl.BlockSpec((tm, width), lambda i: (i, 0))
    return pl.pallas_call(
        functools.partial(_ffn_kernel, fc=fc),
        grid=(n // tm,),
        in_specs=[row(dm), row(DIFF_VW), row(RET_VW), _const_spec(w_out.shape),
                  _const_spec(g_ffn.shape), _const_spec(w_ffn_in.shape),
                  _const_spec(w_ffn_out.shape), _const_spec(g_final.shape)],
        out_specs=row(dm),
        out_shape=jax.ShapeDtypeStruct((n, dm), F32),
        scratch_shapes=[pltpu.VMEM((tm, d_ff), BF16)],
        compiler_params=_params("parallel"),
        name="merge_ffn",
    )(x, d, r, w_out, g_ffn, w_ffn_in, w_ffn_out, g_final)


def _ffn_chunk(d_ff):
    for fc in (512, 256, 128):
        if d_ff % fc == 0:
            return fc
    return d_ff


def kernel(x_prompt, x_sample, cache_diff_k, cache_diff_v, state_ret, page_table, norm_mix_g, w_in, diff_lambda, diff_subln_g, w_out, norm_ffn_g, w_ffn_in, w_ffn_out, norm_final_g):
    bsz, seq, dm = x_prompt.shape
    db, t_seq, _ = x_sample.shape
    n_pages = page_table.shape[1]
    page = cache_diff_k.shape[2]
    past = n_pages * page
    assert w_in.shape[0] == 1, "single layer"
    assert LANES % t_seq == 0 and (db * t_seq) % LANES == 0

    w = w_in[0]
    o_dq, o_dk, o_dv, o_rq, o_rk, o_rv, o_rg = np.cumsum(
        [0, DIFF_W, DIFF_W, DIFF_VW, RET_QW, RET_QW, RET_VW])
    w_main = jnp.concatenate(
        [w[:, o_dq:o_dk], w[:, o_dv:o_rq], w[:, o_rq:o_rk], w[:, o_rv:]], axis=1).astype(BF16)
    w_t = jnp.concatenate([w[:, o_dk:o_dv], w[:, o_rk:o_rv]], axis=1).T.astype(BF16)
    w_out16 = w_out[0].astype(BF16)
    w_ffn_in16 = w_ffn_in[0].astype(BF16)
    w_ffn_out16 = w_ffn_out[0].astype(BF16)
    g_mix, g_ffn = norm_mix_g, norm_ffn_g
    g_final = norm_final_g.reshape(1, dm)
    dl = diff_lambda[0]
    fc = _ffn_chunk(w_ffn_out.shape[1])

    tm = min(512, seq)
    tabs_p = _rotary_tables(jnp.arange(seq))
    q, kt32, kt16, v32, v16, rq, rkt, rv, rg = _inproj(x_prompt, g_mix, w_main, w_t, tabs_p, tm)
    d_p = _prompt_attention(dl, diff_subln_g, q, kt16, v16, min(512, seq))
    r_p, ret_state_p = _prompt_retention(rq, rkt, rv, rg, min(512, seq))
    y_prompt = _merge_ffn(x_prompt.reshape(bsz * seq, dm), d_p.reshape(bsz * seq, DIFF_VW),
                          r_p.reshape(bsz * seq, RET_VW), w_out16, g_ffn, w_ffn_in16,
                          w_ffn_out16, g_final, tm, fc).reshape(bsz, seq, dm)
    k_prompt = jnp.swapaxes(kt32, 1, 2).reshape(1, bsz, seq, N_DIFF_HEADS, 2, DIFF_QK_DIM)
    v_prompt = v32.reshape(1, bsz, seq, N_DIFF_HEADS, DIFF_V_DIM)
    ret_prompt = ret_state_p[None]

    n_s = db * t_seq
    pos_s = past + (jnp.arange(n_s) % t_seq)
    tabs_s = _rotary_tables(pos_s)
    tm_s = min(512, n_s)
    qs, kts32, _, vs32, vs16, rqs, rkts, rvs, rgs = _inproj(
        x_sample.reshape(1, n_s, dm), g_mix, w_main, w_t, tabs_s, tm_s)
    k_s = kts32[0].T
    r_idx = np.arange(2 * N_DIFF_HEADS * t_seq)
    r_map, r_head, r_tok = r_idx // (N_DIFF_HEADS * t_seq), (r_idx // t_seq) % N_DIFF_HEADS, r_idx % t_seq
    feat_owner = np.arange(DIFF_W) // DIFF_QK_DIM
    sel = (feat_owner[None, :] == (r_head * 2 + r_map)[:, None])
    qs3 = qs.reshape(db, t_seq, DIFF_W)
    qblk = jnp.where(sel[None], qs3[:, r_tok, :], jnp.zeros((), BF16))
    kc = jnp.transpose(cache_diff_k[0], (0, 2, 3, 4, 1)).reshape(-1, DIFF_W, page)
    d_s = _sample_attention(page_table, dl, diff_subln_g, qblk,
                            k_s.astype(BF16).reshape(db, t_seq, DIFF_W),
                            vs16.reshape(db, t_seq, DIFF_VW), kc, cache_diff_v[0])
    r_s, ret_state_s = _sample_retention(rqs, rkts, rvs, rgs, state_ret[0], t_seq)
    y_sample = _merge_ffn(x_sample.reshape(n_s, dm), d_s.reshape(n_s, DIFF_VW),
                          r_s.reshape(n_s, RET_VW), w_out16, g_ffn, w_ffn_in16,
                          w_ffn_out16, g_final, tm_s, fc).reshape(db, t_seq, dm)
    k_sample = k_s.reshape(1, db, t_seq, N_DIFF_HEADS, 2, DIFF_QK_DIM)
    v_sample = vs32.reshape(1, db, t_seq, N_DIFF_HEADS, DIFF_V_DIM)
    ret_sample = ret_state_s[None]

    return (y_prompt, y_sample, k_prompt, v_prompt, ret_prompt, k_sample, v_sample, ret_sample)
```

```python
import functools
import math

import jax
import jax.numpy as jnp
import numpy as np
from jax import lax
from jax.experimental import pallas as pl
from jax.experimental.pallas import tpu as pltpu

F32 = jnp.float32
BF16 = jnp.bfloat16

N_DIFF_HEADS = 4
DIFF_QK_DIM = 64
DIFF_V_DIM = 128
ROT_DIM = 16
ROPE_THETA = 500000.0
N_RET_HEADS = 4
RET_QK_DIM = 64
RET_V_DIM = 128
RET_THETA = 10000.0
RET_CHUNK = 128
NORM_EPS = 1e-6
SUBLN_EPS = 1e-5
LAM_INIT = 0.8 - 0.6 * math.exp(-0.3 * 0)
MASK_VALUE = -1e30
Q_SCALE = DIFF_QK_DIM ** -0.5 * math.log2(math.e)

DIFF_W = N_DIFF_HEADS * 2 * DIFF_QK_DIM
DIFF_VW = N_DIFF_HEADS * DIFF_V_DIM
RET_QW = N_RET_HEADS * RET_QK_DIM
RET_VW = N_RET_HEADS * RET_V_DIM

LANES = 128
ACC_PAD = 16
VMEM_LIMIT = 56 * 1024 * 1024


def _params(*sem):
    return pltpu.CompilerParams(dimension_semantics=sem, vmem_limit_bytes=VMEM_LIMIT)


def _const_spec(shape):
    nd = len(shape)
    return pl.BlockSpec(shape, lambda *_: (0,) * nd, pipeline_mode=pl.Buffered(1))


def _rms(x, eps):
    return x * lax.rsqrt(jnp.mean(x * x, axis=-1, keepdims=True) + eps)


def _rope_cos_sin(pos, dim, theta):
    inv = 1.0 / (jnp.float32(theta) ** (jnp.arange(0, dim, 2, dtype=F32) / dim))
    ang = pos.astype(F32)[:, None] * inv[None, :]
    return jnp.cos(ang), jnp.sin(ang)


def _rotary_tables(pos):
    t = pos.shape[0]
    cd, sd = _rope_cos_sin(pos, ROT_DIM, ROPE_THETA)
    cr, sr = _rope_cos_sin(pos, RET_QK_DIM, RET_THETA)
    one = jnp.ones((t, DIFF_QK_DIM - ROT_DIM), F32)
    zero = lambda n: jnp.zeros((t, n), F32)
    c64 = jnp.concatenate([cd, cd, one], axis=1)
    s1 = jnp.concatenate([-sd, zero(DIFF_QK_DIM - ROT_DIM // 2)], axis=1)
    s2 = jnp.concatenate([zero(ROT_DIM // 2), sd, zero(DIFF_QK_DIM - ROT_DIM)], axis=1)
    dk_tabs = tuple(jnp.concatenate([a, a], axis=1) for a in (c64, s1, s2))
    half = RET_QK_DIM // 2
    rc64 = jnp.concatenate([cr, cr], axis=1)
    rs1 = jnp.concatenate([-sr, zero(half)], axis=1)
    rs2 = jnp.concatenate([zero(half), sr], axis=1)
    rq_tabs = tuple(jnp.concatenate([a, a], axis=1) for a in (rc64, rs1, rs2))
    qs = Q_SCALE
    ks = RET_QK_DIM ** -0.5
    t_tabs = (cd.T * qs, sd.T * qs, cd.T, sd.T, cr.T * ks, sr.T * ks)
    return dk_tabs + rq_tabs + t_tabs


def _log_gamma():
    return jnp.log(1.0 - 2.0 ** (-5.0 - jnp.arange(N_RET_HEADS, dtype=F32)))


def _retention_tables(chunk, n_tok):
    lg = _log_gamma()
    idx = jnp.arange(n_tok)
    loc = (idx % chunk).astype(F32)
    rel = loc[:, None] - loc[None, :]
    same = (idx[:, None] // chunk) == (idx[None, :] // chunk)
    decay = jnp.where(same[None] & (rel >= 0)[None],
                      jnp.exp(lg[:, None, None] * jnp.maximum(rel, 0.0)[None]), 0.0)
    qdec = jnp.exp(lg[:, None] * (loc[None, :] + 1.0))[:, :, None]
    kdec = jnp.exp(lg[:, None] * (chunk - 1.0 - loc[None, :]))[:, None, :]
    gc = jnp.broadcast_to(jnp.exp(lg * chunk)[:, None, None], (N_RET_HEADS, 1, LANES))
    return decay, qdec, kdec, gc


def _inproj_kernel(x_ref, g_ref, w_ref, wt_ref,
                   ck_ref, s1k_ref, s2k_ref, cr_ref, s1r_ref, s2r_ref,
                   cqt_ref, sqt_ref, ckt_ref, skt_ref, crt_ref, srt_ref,
                   qt_ref, k16_ref, kt32_ref, v32_ref, vt16_ref,
                   rq_ref, rkt_ref, rv_ref, rg_ref):
    x = x_ref[0]
    hb = (_rms(x, NORM_EPS) * g_ref[...]).astype(BF16)

    def mm(lo, hi):
        return jnp.dot(hb, w_ref[:, lo:hi], preferred_element_type=F32)

    def mm_t(lo, hi):
        return lax.dot_general(wt_ref[lo:hi, :], hb, (((1,), (1,)), ((), ())),
                               preferred_element_type=F32)

    def rot_store(z, c, s1, s2, half, out_ref):
        for j in range(z.shape[1] // LANES):
            blk = z[:, j * LANES:(j + 1) * LANES]
            out = (blk * c + pltpu.roll(blk, LANES - half, 1) * s1
                   + pltpu.roll(blk, half, 1) * s2)
            out_ref[0, :, j * LANES:(j + 1) * LANES] = out.astype(out_ref.dtype)

    def rot_t(zt, base, width, half, c, s, rest_scale=None):
        x1, x2 = zt[base:base + half], zt[base + half:base + 2 * half]
        parts = [x1 * c - x2 * s, x2 * c + x1 * s]
        if 2 * half < width:
            rest = zt[base + 2 * half:base + width]
            parts.append(rest if rest_scale is None else rest * rest_scale)
        return jnp.concatenate(parts, axis=0)

    o = 0
    rot_store(mm(o, o + DIFF_W), ck_ref[...], s1k_ref[...], s2k_ref[...], ROT_DIM // 2, k16_ref)
    o += DIFF_W
    v32_ref[0] = mm(o, o + DIFF_VW)
    o += DIFF_VW
    rot_store(mm(o, o + RET_QW), cr_ref[...], s1r_ref[...], s2r_ref[...], RET_QK_DIM // 2, rq_ref)
    o += RET_QW
    rv_ref[0] = mm(o, o + RET_VW).astype(BF16)
    o += RET_VW
    rg_ref[0] = mm(o, o + RET_VW)

    hr = ROT_DIM // 2
    zt = mm_t(0, DIFF_W)
    c, s = cqt_ref[...], sqt_ref[...]
    for g in range(N_DIFF_HEADS * 2):
        b = g * DIFF_QK_DIM
        qt_ref[0, b:b + DIFF_QK_DIM, :] = rot_t(
            zt, b, DIFF_QK_DIM, hr, c, s, rest_scale=Q_SCALE).astype(BF16)
    zt = mm_t(DIFF_W, 2 * DIFF_W)
    c, s = ckt_ref[...], skt_ref[...]
    for g in range(N_DIFF_HEADS * 2):
        b = g * DIFF_QK_DIM
        kt32_ref[0, b:b + DIFF_QK_DIM, :] = rot_t(zt, b, DIFF_QK_DIM, hr, c, s)
    vt16_ref[0] = mm_t(2 * DIFF_W, 2 * DIFF_W + DIFF_VW).astype(BF16)
    zt = mm_t(2 * DIFF_W + DIFF_VW, 2 * DIFF_W + DIFF_VW + RET_QW)
    c, s = crt_ref[...], srt_ref[...]
    for h in range(N_RET_HEADS):
        b = h * RET_QK_DIM
        rkt_ref[0, b:b + RET_QK_DIM, :] = rot_t(zt, b, RET_QK_DIM, RET_QK_DIM // 2, c, s)


def _inproj(x, g, w_main, w_t, tables, tm):
    b, s, d = x.shape
    row = lambda width: pl.BlockSpec((1, tm, width), lambda bi, i: (bi, i, 0))
    col = lambda height: pl.BlockSpec((1, height, tm), lambda bi, i: (bi, 0, i))
    ltab = pl.BlockSpec((tm, LANES), lambda bi, i: (i, 0))
    ttab = lambda height: pl.BlockSpec((height, tm), lambda bi, i: (0, i))
    out_shape = (
        jax.ShapeDtypeStruct((b, DIFF_W, s), BF16),
        jax.ShapeDtypeStruct((b, s, DIFF_W), BF16),
        jax.ShapeDtypeStruct((b, DIFF_W, s), F32),
        jax.ShapeDtypeStruct((b, s, DIFF_VW), F32),
        jax.ShapeDtypeStruct((b, DIFF_VW, s), BF16),
        jax.ShapeDtypeStruct((b, s, RET_QW), BF16),
        jax.ShapeDtypeStruct((b, RET_QW, s), F32),
        jax.ShapeDtypeStruct((b, s, RET_VW), BF16),
        jax.ShapeDtypeStruct((b, s, RET_VW), F32),
    )
    out_specs = (col(DIFF_W), row(DIFF_W), col(DIFF_W), row(DIFF_VW), col(DIFF_VW),
                 row(RET_QW), col(RET_QW), row(RET_VW), row(RET_VW))
    return pl.pallas_call(
        _inproj_kernel,
        grid=(b, s // tm),
        in_specs=[row(d), _const_spec((1, d)), _const_spec(w_main.shape), _const_spec(w_t.shape)]
                 + [ltab] * 6
                 + [ttab(ROT_DIM // 2)] * 4 + [ttab(RET_QK_DIM // 2)] * 2,
        out_specs=out_specs,
        out_shape=out_shape,
        compiler_params=_params("parallel", "parallel"),
        name="inproj",
    )(x, g, w_main, w_t, *tables)


def _lambda_full(dl_ref):
    lp = dl_ref[...]
    a = jnp.sum(lp[0:1] * lp[1:2], axis=-1, keepdims=True)
    b = jnp.sum(lp[2:3] * lp[3:4], axis=-1, keepdims=True)
    return jnp.exp(a) - jnp.exp(b) + LAM_INIT


def _subln(o, g):
    return _rms(o, SUBLN_EPS) * g * (1.0 - LAM_INIT)


def _attn_kernel(dl_ref, g_ref, qt_ref, k_ref, vt_ref, o_ref, acc, s_scr, p_scr, *, tq, tk):
    i = pl.program_id(2)
    qt = qt_ref[0]
    row = lax.broadcasted_iota(jnp.int32, qt.shape, 0)
    zero = jnp.zeros_like(qt)
    qtb = jnp.concatenate([jnp.where(row < DIFF_QK_DIM, qt, zero),
                           jnp.where(row >= DIFF_QK_DIM, qt, zero)], axis=1)
    acc[...] = jnp.zeros_like(acc)
    n_sub = tq // tk
    n_full = i * n_sub

    def scores(j):
        off = pl.multiple_of(j * tk, tk)
        return jnp.dot(k_ref[0, pl.ds(off, tk), :], qtb, preferred_element_type=F32)

    def softmax_stage(s, m):
        m_new = jnp.maximum(m, jnp.max(s, axis=0, keepdims=True))
        alpha = jnp.exp2(m - m_new)
        p = jnp.exp2(s - m_new)
        return m_new, alpha, p.astype(BF16)

    ones_rows = (lax.broadcasted_iota(jnp.int32, (ACC_PAD, tk), 0) == 0).astype(BF16)

    def value_stage(j, alpha, p):
        off = pl.multiple_of(j * tk, tk)
        vt = jnp.concatenate([vt_ref[0, :, pl.ds(off, tk)], ones_rows], axis=0)
        acc[...] = alpha * acc[...] + jnp.dot(vt, p, preferred_element_type=F32)

    def substep(j, par, m, alpha_prev, diag_off=None, issue_next=True):
        if issue_next:
            s_scr[1 - par] = scores(j + 1)
        value_stage(jnp.maximum(j - 1, 0), alpha_prev, p_scr[1 - par])
        s = s_scr[par]
        if diag_off is not None:
            c = lax.broadcasted_iota(jnp.int32, (tk, 2 * tq), 1)
            c = jnp.where(c >= tq, c - tq, c)
            r = lax.broadcasted_iota(jnp.int32, (tk, 2 * tq), 0) + diag_off
            s = jnp.where(r <= c, s, MASK_VALUE)
        m, alpha, p = softmax_stage(s, m)
        p_scr[par] = p
        return m, alpha

    def body(t, carry):
        m, alpha = substep(2 * t, 0, *carry)
        return substep(2 * t + 1, 1, m, alpha)

    s_scr[0] = scores(0)
    p_scr[1] = jnp.zeros((tk, 2 * tq), BF16)
    carry = (jnp.full((1, 2 * tq), MASK_VALUE, F32), jnp.ones((1, 2 * tq), F32))
    m, alpha = lax.fori_loop(0, n_full // 2, body, carry)
    for jj in range(n_sub):
        m, alpha = substep(n_full + jj, jj % 2, m, alpha, diag_off=jj * tk,
                           issue_next=jj + 1 < n_sub)
    value_stage(n_full + n_sub - 1, alpha, p_scr[(n_sub - 1) % 2])

    lam = _lambda_full(dl_ref)
    a = acc[0:DIFF_V_DIM, :]
    l = acc[DIFF_V_DIM:DIFF_V_DIM + 1, :]
    ot = a[:, :tq] / l[:, :tq] - lam * (a[:, tq:] / l[:, tq:])
    o_ref[0] = _subln(ot.T, g_ref[...]).astype(o_ref.dtype)


def _prompt_attention(dl, subln_g, qt16, k16, vt16, tq, tk):
    b, s, _ = k16.shape
    assert (tq // tk) % 2 == 0, "key blocks alternate between two scratch slots"
    kernel = functools.partial(_attn_kernel, tq=tq, tk=tk)
    return pl.pallas_call(
        kernel,
        grid=(b, N_DIFF_HEADS, s // tq),
        in_specs=[
            _const_spec(dl.shape),
            _const_spec(subln_g.shape),
            pl.BlockSpec((1, LANES, tq), lambda bi, h, i: (bi, h, i)),
            pl.BlockSpec((1, s, LANES), lambda bi, h, i: (bi, 0, h)),
            pl.BlockSpec((1, LANES, s), lambda bi, h, i: (bi, h, 0)),
        ],
        out_specs=pl.BlockSpec((1, tq, LANES), lambda bi, h, i: (bi, i, h)),
        out_shape=jax.ShapeDtypeStruct((b, s, DIFF_VW), BF16),
        scratch_shapes=[pltpu.VMEM((DIFF_V_DIM + ACC_PAD, 2 * tq), F32),
                        pltpu.VMEM((2, tk, 2 * tq), F32),
                        pltpu.VMEM((2, tk, 2 * tq), BF16)],
        compiler_params=_params("parallel", "parallel", "arbitrary"),
        name="prompt_attn",
    )(dl, subln_g, qt16, k16, vt16)


def _gated_norm(o, gate):
    return jax.nn.silu(gate) * _rms(o, NORM_EPS)


def _ret_kernel(rq_ref, rkt_ref, rv_ref, rg_ref, dec_ref, qdec_ref, kdec_ref, gc_ref,
                r_ref, st_ref, state, *, n_chunks):
    j = pl.program_id(1)

    @pl.when(j == 0)
    def _():
        state[...] = jnp.zeros_like(state)

    c = RET_CHUNK
    lane = lax.broadcasted_iota(jnp.int32, (c, LANES), 1)
    for ci in range(n_chunks):
        rows = slice(ci * c, (ci + 1) * c)
        for hp in range(N_RET_HEADS // 2):
            cols = slice(hp * LANES, (hp + 1) * LANES)
            qp = rq_ref[0, rows, cols]
            ktp = rkt_ref[0, cols, rows]
            ktp16 = ktp.astype(BF16)
            stp = state[hp]
            stp16 = stp.astype(BF16)
            for e in range(2):
                h = hp * 2 + e
                hs = slice(e * RET_QK_DIM, (e + 1) * RET_QK_DIM)
                hv = slice(h * RET_V_DIM, (h + 1) * RET_V_DIM)
                in_head = (lane >= e * RET_QK_DIM) & (lane < (e + 1) * RET_QK_DIM)
                qh = jnp.where(in_head, qp, jnp.zeros_like(qp))
                sc = jnp.dot(qh, ktp16, preferred_element_type=F32) * dec_ref[h]
                vh = rv_ref[0, rows, hv]
                inner = jnp.dot(sc.astype(BF16), vh, preferred_element_type=F32)
                cross = jnp.dot(qh, stp16, preferred_element_type=F32) * qdec_ref[h]
                r_ref[0, rows, hv] = _gated_norm(inner + cross, rg_ref[0, rows, hv]).astype(r_ref.dtype)
                kd = (ktp[hs] * kdec_ref[h]).astype(BF16)
                state[hp, hs, :] = gc_ref[h] * stp[hs] + jnp.dot(kd, vh, preferred_element_type=F32)

    @pl.when(j == pl.num_programs(1) - 1)
    def _():
        st_ref[0] = state[...]


def _prompt_retention(rq, rkt, rv, rg, ts):
    b, s, _ = rq.shape
    tables = _retention_tables(RET_CHUNK, RET_CHUNK)
    kernel = functools.partial(_ret_kernel, n_chunks=ts // RET_CHUNK)
    row = lambda width: pl.BlockSpec((1, ts, width), lambda bi, j: (bi, j, 0))
    n_pair = N_RET_HEADS // 2
    r, st = pl.pallas_call(
        kernel,
        grid=(b, s // ts),
        in_specs=[row(RET_QW), pl.BlockSpec((1, RET_QW, ts), lambda bi, j: (bi, 0, j)),
                  row(RET_VW), row(RET_VW)] + [_const_spec(t.shape) for t in tables],
        out_specs=(row(RET_VW),
                   pl.BlockSpec((1, n_pair, LANES, RET_V_DIM), lambda bi, j: (bi, 0, 0, 0))),
        out_shape=(jax.ShapeDtypeStruct((b, s, RET_VW), BF16),
                   jax.ShapeDtypeStruct((b, n_pair, LANES, RET_V_DIM), F32)),
        scratch_shapes=[pltpu.VMEM((n_pair, LANES, RET_V_DIM), F32)],
        compiler_params=_params("parallel", "arbitrary"),
        name="prompt_ret",
    )(rq, rkt, rv, rg, *tables)
    return r, st.reshape(b, N_RET_HEADS, RET_QK_DIM, RET_V_DIM)


def _sret_kernel(rq_ref, rkt_ref, rv_ref, rg_ref, st_in_ref, dec_ref, qdec_ref, kdec_ref, gc_ref,
                 r_ref, st_out_ref, *, t_seq):
    n = LANES
    nb = n // t_seq
    lane = lax.broadcasted_iota(jnp.int32, (n, LANES), 1)
    tok_r = lax.broadcasted_iota(jnp.int32, (nb, n, LANES), 1) // t_seq
    seq_r = lax.broadcasted_iota(jnp.int32, (nb, n, LANES), 0)
    row_in_seq = tok_r == seq_r
    tok_l = lax.broadcasted_iota(jnp.int32, (nb, RET_QK_DIM, n), 2) // t_seq
    seq_l = lax.broadcasted_iota(jnp.int32, (nb, RET_QK_DIM, n), 0)
    lane_in_seq = tok_l == seq_l
    for hp in range(N_RET_HEADS // 2):
        cols = slice(hp * LANES, (hp + 1) * LANES)
        qp = rq_ref[0, :, cols]
        ktp = rkt_ref[0, cols, :]
        ktp16 = ktp.astype(BF16)
        stp = st_in_ref[:, hp]
        st16 = stp.astype(BF16).reshape(nb * LANES, RET_V_DIM)
        for e in range(2):
            h = hp * 2 + e
            hs = slice(e * RET_QK_DIM, (e + 1) * RET_QK_DIM)
            hv = slice(h * RET_V_DIM, (h + 1) * RET_V_DIM)
            in_head = (lane >= e * RET_QK_DIM) & (lane < (e + 1) * RET_QK_DIM)
            qh = jnp.where(in_head, qp, jnp.zeros_like(qp))
            sc = jnp.dot(qh, ktp16, preferred_element_type=F32) * dec_ref[h]
            vh = rv_ref[0, :, hv]
            inner = jnp.dot(sc.astype(BF16), vh, preferred_element_type=F32)
            qbd = jnp.where(row_in_seq, qh[None], jnp.zeros_like(qh)[None])
            qbd = jnp.concatenate([qbd[b] for b in range(nb)], axis=1)
            cross = jnp.dot(qbd, st16, preferred_element_type=F32) * qdec_ref[h]
            r_ref[0, :, hv] = _gated_norm(inner + cross, rg_ref[0, :, hv]).astype(r_ref.dtype)
            kd = (ktp[hs] * kdec_ref[h]).astype(BF16)
            kds = jnp.where(lane_in_seq, kd[None], jnp.zeros_like(kd)[None])
            upd = jnp.dot(kds.reshape(nb * RET_QK_DIM, n), vh, preferred_element_type=F32)
            st_out_ref[:, hp, hs, :] = (gc_ref[h] * stp[:, hs, :]
                                        + upd.reshape(nb, RET_QK_DIM, RET_V_DIM))


def _sample_retention(rq, rkt, rv, rg, state, t_seq):
    n_tok = rq.shape[1]
    db = state.shape[0]
    nb = LANES // t_seq
    n_pair = N_RET_HEADS // 2
    st_pairs = state.reshape(db, n_pair, LANES, RET_V_DIM)
    tables = _retention_tables(t_seq, LANES)
    row = lambda width: pl.BlockSpec((1, LANES, width), lambda j: (0, j, 0))
    st_spec = pl.BlockSpec((nb, n_pair, LANES, RET_V_DIM), lambda j: (j, 0, 0, 0))
    r, st = pl.pallas_call(
        functools.partial(_sret_kernel, t_seq=t_seq),
        grid=(n_tok // LANES,),
        in_specs=[row(RET_QW), pl.BlockSpec((1, RET_QW, LANES), lambda j: (0, 0, j)),
                  row(RET_VW), row(RET_VW), st_spec] + [_const_spec(t.shape) for t in tables],
        out_specs=(row(RET_VW), st_spec),
        out_shape=(jax.ShapeDtypeStruct((1, n_tok, RET_VW), BF16),
                   jax.ShapeDtypeStruct(st_pairs.shape, F32)),
        compiler_params=_params("parallel"),
        name="sample_ret",
    )(rq, rkt, rv, rg, st_pairs, *tables)
    return r, st.reshape(state.shape)


def _sattn_kernel(pt_ref, dl_ref, g_ref, qb_ref, kn_ref, vn_ref, *refs, n_pages, t_seq):
    del pt_ref
    k_refs = refs[:n_pages]
    v_refs = refs[n_pages:2 * n_pages]
    o_ref = refs[2 * n_pages]
    nh = N_DIFF_HEADS
    half = nh * t_seq
    qb = qb_ref[0].astype(F32)
    s = jnp.concatenate(
        [jnp.dot(qb, k_refs[p][0], preferred_element_type=F32) for p in range(n_pages)],
        axis=1)
    kn = kn_ref[0].astype(F32)
    t_row = lax.broadcasted_iota(jnp.int32, (2 * half, 1), 0) % t_seq
    s_new = []
    for tk in range(t_seq):
        col = jnp.sum(qb * kn[tk:tk + 1, :], axis=-1, keepdims=True)
        s_new.append(jnp.where(tk <= t_row, col, MASK_VALUE))
    m = jnp.max(s, axis=-1, keepdims=True)
    for col in s_new:
        m = jnp.maximum(m, col)
    p = jnp.exp2(s - m)
    p_new = [jnp.exp2(col - m) for col in s_new]
    l = jnp.sum(p, axis=-1, keepdims=True)
    for col in p_new:
        l = l + col
    lam = _lambda_full(dl_ref)
    a = (p[:half] / l[:half] - lam * (p[half:] / l[half:])).astype(BF16).astype(F32)
    a_new = [(c[:half] / l[:half] - lam * (c[half:] / l[half:])).astype(BF16).astype(F32)
             for c in p_new]
    page = k_refs[0].shape[2]
    vn = vn_ref[0].astype(F32)
    for h in range(nh):
        hv = slice(h * DIFF_V_DIM, (h + 1) * DIFF_V_DIM)
        acc = jnp.zeros((half, DIFF_V_DIM), F32)
        for pg in range(n_pages):
            acc = acc + jnp.dot(a[:, pg * page:(pg + 1) * page], v_refs[pg][0, :, h, :],
                                preferred_element_type=F32)
        for tk in range(t_seq):
            acc = acc + a_new[tk] * vn[tk:tk + 1, hv]
        o = acc[h * t_seq:(h + 1) * t_seq]
        o_ref[0, :, hv] = _subln(o, g_ref[...]).astype(o_ref.dtype)


def _sample_attention(page_table, dl, subln_g, qblk, k_new16, v_new16, kc, vc):
    db, n_pages = page_table.shape
    t_seq = k_new16.shape[1]
    page = kc.shape[2]
    kernel = functools.partial(_sattn_kernel, n_pages=n_pages, t_seq=t_seq)

    def k_spec(p):
        return pl.BlockSpec((1, DIFF_W, page), lambda b, pt: (pt[b * n_pages + p], 0, 0))

    def v_spec(p):
        return pl.BlockSpec((1, page, N_DIFF_HEADS, DIFF_V_DIM),
                            lambda b, pt: (pt[b * n_pages + p], 0, 0, 0))

    per_b = lambda shape: pl.BlockSpec((1,) + shape, lambda b, pt: (b, 0, 0))
    grid_spec = pltpu.PrefetchScalarGridSpec(
        num_scalar_prefetch=1,
        grid=(db,),
        in_specs=[pl.BlockSpec(dl.shape, lambda b, pt: (0, 0)),
                  pl.BlockSpec(subln_g.shape, lambda b, pt: (0, 0)),
                  per_b(qblk.shape[1:]), per_b(k_new16.shape[1:]), per_b(v_new16.shape[1:])]
                 + [k_spec(p) for p in range(n_pages)]
                 + [v_spec(p) for p in range(n_pages)],
        out_specs=per_b((t_seq, DIFF_VW)),
    )
    return pl.pallas_call(
        kernel,
        grid_spec=grid_spec,
        out_shape=jax.ShapeDtypeStruct((db, t_seq, DIFF_VW), BF16),
        compiler_params=_params("parallel"),
        name="sample_attn",
    )(page_table.reshape(-1), dl, subln_g, qblk, k_new16, v_new16,
      *([kc] * n_pages), *([vc] * n_pages))


def _ffn_kernel(x_ref, d_ref, r_ref, wo_ref, gf_ref, wi_ref, wo2_ref, gl_ref, y_ref, act, *, fc):
    d_ff = wo2_ref.shape[0]
    mix = (jnp.dot(d_ref[...], wo_ref[0:DIFF_VW, :], preferred_element_type=F32)
           + jnp.dot(r_ref[...], wo_ref[DIFF_VW:, :], preferred_element_type=F32))
    x1 = x_ref[...] + mix
    hb = (_rms(x1, NORM_EPS) * gf_ref[...]).astype(BF16)
    for c in range(d_ff // fc):
        g = jnp.dot(hb, wi_ref[:, c * fc:(c + 1) * fc], preferred_element_type=F32)
        u = jnp.dot(hb, wi_ref[:, d_ff + c * fc:d_ff + (c + 1) * fc], preferred_element_type=F32)
        act[:, c * fc:(c + 1) * fc] = (jax.nn.silu(g) * u).astype(BF16)
    x2 = x1 + jnp.dot(act[...], wo2_ref[...], preferred_element_type=F32)
    y_ref[...] = _rms(x2, NORM_EPS) * gl_ref[...]


def _merge_ffn(x, d, r, w_out, g_ffn, w_ffn_in, w_ffn_out, g_final, tm, fc):
    n, dm = x.shape
    d_ff = w_ffn_out.shape[0]
    row = lambda width: pl.BlockSpec((tm, width), lambda i: (i, 0))
    return pl.pallas_call(
        functools.partial(_ffn_kernel, fc=fc),
        grid=(n // tm,),
        in_specs=[row(dm), row(DIFF_VW), row(RET_VW), _const_spec(w_out.shape),
                  _const_spec(g_ffn.shape), _const_spec(w_ffn_in.shape),
                  _const_spec(w_ffn_out.shape), _const_spec(g_final.shape)],
        out_specs=row(dm),
        out_shape=jax.ShapeDtypeStruct((n, dm), F32),
        scratch_shapes=[pltpu.VMEM((tm, d_ff), BF16)],
        compiler_params=_params("parallel"),
        name="merge_ffn",
    )(x, d, r, w_out, g_ffn, w_ffn_in, w_ffn_out, g_final)


def _ffn_chunk(d_ff):
    for fc in (512, 256, 128):
        if d_ff % fc == 0:
            return fc
    return d_ff


def kernel(x_prompt, x_sample, cache_diff_k, cache_diff_v, state_ret, page_table, norm_mix_g, w_in, diff_lambda, diff_subln_g, w_out, norm_ffn_g, w_ffn_in, w_ffn_out, norm_final_g):
    bsz, seq, dm = x_prompt.shape
    db, t_seq, _ = x_sample.shape
    n_pages = page_table.shape[1]
    page = cache_diff_k.shape[2]
    past = n_pages * page
    assert w_in.shape[0] == 1, "single layer"
    assert LANES % t_seq == 0 and (db * t_seq) % LANES == 0

    w = w_in[0]
    o_dq, o_dk, o_dv, o_rq, o_rk, o_rv, o_rg = np.cumsum(
        [0, DIFF_W, DIFF_W, DIFF_VW, RET_QW, RET_QW, RET_VW])
    w_main = jnp.concatenate(
        [w[:, o_dk:o_dv], w[:, o_dv:o_rq], w[:, o_rq:o_rk], w[:, o_rv:]], axis=1).astype(BF16)
    w_t = jnp.concatenate([w[:, o_dq:o_rq], w[:, o_rk:o_rv]], axis=1).T.astype(BF16)
    w_out16 = w_out[0].astype(BF16)
    w_ffn_in16 = w_ffn_in[0].astype(BF16)
    w_ffn_out16 = w_ffn_out[0].astype(BF16)
    g_mix, g_ffn = norm_mix_g, norm_ffn_g
    g_final = norm_final_g.reshape(1, dm)
    dl = diff_lambda[0]
    fc = _ffn_chunk(w_ffn_out.shape[1])

    tm = min(512, seq)
    tabs_p = _rotary_tables(jnp.arange(seq))
    qt16, k16, kt32, v32, vt16, rq, rkt, rv, rg = _inproj(x_prompt, g_mix, w_main, w_t, tabs_p, tm)
    d_p = _prompt_attention(dl, diff_subln_g, qt16, k16, vt16, min(512, seq), min(256, seq))
    r_p, ret_state_p = _prompt_retention(rq, rkt, rv, rg, min(512, seq))
    y_prompt = _merge_ffn(x_prompt.reshape(bsz * seq, dm), d_p.reshape(bsz * seq, DIFF_VW),
                          r_p.reshape(bsz * seq, RET_VW), w_out16, g_ffn, w_ffn_in16,
                          w_ffn_out16, g_final, tm, fc).reshape(bsz, seq, dm)
    k_prompt = jnp.swapaxes(kt32, 1, 2).reshape(1, bsz, seq, N_DIFF_HEADS, 2, DIFF_QK_DIM)
    v_prompt = v32.reshape(1, bsz, seq, N_DIFF_HEADS, DIFF_V_DIM)
    ret_prompt = ret_state_p[None]

    n_s = db * t_seq
    pos_s = past + (jnp.arange(n_s) % t_seq)
    tabs_s = _rotary_tables(pos_s)
    tm_s = min(512, n_s)
    qts, ks16, kts32, vs32, _, rqs, rkts, rvs, rgs = _inproj(
        x_sample.reshape(1, n_s, dm), g_mix, w_main, w_t, tabs_s, tm_s)
    k_s = kts32[0].T
    qs = qts[0].T
    r_idx = np.arange(2 * N_DIFF_HEADS * t_seq)
    r_map, r_head, r_tok = r_idx // (N_DIFF_HEADS * t_seq), (r_idx // t_seq) % N_DIFF_HEADS, r_idx % t_seq
    feat_owner = np.arange(DIFF_W) // DIFF_QK_DIM
    sel = (feat_owner[None, :] == (r_head * 2 + r_map)[:, None])
    qs3 = qs.reshape(db, t_seq, DIFF_W)
    qblk = jnp.where(sel[None], qs3[:, r_tok, :], jnp.zeros((), BF16))
    kc = jnp.transpose(cache_diff_k[0], (0, 2, 3, 4, 1)).reshape(-1, DIFF_W, page)
    d_s = _sample_attention(page_table, dl, diff_subln_g, qblk,
                            ks16.reshape(db, t_seq, DIFF_W),
                            vs32.astype(BF16).reshape(db, t_seq, DIFF_VW), kc, cache_diff_v[0])
    r_s, ret_state_s = _sample_retention(rqs, rkts, rvs, rgs, state_ret[0], t_seq)
    y_sample = _merge_ffn(x_sample.reshape(n_s, dm), d_s.reshape(n_s, DIFF_VW),
                          r_s.reshape(n_s, RET_VW), w_out16, g_ffn, w_ffn_in16,
                          w_ffn_out16, g_final, tm_s, fc).reshape(db, t_seq, dm)
    k_sample = k_s.reshape(1, db, t_seq, N_DIFF_HEADS, 2, DIFF_QK_DIM)
    v_sample = vs32.reshape(1, db, t_seq, N_DIFF_HEADS, DIFF_V_DIM)
    ret_sample = ret_state_s[None]

    return (y_prompt, y_sample, k_prompt, v_prompt, ret_prompt, k_sample, v_sample, ret_sample)
```

```python
import functools
import math

import jax
import jax.numpy as jnp
import numpy as np
from jax import lax
from jax.experimental import pallas as pl
from jax.experimental.pallas import tpu as pltpu

F32 = jnp.float32
BF16 = jnp.bfloat16

N_DIFF_HEADS = 4
DIFF_QK_DIM = 64
DIFF_V_DIM = 128
ROT_DIM = 16
ROPE_THETA = 500000.0
N_RET_HEADS = 4
RET_QK_DIM = 64
RET_V_DIM = 128
RET_THETA = 10000.0
RET_CHUNK = 128
NORM_EPS = 1e-6
SUBLN_EPS = 1e-5
LAM_INIT = 0.8 - 0.6 * math.exp(-0.3 * 0)
MASK_VALUE = -1e30
Q_SCALE = DIFF_QK_DIM ** -0.5 * math.log2(math.e)

DIFF_W = N_DIFF_HEADS * 2 * DIFF_QK_DIM
DIFF_VW = N_DIFF_HEADS * DIFF_V_DIM
RET_QW = N_RET_HEADS * RET_QK_DIM
RET_VW = N_RET_HEADS * RET_V_DIM

LANES = 128
ACC_PAD = 16
VMEM_LIMIT = 56 * 1024 * 1024


def _params(*sem, flags=None):
    return pltpu.CompilerParams(dimension_semantics=sem, vmem_limit_bytes=VMEM_LIMIT, flags=flags)


def _const_spec(shape):
    nd = len(shape)
    return pl.BlockSpec(shape, lambda *_: (0,) * nd, pipeline_mode=pl.Buffered(1))


def _rms(x, eps):
    return x * lax.rsqrt(jnp.mean(x * x, axis=-1, keepdims=True) + eps)


def _inv_freq(freq_idx, dim, theta):
    return 1.0 / (jnp.float32(theta) ** (jnp.asarray(2 * freq_idx, F32) / dim))


def _rotary_tables(pos):
    posf = pos.astype(F32)
    d = np.arange(LANES) % DIFF_QK_DIM

    def lane_tables(half, n_rot, dim, theta):
        ang = posf[:, None] * _inv_freq(d % half, dim, theta)[None, :]
        cos, sin = jnp.cos(ang), jnp.sin(ang)
        first, second = (d < half)[None, :], ((d >= half) & (d < n_rot))[None, :]
        return (jnp.where((d < n_rot)[None, :], cos, 1.0),
                jnp.where(first, -sin, 0.0), jnp.where(second, sin, 0.0))

    def feature_tables(half, dim, theta, scale):
        ang = _inv_freq(np.arange(half), dim, theta)[:, None] * posf[None, :]
        return jnp.cos(ang) * scale, jnp.sin(ang) * scale

    dk_tabs = lane_tables(ROT_DIM // 2, ROT_DIM, ROT_DIM, ROPE_THETA)
    rq_tabs = lane_tables(RET_QK_DIM // 2, RET_QK_DIM, RET_QK_DIM, RET_THETA)
    t_tabs = (feature_tables(ROT_DIM // 2, ROT_DIM, ROPE_THETA, Q_SCALE)
              + feature_tables(ROT_DIM // 2, ROT_DIM, ROPE_THETA, 1.0)
              + feature_tables(RET_QK_DIM // 2, RET_QK_DIM, RET_THETA, RET_QK_DIM ** -0.5))
    return dk_tabs + rq_tabs + t_tabs


def _log_gamma():
    return jnp.log(1.0 - 2.0 ** (-5.0 - jnp.arange(N_RET_HEADS, dtype=F32)))


def _retention_tables(chunk, n_tok):
    lg = _log_gamma()
    idx = jnp.arange(n_tok)
    loc = (idx % chunk).astype(F32)
    rel = loc[:, None] - loc[None, :]
    same = (idx[:, None] // chunk) == (idx[None, :] // chunk)
    decay = jnp.where(same[None] & (rel >= 0)[None],
                      jnp.exp(lg[:, None, None] * jnp.maximum(rel, 0.0)[None]), 0.0)
    qdec = jnp.exp(lg[:, None] * (loc[None, :] + 1.0))[:, :, None]
    kdec = jnp.exp(lg[:, None] * (chunk - 1.0 - loc[None, :]))[:, None, :]
    gc = jnp.broadcast_to(jnp.exp(lg * chunk)[:, None, None], (N_RET_HEADS, 1, LANES))
    return decay, qdec, kdec, gc


def _inproj_kernel(x_ref, g_ref, w_ref, wt_ref,
                   ck_ref, s1k_ref, s2k_ref, cr_ref, s1r_ref, s2r_ref,
                   cqt_ref, sqt_ref, ckt_ref, skt_ref, crt_ref, srt_ref,
                   qt_ref, k16_ref, kt32_ref, v32_ref, vt16_ref,
                   rq_ref, rkt_ref, rv_ref, rg_ref):
    x = x_ref[0]
    hb = (_rms(x, NORM_EPS) * g_ref[...]).astype(BF16)

    def mm(lo, hi):
        return jnp.dot(hb, w_ref[:, lo:hi], preferred_element_type=F32)

    def mm_t(lo, hi):
        return lax.dot_general(wt_ref[lo:hi, :], hb, (((1,), (1,)), ((), ())),
                               preferred_element_type=F32)

    def rot_store(z, c, s1, s2, half, out_ref):
        for j in range(z.shape[1] // LANES):
            blk = z[:, j * LANES:(j + 1) * LANES]
            out = (blk * c + pltpu.roll(blk, LANES - half, 1) * s1
                   + pltpu.roll(blk, half, 1) * s2)
            out_ref[0, :, j * LANES:(j + 1) * LANES] = out.astype(out_ref.dtype)

    def rot_t(zt, base, width, half, c, s, rest_scale=None):
        x1, x2 = zt[base:base + half], zt[base + half:base + 2 * half]
        parts = [x1 * c - x2 * s, x2 * c + x1 * s]
        if 2 * half < width:
            rest = zt[base + 2 * half:base + width]
            parts.append(rest if rest_scale is None else rest * rest_scale)
        return jnp.concatenate(parts, axis=0)

    o = 0
    rot_store(mm(o, o + DIFF_W), ck_ref[...], s1k_ref[...], s2k_ref[...], ROT_DIM // 2, k16_ref)
    o += DIFF_W
    v32_ref[0] = mm(o, o + DIFF_VW)
    o += DIFF_VW
    rot_store(mm(o, o + RET_QW), cr_ref[...], s1r_ref[...], s2r_ref[...], RET_QK_DIM // 2, rq_ref)
    o += RET_QW
    rv_ref[0] = mm(o, o + RET_VW).astype(BF16)
    o += RET_VW
    rg_ref[0] = mm(o, o + RET_VW)

    hr = ROT_DIM // 2
    zt = mm_t(0, DIFF_W)
    c, s = cqt_ref[...], sqt_ref[...]
    for g in range(N_DIFF_HEADS * 2):
        b = g * DIFF_QK_DIM
        qt_ref[0, b:b + DIFF_QK_DIM, :] = rot_t(
            zt, b, DIFF_QK_DIM, hr, c, s, rest_scale=Q_SCALE).astype(BF16)
    zt = mm_t(DIFF_W, 2 * DIFF_W)
    c, s = ckt_ref[...], skt_ref[...]
    for g in range(N_DIFF_HEADS * 2):
        b = g * DIFF_QK_DIM
        kt32_ref[0, b:b + DIFF_QK_DIM, :] = rot_t(zt, b, DIFF_QK_DIM, hr, c, s)
    vt16_ref[0] = mm_t(2 * DIFF_W, 2 * DIFF_W + DIFF_VW).astype(BF16)
    zt = mm_t(2 * DIFF_W + DIFF_VW, 2 * DIFF_W + DIFF_VW + RET_QW)
    c, s = crt_ref[...], srt_ref[...]
    for h in range(N_RET_HEADS):
        b = h * RET_QK_DIM
        rkt_ref[0, b:b + RET_QK_DIM, :] = rot_t(zt, b, RET_QK_DIM, RET_QK_DIM // 2, c, s)


def _inproj(x, g, w_main, w_t, tables, tm):
    b, s, d = x.shape
    row = lambda width: pl.BlockSpec((1, tm, width), lambda bi, i: (bi, i, 0))
    col = lambda height: pl.BlockSpec((1, height, tm), lambda bi, i: (bi, 0, i))
    ltab = pl.BlockSpec((tm, LANES), lambda bi, i: (i, 0))
    ttab = lambda height: pl.BlockSpec((height, tm), lambda bi, i: (0, i))
    out_shape = (
        jax.ShapeDtypeStruct((b, DIFF_W, s), BF16),
        jax.ShapeDtypeStruct((b, s, DIFF_W), BF16),
        jax.ShapeDtypeStruct((b, DIFF_W, s), F32),
        jax.ShapeDtypeStruct((b, s, DIFF_VW), F32),
        jax.ShapeDtypeStruct((b, DIFF_VW, s), BF16),
        jax.ShapeDtypeStruct((b, s, RET_QW), BF16),
        jax.ShapeDtypeStruct((b, RET_QW, s), F32),
        jax.ShapeDtypeStruct((b, s, RET_VW), BF16),
        jax.ShapeDtypeStruct((b, s, RET_VW), F32),
    )
    out_specs = (col(DIFF_W), row(DIFF_W), col(DIFF_W), row(DIFF_VW), col(DIFF_VW),
                 row(RET_QW), col(RET_QW), row(RET_VW), row(RET_VW))
    return pl.pallas_call(
        _inproj_kernel,
        grid=(b, s // tm),
        in_specs=[row(d), _const_spec((1, d)), _const_spec(w_main.shape), _const_spec(w_t.shape)]
                 + [ltab] * 6
                 + [ttab(ROT_DIM // 2)] * 4 + [ttab(RET_QK_DIM // 2)] * 2,
        out_specs=out_specs,
        out_shape=out_shape,
        compiler_params=_params("parallel", "parallel"),
        name="inproj",
    )(x, g, w_main, w_t, *tables)


def _lambda_full(dl_ref):
    lp = dl_ref[...]
    a = jnp.sum(lp[0:1] * lp[1:2], axis=-1, keepdims=True)
    b = jnp.sum(lp[2:3] * lp[3:4], axis=-1, keepdims=True)
    return jnp.exp(a) - jnp.exp(b) + LAM_INIT


def _subln(o, g):
    return _rms(o, SUBLN_EPS) * g * (1.0 - LAM_INIT)


def _attn_kernel(dl_ref, g_ref, qt_ref, k_ref, vt_ref, o_ref, acc, s_scr, p_scr, *, tq, tk):
    i = pl.program_id(2)
    qt = qt_ref[0]
    row = lax.broadcasted_iota(jnp.int32, qt.shape, 0)
    zero = jnp.zeros_like(qt)
    qtb = jnp.concatenate([jnp.where(row < DIFF_QK_DIM, qt, zero),
                           jnp.where(row >= DIFF_QK_DIM, qt, zero)], axis=1)
    acc[...] = jnp.zeros_like(acc)
    n_sub = tq // tk
    n_full = i * n_sub

    def scores(j):
        off = pl.multiple_of(j * tk, tk)
        return jnp.dot(k_ref[0, pl.ds(off, tk), :], qtb, preferred_element_type=F32)

    def softmax_stage(s, m):
        m_new = jnp.maximum(m, jnp.max(s, axis=0, keepdims=True))
        alpha = jnp.exp2(m - m_new)
        p = jnp.exp2(s - m_new)
        return m_new, alpha, p.astype(BF16)

    ones_rows = (lax.broadcasted_iota(jnp.int32, (ACC_PAD, tk), 0) == 0).astype(BF16)

    def value_stage(j, alpha, p):
        off = pl.multiple_of(j * tk, tk)
        vt = jnp.concatenate([vt_ref[0, :, pl.ds(off, tk)], ones_rows], axis=0)
        acc[...] = alpha * acc[...] + jnp.dot(vt, p, preferred_element_type=F32)

    def substep(j, par, m, alpha_prev, diag_off=None, issue_next=True):
        if issue_next:
            s_scr[1 - par] = scores(j + 1)
        value_stage(jnp.maximum(j - 1, 0), alpha_prev, p_scr[1 - par])
        s = s_scr[par]
        if diag_off is not None:
            c = lax.broadcasted_iota(jnp.int32, (tk, 2 * tq), 1)
            c = jnp.where(c >= tq, c - tq, c)
            r = lax.broadcasted_iota(jnp.int32, (tk, 2 * tq), 0) + diag_off
            s = jnp.where(r <= c, s, MASK_VALUE)
        m, alpha, p = softmax_stage(s, m)
        p_scr[par] = p
        return m, alpha

    def body(t, carry):
        m, alpha = substep(2 * t, 0, *carry)
        return substep(2 * t + 1, 1, m, alpha)

    s_scr[0] = scores(0)
    p_scr[1] = jnp.zeros((tk, 2 * tq), BF16)
    carry = (jnp.full((1, 2 * tq), MASK_VALUE, F32), jnp.ones((1, 2 * tq), F32))
    m, alpha = lax.fori_loop(0, n_full // 2, body, carry)
    for jj in range(n_sub):
        m, alpha = substep(n_full + jj, jj % 2, m, alpha, diag_off=jj * tk,
                           issue_next=jj + 1 < n_sub)
    value_stage(n_full + n_sub - 1, alpha, p_scr[(n_sub - 1) % 2])

    lam = _lambda_full(dl_ref)
    a = acc[0:DIFF_V_DIM, :]
    l = acc[DIFF_V_DIM:DIFF_V_DIM + 1, :]
    ot = a[:, :tq] / l[:, :tq] - lam * (a[:, tq:] / l[:, tq:])
    o_ref[0] = _subln(ot.T, g_ref[...]).astype(o_ref.dtype)


def _prompt_attention(dl, subln_g, qt16, k16, vt16, tq, tk):
    b, s, _ = k16.shape
    assert (tq // tk) % 2 == 0, "key blocks alternate between two scratch slots"
    kernel = functools.partial(_attn_kernel, tq=tq, tk=tk)
    return pl.pallas_call(
        kernel,
        grid=(b, N_DIFF_HEADS, s // tq),
        in_specs=[
            _const_spec(dl.shape),
            _const_spec(subln_g.shape),
            pl.BlockSpec((1, LANES, tq), lambda bi, h, i: (bi, h, i)),
            pl.BlockSpec((1, s, LANES), lambda bi, h, i: (bi, 0, h)),
            pl.BlockSpec((1, LANES, s), lambda bi, h, i: (bi, h, 0)),
        ],
        out_specs=pl.BlockSpec((1, tq, LANES), lambda bi, h, i: (bi, i, h)),
        out_shape=jax.ShapeDtypeStruct((b, s, DIFF_VW), BF16),
        scratch_shapes=[pltpu.VMEM((DIFF_V_DIM + ACC_PAD, 2 * tq), F32),
                        pltpu.VMEM((2, tk, 2 * tq), F32),
                        pltpu.VMEM((2, tk, 2 * tq), BF16)],
        compiler_params=_params("parallel", "parallel", "arbitrary"),
        name="prompt_attn",
    )(dl, subln_g, qt16, k16, vt16)


def _gated_norm(o, gate):
    return jax.nn.silu(gate) * _rms(o, NORM_EPS)


def _ret_kernel(rq_ref, rkt_ref, rv_ref, rg_ref, dec_ref, qdec_ref, kdec_ref, gc_ref,
                r_ref, st_ref, state, *, n_chunks):
    j = pl.program_id(1)

    @pl.when(j == 0)
    def _():
        state[...] = jnp.zeros_like(state)

    c = RET_CHUNK
    lane = lax.broadcasted_iota(jnp.int32, (c, LANES), 1)
    for ci in range(n_chunks):
        rows = slice(ci * c, (ci + 1) * c)
        for hp in range(N_RET_HEADS // 2):
            cols = slice(hp * LANES, (hp + 1) * LANES)
            qp = rq_ref[0, rows, cols]
            ktp = rkt_ref[0, cols, rows]
            ktp16 = ktp.astype(BF16)
            stp = state[hp]
            stp16 = stp.astype(BF16)
            for e in range(2):
                h = hp * 2 + e
                hs = slice(e * RET_QK_DIM, (e + 1) * RET_QK_DIM)
                hv = slice(h * RET_V_DIM, (h + 1) * RET_V_DIM)
                in_head = (lane >= e * RET_QK_DIM) & (lane < (e + 1) * RET_QK_DIM)
                qh = jnp.where(in_head, qp, jnp.zeros_like(qp))
                sc = jnp.dot(qh, ktp16, preferred_element_type=F32) * dec_ref[h]
                vh = rv_ref[0, rows, hv]
                inner = jnp.dot(sc.astype(BF16), vh, preferred_element_type=F32)
                cross = jnp.dot(qh, stp16, preferred_element_type=F32) * qdec_ref[h]
                r_ref[0, rows, hv] = _gated_norm(inner + cross, rg_ref[0, rows, hv]).astype(r_ref.dtype)
                kd = (ktp[hs] * kdec_ref[h]).astype(BF16)
                state[hp, hs, :] = gc_ref[h] * stp[hs] + jnp.dot(kd, vh, preferred_element_type=F32)

    @pl.when(j == pl.num_programs(1) - 1)
    def _():
        st_ref[0] = state[...]


def _prompt_retention(rq, rkt, rv, rg, ts):
    b, s, _ = rq.shape
    tables = _retention_tables(RET_CHUNK, RET_CHUNK)
    kernel = functools.partial(_ret_kernel, n_chunks=ts // RET_CHUNK)
    row = lambda width: pl.BlockSpec((1, ts, width), lambda bi, j: (bi, j, 0))
    n_pair = N_RET_HEADS // 2
    r, st = pl.pallas_call(
        kernel,
        grid=(b, s // ts),
        in_specs=[row(RET_QW), pl.BlockSpec((1, RET_QW, ts), lambda bi, j: (bi, 0, j)),
                  row(RET_VW), row(RET_VW)] + [_const_spec(t.shape) for t in tables],
        out_specs=(row(RET_VW),
                   pl.BlockSpec((1, n_pair, LANES, RET_V_DIM), lambda bi, j: (bi, 0, 0, 0))),
        out_shape=(jax.ShapeDtypeStruct((b, s, RET_VW), BF16),
                   jax.ShapeDtypeStruct((b, n_pair, LANES, RET_V_DIM), F32)),
        scratch_shapes=[pltpu.VMEM((n_pair, LANES, RET_V_DIM), F32)],
        compiler_params=_params("parallel", "arbitrary"),
        name="prompt_ret",
    )(rq, rkt, rv, rg, *tables)
    return r, st.reshape(b, N_RET_HEADS, RET_QK_DIM, RET_V_DIM)


def _sret_kernel(rq_ref, rkt_ref, rv_ref, rg_ref, st_in_ref, dec_ref, qdec_ref, kdec_ref, gc_ref,
                 r_ref, st_out_ref, *, t_seq):
    n = LANES
    nb = n // t_seq
    lane = lax.broadcasted_iota(jnp.int32, (n, LANES), 1)
    tok_r = lax.broadcasted_iota(jnp.int32, (nb, n, LANES), 1) // t_seq
    seq_r = lax.broadcasted_iota(jnp.int32, (nb, n, LANES), 0)
    row_in_seq = tok_r == seq_r
    tok_l = lax.broadcasted_iota(jnp.int32, (nb, RET_QK_DIM, n), 2) // t_seq
    seq_l = lax.broadcasted_iota(jnp.int32, (nb, RET_QK_DIM, n), 0)
    lane_in_seq = tok_l == seq_l
    for hp in range(N_RET_HEADS // 2):
        cols = slice(hp * LANES, (hp + 1) * LANES)
        qp = rq_ref[0, :, cols]
        ktp = rkt_ref[0, cols, :]
        ktp16 = ktp.astype(BF16)
        stp = st_in_ref[:, hp]
        st16 = stp.astype(BF16).reshape(nb * LANES, RET_V_DIM)
        for e in range(2):
            h = hp * 2 + e
            hs = slice(e * RET_QK_DIM, (e + 1) * RET_QK_DIM)
            hv = slice(h * RET_V_DIM, (h + 1) * RET_V_DIM)
            in_head = (lane >= e * RET_QK_DIM) & (lane < (e + 1) * RET_QK_DIM)
            qh = jnp.where(in_head, qp, jnp.zeros_like(qp))
            sc = jnp.dot(qh, ktp16, preferred_element_type=F32) * dec_ref[h]
            vh = rv_ref[0, :, hv]
            inner = jnp.dot(sc.astype(BF16), vh, preferred_element_type=F32)
            qbd = jnp.where(row_in_seq, qh[None], jnp.zeros_like(qh)[None])
            qbd = jnp.concatenate([qbd[b] for b in range(nb)], axis=1)
            cross = jnp.dot(qbd, st16, preferred_element_type=F32) * qdec_ref[h]
            r_ref[0, :, hv] = _gated_norm(inner + cross, rg_ref[0, :, hv]).astype(r_ref.dtype)
            kd = (ktp[hs] * kdec_ref[h]).astype(BF16)
            kds = jnp.where(lane_in_seq, kd[None], jnp.zeros_like(kd)[None])
            upd = jnp.dot(kds.reshape(nb * RET_QK_DIM, n), vh, preferred_element_type=F32)
            st_out_ref[:, hp, hs, :] = (gc_ref[h] * stp[:, hs, :]
                                        + upd.reshape(nb, RET_QK_DIM, RET_V_DIM))


def _sample_retention(rq, rkt, rv, rg, state, t_seq):
    n_tok = rq.shape[1]
    db = state.shape[0]
    nb = LANES // t_seq
    n_pair = N_RET_HEADS // 2
    st_pairs = state.reshape(db, n_pair, LANES, RET_V_DIM)
    tables = _retention_tables(t_seq, LANES)
    row = lambda width: pl.BlockSpec((1, LANES, width), lambda j: (0, j, 0))
    st_spec = pl.BlockSpec((nb, n_pair, LANES, RET_V_DIM), lambda j: (j, 0, 0, 0))
    r, st = pl.pallas_call(
        functools.partial(_sret_kernel, t_seq=t_seq),
        grid=(n_tok // LANES,),
        in_specs=[row(RET_QW), pl.BlockSpec((1, RET_QW, LANES), lambda j: (0, 0, j)),
                  row(RET_VW), row(RET_VW), st_spec] + [_const_spec(t.shape) for t in tables],
        out_specs=(row(RET_VW), st_spec),
        out_shape=(jax.ShapeDtypeStruct((1, n_tok, RET_VW), BF16),
                   jax.ShapeDtypeStruct(st_pairs.shape, F32)),
        compiler_params=_params("parallel"),
        name="sample_ret",
    )(rq, rkt, rv, rg, st_pairs, *tables)
    return r, st.reshape(state.shape)


def _sattn_kernel(pt_ref, dl_ref, g_ref, qb_ref, kn_ref, vn_ref, *refs, n_pages, t_seq):
    del pt_ref
    k_refs = refs[:n_pages]
    v_refs = refs[n_pages:2 * n_pages]
    o_ref = refs[2 * n_pages]
    nh = N_DIFF_HEADS
    half = nh * t_seq
    qb = qb_ref[0].astype(F32)
    s = jnp.concatenate(
        [jnp.dot(qb, k_refs[p][0], preferred_element_type=F32) for p in range(n_pages)],
        axis=1)
    kn = kn_ref[0].astype(F32)
    t_row = lax.broadcasted_iota(jnp.int32, (2 * half, 1), 0) % t_seq
    s_new = []
    for tk in range(t_seq):
        col = jnp.sum(qb * kn[tk:tk + 1, :], axis=-1, keepdims=True)
        s_new.append(jnp.where(tk <= t_row, col, MASK_VALUE))
    m = jnp.max(s, axis=-1, keepdims=True)
    for col in s_new:
        m = jnp.maximum(m, col)
    p = jnp.exp2(s - m)
    p_new = [jnp.exp2(col - m) for col in s_new]
    l = jnp.sum(p, axis=-1, keepdims=True)
    for col in p_new:
        l = l + col
    lam = _lambda_full(dl_ref)
    a = (p[:half] / l[:half] - lam * (p[half:] / l[half:])).astype(BF16)
    a_new = [(c[:half] / l[:half] - lam * (c[half:] / l[half:])).astype(BF16).astype(F32)
             for c in p_new]
    page = k_refs[0].shape[2]
    a_pages = jnp.concatenate([a[:, pg * page:(pg + 1) * page] for pg in range(n_pages)],
                              axis=0)
    spread = (lax.broadcasted_iota(jnp.int32, (page, page * nh), 1) // nh
              == lax.broadcasted_iota(jnp.int32, (page, page * nh), 0)).astype(BF16)
    ax = jnp.dot(a_pages, spread, preferred_element_type=F32)
    col_head = lax.broadcasted_iota(jnp.int32, ax.shape, 1) % nh
    row_head = (lax.broadcasted_iota(jnp.int32, ax.shape, 0) % half) // t_seq
    ax = jnp.where(col_head == row_head, ax, 0.0)
    acc = jnp.zeros((half, DIFF_V_DIM), F32)
    for pg in range(n_pages):
        acc = acc + jnp.dot(ax[pg * half:(pg + 1) * half], v_refs[pg][0],
                            preferred_element_type=F32)
    vn = vn_ref[0].astype(F32)
    for h in range(nh):
        hv = slice(h * DIFF_V_DIM, (h + 1) * DIFF_V_DIM)
        rows = slice(h * t_seq, (h + 1) * t_seq)
        o = acc[rows]
        for tk in range(t_seq):
            o = o + a_new[tk][rows] * vn[tk:tk + 1, hv]
        o_ref[0, :, hv] = _subln(o, g_ref[...]).astype(o_ref.dtype)


def _sample_attention(page_table, dl, subln_g, qblk, k_new16, v_new16, kc, vc):
    db, n_pages = page_table.shape
    t_seq = k_new16.shape[1]
    page = kc.shape[2]
    kernel = functools.partial(_sattn_kernel, n_pages=n_pages, t_seq=t_seq)

    def k_spec(p):
        return pl.BlockSpec((1, DIFF_W, page), lambda b, pt: (pt[b * n_pages + p], 0, 0))

    def v_spec(p):
        return pl.BlockSpec((1, page * N_DIFF_HEADS, DIFF_V_DIM),
                            lambda b, pt: (pt[b * n_pages + p], 0, 0))

    per_b = lambda shape: pl.BlockSpec((1,) + shape, lambda b, pt: (b, 0, 0))
    grid_spec = pltpu.PrefetchScalarGridSpec(
        num_scalar_prefetch=1,
        grid=(db,),
        in_specs=[pl.BlockSpec(dl.shape, lambda b, pt: (0, 0)),
                  pl.BlockSpec(subln_g.shape, lambda b, pt: (0, 0)),
                  per_b(qblk.shape[1:]), per_b(k_new16.shape[1:]), per_b(v_new16.shape[1:])]
                 + [k_spec(p) for p in range(n_pages)]
                 + [v_spec(p) for p in range(n_pages)],
        out_specs=per_b((t_seq, DIFF_VW)),
    )
    return pl.pallas_call(
        kernel,
        grid_spec=grid_spec,
        out_shape=jax.ShapeDtypeStruct((db, t_seq, DIFF_VW), BF16),
        compiler_params=_params("parallel"),
        name="sample_attn",
    )(page_table.reshape(-1), dl, subln_g, qblk, k_new16, v_new16,
      *([kc] * n_pages), *([vc] * n_pages))


def _ffn_kernel(x_ref, d_ref, r_ref, wo_ref, gf_ref, wi_ref, wo2_ref, gl_ref, y_ref, act, *, fc):
    d_ff = wo2_ref.shape[0]
    mix = (jnp.dot(d_ref[...], wo_ref[0:DIFF_VW, :], preferred_element_type=F32)
           + jnp.dot(r_ref[...], wo_ref[DIFF_VW:, :], preferred_element_type=F32))
    x1 = x_ref[...] + mix
    hb = (_rms(x1, NORM_EPS) * gf_ref[...]).astype(BF16)
    for c in range(d_ff // fc):
        g = jnp.dot(hb, wi_ref[:, c * fc:(c + 1) * fc], preferred_element_type=F32)
        u = jnp.dot(hb, wi_ref[:, d_ff + c * fc:d_ff + (c + 1) * fc], preferred_element_type=F32)
        act[:, c * fc:(c + 1) * fc] = (jax.nn.silu(g) * u).astype(BF16)
    x2 = x1 + jnp.dot(act[...], wo2_ref[...], preferred_element_type=F32)
    y_ref[...] = _rms(x2, NORM_EPS) * gl_ref[...]


def _merge_ffn(x, d, r, w_out, g_ffn, w_ffn_in, w_ffn_out, g_final, tm, fc):
    n, dm = x.shape
    d_ff = w_ffn_out.shape[0]
    row = lambda width: pl.BlockSpec((tm, width), lambda i: (i, 0))
    return pl.pallas_call(
        functools.partial(_ffn_kernel, fc=fc),
        grid=(n // tm,),
        in_specs=[row(dm), row(DIFF_VW), row(RET_VW), _const_spec(w_out.shape),
                  _const_spec(g_ffn.shape), _const_spec(w_ffn_in.shape),
                  _const_spec(w_ffn_out.shape), _const_spec(g_final.shape)],
        out_specs=row(dm),
        out_shape=jax.ShapeDtypeStruct((n, dm), F32),
        scratch_shapes=[pltpu.VMEM((tm, d_ff), BF16)],
        compiler_params=_params("parallel"),
        name="merge_ffn",
    )(x, d, r, w_out, g_ffn, w_ffn_in, w_ffn_out, g_final)


def _ffn_chunk(d_ff):
    for fc in (512, 256, 128):
        if d_ff % fc == 0:
            return fc
    return d_ff


def kernel(x_prompt, x_sample, cache_diff_k, cache_diff_v, state_ret, page_table, norm_mix_g, w_in, diff_lambda, diff_subln_g, w_out, norm_ffn_g, w_ffn_in, w_ffn_out, norm_final_g):
    bsz, seq, dm = x_prompt.shape
    db, t_seq, _ = x_sample.shape
    n_pages = page_table.shape[1]
    page = cache_diff_k.shape[2]
    past = n_pages * page
    assert w_in.shape[0] == 1, "single layer"
    assert LANES % t_seq == 0 and (db * t_seq) % LANES == 0

    w = w_in[0]
    o_dq, o_dk, o_dv, o_rq, o_rk, o_rv, o_rg = np.cumsum(
        [0, DIFF_W, DIFF_W, DIFF_VW, RET_QW, RET_QW, RET_VW])
    w_main = jnp.concatenate(
        [w[:, o_dk:o_dv], w[:, o_dv:o_rq], w[:, o_rq:o_rk], w[:, o_rv:]], axis=1).astype(BF16)
    w_t = jnp.concatenate([w[:, o_dq:o_rq], w[:, o_rk:o_rv]], axis=1).T.astype(BF16)
    w_out16 = w_out[0].astype(BF16)
    w_ffn_in16 = w_ffn_in[0].astype(BF16)
    w_ffn_out16 = w_ffn_out[0].astype(BF16)
    g_mix, g_ffn = norm_mix_g, norm_ffn_g
    g_final = norm_final_g.reshape(1, dm)
    dl = diff_lambda[0]
    fc = _ffn_chunk(w_ffn_out.shape[1])

    tm = min(512, seq)
    tabs_p = _rotary_tables(jnp.arange(seq))
    qt16, k16, kt32, v32, vt16, rq, rkt, rv, rg = _inproj(x_prompt, g_mix, w_main, w_t, tabs_p, tm)
    d_p = _prompt_attention(dl, diff_subln_g, qt16, k16, vt16, min(1024, seq), min(512, seq))
    r_p, ret_state_p = _prompt_retention(rq, rkt, rv, rg, min(512, seq))
    y_prompt = _merge_ffn(x_prompt.reshape(bsz * seq, dm), d_p.reshape(bsz * seq, DIFF_VW),
                          r_p.reshape(bsz * seq, RET_VW), w_out16, g_ffn, w_ffn_in16,
                          w_ffn_out16, g_final, tm, fc).reshape(bsz, seq, dm)
    k_prompt = jnp.swapaxes(kt32, 1, 2).reshape(1, bsz, seq, N_DIFF_HEADS, 2, DIFF_QK_DIM)
    v_prompt = v32.reshape(1, bsz, seq, N_DIFF_HEADS, DIFF_V_DIM)
    ret_prompt = ret_state_p[None]

    n_s = db * t_seq
    pos_s = past + (jnp.arange(n_s) % t_seq)
    tabs_s = _rotary_tables(pos_s)
    tm_s = min(512, n_s)
    qts, ks16, kts32, vs32, _, rqs, rkts, rvs, rgs = _inproj(
        x_sample.reshape(1, n_s, dm), g_mix, w_main, w_t, tabs_s, tm_s)
    k_s = kts32[0].T
    qs = qts[0].T
    r_idx = np.arange(2 * N_DIFF_HEADS * t_seq)
    r_map, r_head, r_tok = r_idx // (N_DIFF_HEADS * t_seq), (r_idx // t_seq) % N_DIFF_HEADS, r_idx % t_seq
    feat_owner = np.arange(DIFF_W) // DIFF_QK_DIM
    sel = (feat_owner[None, :] == (r_head * 2 + r_map)[:, None])
    qs3 = qs.reshape(db, t_seq, DIFF_W)
    qblk = jnp.where(sel[None], qs3[:, r_tok, :], jnp.zeros((), BF16))
    kc = jnp.transpose(cache_diff_k[0], (0, 2, 3, 4, 1)).reshape(-1, DIFF_W, page)
    d_s = _sample_attention(page_table, dl, diff_subln_g, qblk,
                            ks16.reshape(db, t_seq, DIFF_W),
                            vs32.astype(BF16).reshape(db, t_seq, DIFF_VW), kc,
                            cache_diff_v[0].reshape(-1, page * N_DIFF_HEADS, DIFF_V_DIM))
    r_s, ret_state_s = _sample_retention(rqs, rkts, rvs, rgs, state_ret[0], t_seq)
    y_sample = _merge_ffn(x_sample.reshape(n_s, dm), d_s.reshape(n_s, DIFF_VW),
                          r_s.reshape(n_s, RET_VW), w_out16, g_ffn, w_ffn_in16,
                          w_ffn_out16, g_final, tm_s, fc).reshape(db, t_seq, dm)
    k_sample = k_s.reshape(1, db, t_seq, N_DIFF_HEADS, 2, DIFF_QK_DIM)
    v_sample = vs32.reshape(1, db, t_seq, N_DIFF_HEADS, DIFF_V_DIM)
    ret_sample = ret_state_s[None]

    return (y_prompt, y_sample, k_prompt, v_prompt, ret_prompt, k_sample, v_sample, ret_sample)
```

```python
import functools
import math

import jax
import jax.numpy as jnp
import numpy as np
from jax import lax
from jax.experimental import pallas as pl
from jax.experimental.pallas import tpu as pltpu

F32 = jnp.float32
BF16 = jnp.bfloat16

N_DIFF_HEADS = 4
DIFF_QK_DIM = 64
DIFF_V_DIM = 128
ROT_DIM = 16
ROPE_THETA = 500000.0
N_RET_HEADS = 4
RET_QK_DIM = 64
RET_V_DIM = 128
RET_THETA = 10000.0
RET_CHUNK = 128
NORM_EPS = 1e-6
SUBLN_EPS = 1e-5
LAM_INIT = 0.8 - 0.6 * math.exp(-0.3 * 0)
MASK_VALUE = -1e30
Q_SCALE = DIFF_QK_DIM ** -0.5 * math.log2(math.e)

DIFF_W = N_DIFF_HEADS * 2 * DIFF_QK_DIM
DIFF_VW = N_DIFF_HEADS * DIFF_V_DIM
RET_QW = N_RET_HEADS * RET_QK_DIM
RET_VW = N_RET_HEADS * RET_V_DIM
W_IN_OFFSETS = tuple(int(o) for o in np.cumsum([0, DIFF_W, DIFF_W, DIFF_VW, RET_QW, RET_QW, RET_VW]))

LANES = 128
ACC_PAD = 16
VMEM_LIMIT = 56 * 1024 * 1024


def _params(*sem, flags=None):
    return pltpu.CompilerParams(dimension_semantics=sem, vmem_limit_bytes=VMEM_LIMIT, flags=flags)


def _const_spec(shape):
    nd = len(shape)
    return pl.BlockSpec(shape, lambda *_: (0,) * nd, pipeline_mode=pl.Buffered(1))


def _rms(x, eps):
    return x * lax.rsqrt(jnp.mean(x * x, axis=-1, keepdims=True) + eps)


def _inv_freq(freq_idx, dim, theta):
    return 1.0 / (jnp.float32(theta) ** (jnp.asarray(2 * freq_idx, F32) / dim))


def _rotary_tables(pos):
    posf = pos.astype(F32)

    def feature_tables(half, dim, theta, scale):
        ang = _inv_freq(np.arange(half), dim, theta)[:, None] * posf[None, :]
        return jnp.cos(ang) * scale, jnp.sin(ang) * scale

    return (feature_tables(ROT_DIM // 2, ROT_DIM, ROPE_THETA, Q_SCALE)
            + feature_tables(ROT_DIM // 2, ROT_DIM, ROPE_THETA, 1.0)
            + feature_tables(RET_QK_DIM // 2, RET_QK_DIM, RET_THETA, 1.0)
            + feature_tables(RET_QK_DIM // 2, RET_QK_DIM, RET_THETA, RET_QK_DIM ** -0.5))


def _log_gamma():
    return jnp.log(1.0 - 2.0 ** (-5.0 - jnp.arange(N_RET_HEADS, dtype=F32)))


def _retention_tables(chunk, n_tok):
    lg = _log_gamma()
    idx = jnp.arange(n_tok)
    loc = (idx % chunk).astype(F32)
    rel = loc[:, None] - loc[None, :]
    same = (idx[:, None] // chunk) == (idx[None, :] // chunk)
    decay = jnp.where(same[None] & (rel >= 0)[None],
                      jnp.exp(lg[:, None, None] * jnp.maximum(rel, 0.0)[None]), 0.0)
    qdec = jnp.exp(lg[:, None] * (loc[None, :] + 1.0))[:, :, None]
    kdec = jnp.exp(lg[:, None] * (chunk - 1.0 - loc[None, :]))[:, None, :]
    gc = jnp.broadcast_to(jnp.exp(lg * chunk)[:, None, None], (N_RET_HEADS, 1, LANES))
    return decay, qdec, kdec, gc


def _inproj_kernel(x_ref, g_ref, w_ref,
                   cqt_ref, sqt_ref, ckt_ref, skt_ref, crqt_ref, srqt_ref, crkt_ref, srkt_ref,
                   qt_ref, k16_ref, kt32_ref, v32_ref, vt16_ref,
                   rq_ref, rkt_ref, rv_ref, rg_ref, rqt_scr):
    x = x_ref[0]
    hb = (_rms(x, NORM_EPS) * g_ref[...]).astype(BF16)

    def mm(lo, width):
        return jnp.dot(hb, w_ref[:, lo:lo + width], preferred_element_type=F32)

    def rot_t(zt, base, width, half, c, s, rest_scale=None):
        x1, x2 = zt[base:base + half], zt[base + half:base + 2 * half]
        parts = [x1 * c - x2 * s, x2 * c + x1 * s]
        if 2 * half < width:
            rest = zt[base + 2 * half:base + width]
            parts.append(rest if rest_scale is None else rest * rest_scale)
        return jnp.concatenate(parts, axis=0)

    o_dq, o_dk, o_dv, o_rq, o_rk, o_rv, o_rg = W_IN_OFFSETS
    hr = ROT_DIM // 2
    zt = mm(o_dq, DIFF_W).T
    c, s = cqt_ref[...], sqt_ref[...]
    for g in range(N_DIFF_HEADS * 2):
        b = g * DIFF_QK_DIM
        qt_ref[0, b:b + DIFF_QK_DIM, :] = rot_t(
            zt, b, DIFF_QK_DIM, hr, c, s, rest_scale=Q_SCALE).astype(BF16)
    zt = mm(o_dk, DIFF_W).T
    c, s = ckt_ref[...], skt_ref[...]
    for g in range(N_DIFF_HEADS * 2):
        b = g * DIFF_QK_DIM
        kt32_ref[0, b:b + DIFF_QK_DIM, :] = rot_t(zt, b, DIFF_QK_DIM, hr, c, s)
    k16_ref[0] = kt32_ref[0].T.astype(BF16)
    zv = mm(o_dv, DIFF_VW)
    for h in range(N_DIFF_HEADS):
        v32_ref[0, pl.ds(h, zv.shape[0], stride=N_DIFF_HEADS), :] = (
            zv[:, h * DIFF_V_DIM:(h + 1) * DIFF_V_DIM])
    vt16_ref[0] = zv.T.astype(BF16)
    zt = mm(o_rq, RET_QW).T
    c, s = crqt_ref[...], srqt_ref[...]
    for h in range(N_RET_HEADS):
        b = h * RET_QK_DIM
        rqt_scr[b:b + RET_QK_DIM, :] = rot_t(zt, b, RET_QK_DIM, RET_QK_DIM // 2, c, s)
    rq_ref[0] = rqt_scr[...].T.astype(BF16)
    zt = mm(o_rk, RET_QW).T
    c, s = crkt_ref[...], srkt_ref[...]
    for h in range(N_RET_HEADS):
        b = h * RET_QK_DIM
        rkt_ref[0, b:b + RET_QK_DIM, :] = rot_t(zt, b, RET_QK_DIM, RET_QK_DIM // 2, c, s)
    rv_ref[0] = mm(o_rv, RET_VW).astype(BF16)
    rg_ref[0] = mm(o_rg, RET_VW)


def _inproj(x, g, w16, tables, tm):
    b, s, d = x.shape
    row = lambda width: pl.BlockSpec((1, tm, width), lambda bi, i: (bi, i, 0))
    col = lambda height: pl.BlockSpec((1, height, tm), lambda bi, i: (bi, 0, i))
    ttab = lambda height: pl.BlockSpec((height, tm), lambda bi, i: (0, i))
    out_shape = (
        jax.ShapeDtypeStruct((b, DIFF_W, s), BF16),
        jax.ShapeDtypeStruct((b, s, DIFF_W), BF16),
        jax.ShapeDtypeStruct((b, DIFF_W, s), F32),
        jax.ShapeDtypeStruct((b, s * N_DIFF_HEADS, DIFF_V_DIM), F32),
        jax.ShapeDtypeStruct((b, DIFF_VW, s), BF16),
        jax.ShapeDtypeStruct((b, s, RET_QW), BF16),
        jax.ShapeDtypeStruct((b, RET_QW, s), F32),
        jax.ShapeDtypeStruct((b, s, RET_VW), BF16),
        jax.ShapeDtypeStruct((b, s, RET_VW), F32),
    )
    v_rows = pl.BlockSpec((1, tm * N_DIFF_HEADS, DIFF_V_DIM), lambda bi, i: (bi, i, 0))
    out_specs = (col(DIFF_W), row(DIFF_W), col(DIFF_W), v_rows, col(DIFF_VW),
                 row(RET_QW), col(RET_QW), row(RET_VW), row(RET_VW))
    return pl.pallas_call(
        _inproj_kernel,
        grid=(b, s // tm),
        in_specs=[row(d), _const_spec((1, d)), _const_spec(w16.shape)]
                 + [ttab(ROT_DIM // 2)] * 4 + [ttab(RET_QK_DIM // 2)] * 4,
        out_specs=out_specs,
        out_shape=out_shape,
        scratch_shapes=[pltpu.VMEM((RET_QW, tm), F32)],
        compiler_params=_params("parallel", "parallel"),
        name="inproj",
    )(x, g, w16, *tables)


def _lambda_full(dl_ref):
    lp = dl_ref[...]
    a = jnp.sum(lp[0:1] * lp[1:2], axis=-1, keepdims=True)
    b = jnp.sum(lp[2:3] * lp[3:4], axis=-1, keepdims=True)
    return jnp.exp(a) - jnp.exp(b) + LAM_INIT


def _subln(o, g):
    return _rms(o, SUBLN_EPS) * g * (1.0 - LAM_INIT)


def _attn_kernel(dl_ref, g_ref, qt_ref, k_ref, vt_ref, o_ref, acc, s_scr, p_scr, *, tq, tk):
    i = pl.program_id(2)
    qt = qt_ref[0]
    row = lax.broadcasted_iota(jnp.int32, qt.shape, 0)
    zero = jnp.zeros_like(qt)
    qtb = jnp.concatenate([jnp.where(row < DIFF_QK_DIM, qt, zero),
                           jnp.where(row >= DIFF_QK_DIM, qt, zero)], axis=1)
    acc[...] = jnp.zeros_like(acc)
    n_sub = tq // tk
    n_full = i * n_sub

    def scores(j):
        off = pl.multiple_of(j * tk, tk)
        return jnp.dot(k_ref[0, pl.ds(off, tk), :], qtb, preferred_element_type=F32)

    def softmax_stage(s, m):
        m_new = jnp.maximum(m, jnp.max(s, axis=0, keepdims=True))
        alpha = jnp.exp2(m - m_new)
        p = jnp.exp2(s - m_new)
        return m_new, alpha, p.astype(BF16)

    ones_rows = (lax.broadcasted_iota(jnp.int32, (ACC_PAD, tk), 0) == 0).astype(BF16)

    def value_stage(j, alpha, p):
        off = pl.multiple_of(j * tk, tk)
        vt = jnp.concatenate([vt_ref[0, :, pl.ds(off, tk)], ones_rows], axis=0)
        acc[...] = alpha * acc[...] + jnp.dot(vt, p, preferred_element_type=F32)

    def substep(j, par, m, alpha_prev, diag_off=None, issue_next=True):
        if issue_next:
            s_scr[1 - par] = scores(j + 1)
        value_stage(jnp.maximum(j - 1, 0), alpha_prev, p_scr[1 - par])
        s = s_scr[par]
        if diag_off is not None:
            c = lax.broadcasted_iota(jnp.int32, (tk, 2 * tq), 1)
            c = jnp.where(c >= tq, c - tq, c)
            r = lax.broadcasted_iota(jnp.int32, (tk, 2 * tq), 0) + diag_off
            s = jnp.where(r <= c, s, MASK_VALUE)
        m, alpha, p = softmax_stage(s, m)
        p_scr[par] = p
        return m, alpha

    def body(t, carry):
        m, alpha = substep(2 * t, 0, *carry)
        return substep(2 * t + 1, 1, m, alpha)

    s_scr[0] = scores(0)
    p_scr[1] = jnp.zeros((tk, 2 * tq), BF16)
    carry = (jnp.full((1, 2 * tq), MASK_VALUE, F32), jnp.ones((1, 2 * tq), F32))
    m, alpha = lax.fori_loop(0, n_full // 2, body, carry)
    for jj in range(n_sub):
        m, alpha = substep(n_full + jj, jj % 2, m, alpha, diag_off=jj * tk,
                           issue_next=jj + 1 < n_sub)
    value_stage(n_full + n_sub - 1, alpha, p_scr[(n_sub - 1) % 2])

    lam = _lambda_full(dl_ref)
    a = acc[0:DIFF_V_DIM, :]
    l = acc[DIFF_V_DIM:DIFF_V_DIM + 1, :]
    ot = a[:, :tq] / l[:, :tq] - lam * (a[:, tq:] / l[:, tq:])
    o_ref[0] = _subln(ot.T, g_ref[...]).astype(o_ref.dtype)


def _prompt_attention(dl, subln_g, qt16, k16, vt16, tq, tk):
    b, s, _ = k16.shape
    assert (tq // tk) % 2 == 0, "key blocks alternate between two scratch slots"
    kernel = functools.partial(_attn_kernel, tq=tq, tk=tk)
    return pl.pallas_call(
        kernel,
        grid=(b, N_DIFF_HEADS, s // tq),
        in_specs=[
            _const_spec(dl.shape),
            _const_spec(subln_g.shape),
            pl.BlockSpec((1, LANES, tq), lambda bi, h, i: (bi, h, i)),
            pl.BlockSpec((1, s, LANES), lambda bi, h, i: (bi, 0, h)),
            pl.BlockSpec((1, LANES, s), lambda bi, h, i: (bi, h, 0)),
        ],
        out_specs=pl.BlockSpec((1, tq, LANES), lambda bi, h, i: (bi, i, h)),
        out_shape=jax.ShapeDtypeStruct((b, s, DIFF_VW), BF16),
        scratch_shapes=[pltpu.VMEM((DIFF_V_DIM + ACC_PAD, 2 * tq), F32),
                        pltpu.VMEM((2, tk, 2 * tq), F32),
                        pltpu.VMEM((2, tk, 2 * tq), BF16)],
        compiler_params=_params("parallel", "parallel", "arbitrary"),
        name="prompt_attn",
    )(dl, subln_g, qt16, k16, vt16)


def _gated_norm(o, gate):
    return jax.nn.silu(gate) * _rms(o, NORM_EPS)


def _ret_kernel(rq_ref, rkt_ref, rv_ref, rg_ref, dec_ref, qdec_ref, kdec_ref, gc_ref,
                r_ref, st_ref, state, *, n_chunks):
    j = pl.program_id(1)

    @pl.when(j == 0)
    def _():
        state[...] = jnp.zeros_like(state)

    c = RET_CHUNK
    lane = lax.broadcasted_iota(jnp.int32, (c, LANES), 1)
    n_pair = N_RET_HEADS // 2
    st = [state[hp] for hp in range(n_pair)]
    for ci in range(n_chunks):
        rows = slice(ci * c, (ci + 1) * c)
        for hp in range(n_pair):
            cols = slice(hp * LANES, (hp + 1) * LANES)
            qp = rq_ref[0, rows, cols]
            ktp = rkt_ref[0, cols, rows]
            ktp16 = ktp.astype(BF16)
            stp = st[hp]
            stp16 = stp.astype(BF16)
            q2 = jnp.concatenate(
                [jnp.where((lane >= e * RET_QK_DIM) & (lane < (e + 1) * RET_QK_DIM),
                           qp, jnp.zeros_like(qp)) for e in range(2)], axis=0)
            sc2 = jnp.dot(q2, ktp16, preferred_element_type=F32)
            cross2 = jnp.dot(q2, stp16, preferred_element_type=F32)
            new = []
            for e in range(2):
                h = hp * 2 + e
                hs = slice(e * RET_QK_DIM, (e + 1) * RET_QK_DIM)
                hv = slice(h * RET_V_DIM, (h + 1) * RET_V_DIM)
                sc = sc2[e * c:(e + 1) * c] * dec_ref[h]
                vh = rv_ref[0, rows, hv]
                kd = (ktp[hs] * kdec_ref[h]).astype(BF16)
                both = jnp.dot(jnp.concatenate([sc.astype(BF16), kd], axis=0), vh,
                               preferred_element_type=F32)
                cross = cross2[e * c:(e + 1) * c] * qdec_ref[h]
                r_ref[0, rows, hv] = _gated_norm(both[:c] + cross, rg_ref[0, rows, hv]).astype(r_ref.dtype)
                new.append(gc_ref[h] * stp[hs] + both[c:])
            st[hp] = jnp.concatenate(new, axis=0)
    for hp in range(n_pair):
        state[hp] = st[hp]

    @pl.when(j == pl.num_programs(1) - 1)
    def _():
        for hp in range(n_pair):
            st_ref[0, hp] = st[hp]


def _prompt_retention(rq, rkt, rv, rg, ts):
    b, s, _ = rq.shape
    tables = _retention_tables(RET_CHUNK, RET_CHUNK)
    kernel = functools.partial(_ret_kernel, n_chunks=ts // RET_CHUNK)
    row = lambda width: pl.BlockSpec((1, ts, width), lambda bi, j: (bi, j, 0))
    n_pair = N_RET_HEADS // 2
    r, st = pl.pallas_call(
        kernel,
        grid=(b, s // ts),
        in_specs=[row(RET_QW), pl.BlockSpec((1, RET_QW, ts), lambda bi, j: (bi, 0, j)),
                  row(RET_VW), row(RET_VW)] + [_const_spec(t.shape) for t in tables],
        out_specs=(row(RET_VW),
                   pl.BlockSpec((1, n_pair, LANES, RET_V_DIM), lambda bi, j: (bi, 0, 0, 0))),
        out_shape=(jax.ShapeDtypeStruct((b, s, RET_VW), BF16),
                   jax.ShapeDtypeStruct((b, n_pair, LANES, RET_V_DIM), F32)),
        scratch_shapes=[pltpu.VMEM((n_pair, LANES, RET_V_DIM), F32)],
        compiler_params=_params("parallel", "arbitrary"),
        name="prompt_ret",
    )(rq, rkt, rv, rg, *tables)
    return r, st.reshape(b, N_RET_HEADS, RET_QK_DIM, RET_V_DIM)


def _sret_kernel(rq_ref, rkt_ref, rv_ref, rg_ref, st_in_ref, dec_ref, qdec_ref, kdec_ref, gc_ref,
                 r_ref, st_out_ref, *, t_seq):
    n = LANES
    nb = n // t_seq
    lane = lax.broadcasted_iota(jnp.int32, (n, LANES), 1)
    tok_r = lax.broadcasted_iota(jnp.int32, (nb, n, LANES), 1) // t_seq
    seq_r = lax.broadcasted_iota(jnp.int32, (nb, n, LANES), 0)
    row_in_seq = tok_r == seq_r
    tok_l = lax.broadcasted_iota(jnp.int32, (nb, RET_QK_DIM, n), 2) // t_seq
    seq_l = lax.broadcasted_iota(jnp.int32, (nb, RET_QK_DIM, n), 0)
    lane_in_seq = tok_l == seq_l
    for hp in range(N_RET_HEADS // 2):
        cols = slice(hp * LANES, (hp + 1) * LANES)
        qp = rq_ref[0, :, cols]
        ktp = rkt_ref[0, cols, :]
        ktp16 = ktp.astype(BF16)
        stp = st_in_ref[:, hp]
        st16 = stp.astype(BF16).reshape(nb * LANES, RET_V_DIM)
        for e in range(2):
            h = hp * 2 + e
            hs = slice(e * RET_QK_DIM, (e + 1) * RET_QK_DIM)
            hv = slice(h * RET_V_DIM, (h + 1) * RET_V_DIM)
            in_head = (lane >= e * RET_QK_DIM) & (lane < (e + 1) * RET_QK_DIM)
            qh = jnp.where(in_head, qp, jnp.zeros_like(qp))
            sc = jnp.dot(qh, ktp16, preferred_element_type=F32) * dec_ref[h]
            vh = rv_ref[0, :, hv]
            inner = jnp.dot(sc.astype(BF16), vh, preferred_element_type=F32)
            qbd = jnp.where(row_in_seq, qh[None], jnp.zeros_like(qh)[None])
            qbd = jnp.concatenate([qbd[b] for b in range(nb)], axis=1)
            cross = jnp.dot(qbd, st16, preferred_element_type=F32) * qdec_ref[h]
            r_ref[0, :, hv] = _gated_norm(inner + cross, rg_ref[0, :, hv]).astype(r_ref.dtype)
            kd = (ktp[hs] * kdec_ref[h]).astype(BF16)
            kds = jnp.where(lane_in_seq, kd[None], jnp.zeros_like(kd)[None])
            upd = jnp.dot(kds.reshape(nb * RET_QK_DIM, n), vh, preferred_element_type=F32)
            st_out_ref[:, hp, hs, :] = (gc_ref[h] * stp[:, hs, :]
                                        + upd.reshape(nb, RET_QK_DIM, RET_V_DIM))


def _sample_retention(rq, rkt, rv, rg, state, t_seq):
    n_tok = rq.shape[1]
    db = state.shape[0]
    nb = LANES // t_seq
    n_pair = N_RET_HEADS // 2
    st_pairs = state.reshape(db, n_pair, LANES, RET_V_DIM)
    tables = _retention_tables(t_seq, LANES)
    row = lambda width: pl.BlockSpec((1, LANES, width), lambda j: (0, j, 0))
    st_spec = pl.BlockSpec((nb, n_pair, LANES, RET_V_DIM), lambda j: (j, 0, 0, 0))
    r, st = pl.pallas_call(
        functools.partial(_sret_kernel, t_seq=t_seq),
        grid=(n_tok // LANES,),
        in_specs=[row(RET_QW), pl.BlockSpec((1, RET_QW, LANES), lambda j: (0, 0, j)),
                  row(RET_VW), row(RET_VW), st_spec] + [_const_spec(t.shape) for t in tables],
        out_specs=(row(RET_VW), st_spec),
        out_shape=(jax.ShapeDtypeStruct((1, n_tok, RET_VW), BF16),
                   jax.ShapeDtypeStruct(st_pairs.shape, F32)),
        compiler_params=_params("parallel"),
        name="sample_ret",
    )(rq, rkt, rv, rg, st_pairs, *tables)
    return r, st.reshape(state.shape)


def _sattn_kernel(pt_ref, dl_ref, g_ref, qb_ref, kn_ref, vn_ref, *refs, n_pages, t_seq):
    del pt_ref
    k_refs = refs[:n_pages]
    v_refs = refs[n_pages:2 * n_pages]
    o_ref = refs[2 * n_pages]
    nh = N_DIFF_HEADS
    half = nh * t_seq
    qb = qb_ref[0].astype(F32)
    s = jnp.concatenate(
        [jnp.dot(qb, k_refs[p][0], preferred_element_type=F32) for p in range(n_pages)],
        axis=1)
    kn = kn_ref[0].astype(F32)
    t_row = lax.broadcasted_iota(jnp.int32, (2 * half, 1), 0) % t_seq
    s_new = []
    for tk in range(t_seq):
        col = jnp.sum(qb * kn[tk:tk + 1, :], axis=-1, keepdims=True)
        s_new.append(jnp.where(tk <= t_row, col, MASK_VALUE))
    m = jnp.max(s, axis=-1, keepdims=True)
    for col in s_new:
        m = jnp.maximum(m, col)
    p = jnp.exp2(s - m)
    p_new = [jnp.exp2(col - m) for col in s_new]
    l = jnp.sum(p, axis=-1, keepdims=True)
    for col in p_new:
        l = l + col
    lam = _lambda_full(dl_ref)
    a = (p[:half] / l[:half] - lam * (p[half:] / l[half:])).astype(BF16)
    a_new = [(c[:half] / l[:half] - lam * (c[half:] / l[half:])).astype(BF16).astype(F32)
             for c in p_new]
    page = k_refs[0].shape[2]
    a_pages = jnp.concatenate([a[:, pg * page:(pg + 1) * page] for pg in range(n_pages)],
                              axis=0)
    spread = (lax.broadcasted_iota(jnp.int32, (page, page * nh), 1) // nh
              == lax.broadcasted_iota(jnp.int32, (page, page * nh), 0)).astype(BF16)
    ax = jnp.dot(a_pages, spread, preferred_element_type=F32)
    col_head = lax.broadcasted_iota(jnp.int32, ax.shape, 1) % nh
    row_head = (lax.broadcasted_iota(jnp.int32, ax.shape, 0) % half) // t_seq
    ax = jnp.where(col_head == row_head, ax, 0.0)
    acc = jnp.zeros((half, DIFF_V_DIM), F32)
    for pg in range(n_pages):
        acc = acc + jnp.dot(ax[pg * half:(pg + 1) * half], v_refs[pg][0],
                            preferred_element_type=F32)
    vn = vn_ref[0].astype(F32)
    for h in range(nh):
        hv = slice(h * DIFF_V_DIM, (h + 1) * DIFF_V_DIM)
        rows = slice(h * t_seq, (h + 1) * t_seq)
        o = acc[rows]
        for tk in range(t_seq):
            o = o + a_new[tk][rows] * vn[tk:tk + 1, hv]
        o_ref[0, :, hv] = _subln(o, g_ref[...]).astype(o_ref.dtype)


def _sample_attention(page_table, dl, subln_g, qblk, k_new16, v_new16, kc, vc):
    db, n_pages = page_table.shape
    t_seq = k_new16.shape[1]
    page = kc.shape[2]
    kernel = functools.partial(_sattn_kernel, n_pages=n_pages, t_seq=t_seq)

    def k_spec(p):
        return pl.BlockSpec((1, DIFF_W, page), lambda b, pt: (pt[b * n_pages + p], 0, 0))

    def v_spec(p):
        return pl.BlockSpec((1, page * N_DIFF_HEADS, DIFF_V_DIM),
                            lambda b, pt: (pt[b * n_pages + p], 0, 0))

    per_b = lambda shape: pl.BlockSpec((1,) + shape, lambda b, pt: (b, 0, 0))
    grid_spec = pltpu.PrefetchScalarGridSpec(
        num_scalar_prefetch=1,
        grid=(db,),
        in_specs=[pl.BlockSpec(dl.shape, lambda b, pt: (0, 0)),
                  pl.BlockSpec(subln_g.shape, lambda b, pt: (0, 0)),
                  per_b(qblk.shape[1:]), per_b(k_new16.shape[1:]), per_b(v_new16.shape[1:])]
                 + [k_spec(p) for p in range(n_pages)]
                 + [v_spec(p) for p in range(n_pages)],
        out_specs=per_b((t_seq, DIFF_VW)),
    )
    return pl.pallas_call(
        kernel,
        grid_spec=grid_spec,
        out_shape=jax.ShapeDtypeStruct((db, t_seq, DIFF_VW), BF16),
        compiler_params=_params("parallel"),
        name="sample_attn",
    )(page_table.reshape(-1), dl, subln_g, qblk, k_new16, v_new16,
      *([kc] * n_pages), *([vc] * n_pages))


def _ffn_kernel(x_ref, d_ref, r_ref, wo_ref, gf_ref, wi_ref, wo2_ref, gl_ref, y_ref, act, *, fc):
    d_ff = wo2_ref.shape[0]
    mix = (jnp.dot(d_ref[...], wo_ref[0:DIFF_VW, :], preferred_element_type=F32)
           + jnp.dot(r_ref[...], wo_ref[DIFF_VW:, :], preferred_element_type=F32))
    x1 = x_ref[...] + mix
    hb = (_rms(x1, NORM_EPS) * gf_ref[...]).astype(BF16)
    for c in range(d_ff // fc):
        g = jnp.dot(hb, wi_ref[:, c * fc:(c + 1) * fc], preferred_element_type=F32)
        u = jnp.dot(hb, wi_ref[:, d_ff + c * fc:d_ff + (c + 1) * fc], preferred_element_type=F32)
        act[:, c * fc:(c + 1) * fc] = (jax.nn.silu(g) * u).astype(BF16)
    x2 = x1 + jnp.dot(act[...], wo2_ref[...], preferred_element_type=F32)
    y_ref[...] = _rms(x2, NORM_EPS) * gl_ref[...]


def _merge_ffn(x, d, r, w_out, g_ffn, w_ffn_in, w_ffn_out, g_final, tm, fc):
    n, dm = x.shape
    d_ff = w_ffn_out.shape[0]
    row = lambda width: pl.BlockSpec((tm, width), lambda i: (i, 0))
    return pl.pallas_call(
        functools.partial(_ffn_kernel, fc=fc),
        grid=(n // tm,),
        in_specs=[row(dm), row(DIFF_VW), row(RET_VW), _const_spec(w_out.shape),
                  _const_spec(g_ffn.shape), _const_spec(w_ffn_in.shape),
                  _const_spec(w_ffn_out.shape), _const_spec(g_final.shape)],
        out_specs=row(dm),
        out_shape=jax.ShapeDtypeStruct((n, dm), F32),
        scratch_shapes=[pltpu.VMEM((tm, d_ff), BF16)],
        compiler_params=_params("parallel"),
        name="merge_ffn",
    )(x, d, r, w_out, g_ffn, w_ffn_in, w_ffn_out, g_final)


def _ffn_chunk(d_ff):
    for fc in (512, 256, 128):
        if d_ff % fc == 0:
            return fc
    return d_ff


def kernel(x_prompt, x_sample, cache_diff_k, cache_diff_v, state_ret, page_table, norm_mix_g, w_in, diff_lambda, diff_subln_g, w_out, norm_ffn_g, w_ffn_in, w_ffn_out, norm_final_g):
    bsz, seq, dm = x_prompt.shape
    db, t_seq, _ = x_sample.shape
    n_pages = page_table.shape[1]
    page = cache_diff_k.shape[2]
    past = n_pages * page
    assert w_in.shape[0] == 1, "single layer"
    assert LANES % t_seq == 0 and (db * t_seq) % LANES == 0

    w_in16 = w_in[0].astype(BF16)
    w_out16 = w_out[0].astype(BF16)
    w_ffn_in16 = w_ffn_in[0].astype(BF16)
    w_ffn_out16 = w_ffn_out[0].astype(BF16)
    g_mix, g_ffn = norm_mix_g, norm_ffn_g
    g_final = norm_final_g.reshape(1, dm)
    dl = diff_lambda[0]
    fc = _ffn_chunk(w_ffn_out.shape[1])

    tm = min(512, seq)
    tabs_p = _rotary_tables(jnp.arange(seq))
    qt16, k16, kt32, v32, vt16, rq, rkt, rv, rg = _inproj(x_prompt, g_mix, w_in16, tabs_p, tm)
    d_p = _prompt_attention(dl, diff_subln_g, qt16, k16, vt16, min(1024, seq), min(512, seq))
    r_p, ret_state_p = _prompt_retention(rq, rkt, rv, rg, min(512, seq))
    y_prompt = _merge_ffn(x_prompt.reshape(bsz * seq, dm), d_p.reshape(bsz * seq, DIFF_VW),
                          r_p.reshape(bsz * seq, RET_VW), w_out16, g_ffn, w_ffn_in16,
                          w_ffn_out16, g_final, tm, fc).reshape(bsz, seq, dm)
    k_prompt = jnp.swapaxes(kt32, 1, 2).reshape(1, bsz, seq, N_DIFF_HEADS, 2, DIFF_QK_DIM)
    v_prompt = v32.reshape(1, bsz, seq, N_DIFF_HEADS, DIFF_V_DIM)
    ret_prompt = ret_state_p[None]

    n_s = db * t_seq
    pos_s = past + (jnp.arange(n_s) % t_seq)
    tabs_s = _rotary_tables(pos_s)
    tm_s = min(512, n_s)
    qts, ks16, kts32, vs32, _, rqs, rkts, rvs, rgs = _inproj(
        x_sample.reshape(1, n_s, dm), g_mix, w_in16, tabs_s, tm_s)
    k_s = kts32[0].T
    qs = qts[0].T
    r_idx = np.arange(2 * N_DIFF_HEADS * t_seq)
    r_map, r_head, r_tok = r_idx // (N_DIFF_HEADS * t_seq), (r_idx // t_seq) % N_DIFF_HEADS, r_idx % t_seq
    feat_owner = np.arange(DIFF_W) // DIFF_QK_DIM
    sel = (feat_owner[None, :] == (r_head * 2 + r_map)[:, None])
    qs3 = qs.reshape(db, t_seq, DIFF_W)
    qblk = jnp.where(sel[None], qs3[:, r_tok, :], jnp.zeros((), BF16))
    kc = jnp.transpose(cache_diff_k[0], (0, 2, 3, 4, 1)).reshape(-1, DIFF_W, page)
    d_s = _sample_attention(page_table, dl, diff_subln_g, qblk,
                            ks16.reshape(db, t_seq, DIFF_W),
                            vs32.astype(BF16).reshape(db, t_seq, DIFF_VW), kc,
                            cache_diff_v[0].reshape(-1, page * N_DIFF_HEADS, DIFF_V_DIM))
    r_s, ret_state_s = _sample_retention(rqs, rkts, rvs, rgs, state_ret[0], t_seq)
    y_sample = _merge_ffn(x_sample.reshape(n_s, dm), d_s.reshape(n_s, DIFF_VW),
                          r_s.reshape(n_s, RET_VW), w_out16, g_ffn, w_ffn_in16,
                          w_ffn_out16, g_final, tm_s, fc).reshape(db, t_seq, dm)
    k_sample = k_s.reshape(1, db, t_seq, N_DIFF_HEADS, 2, DIFF_QK_DIM)
    v_sample = vs32.reshape(1, db, t_seq, N_DIFF_HEADS, DIFF_V_DIM)
    ret_sample = ret_state_s[None]

    return (y_prompt, y_sample, k_prompt, v_prompt, ret_prompt, k_sample, v_sample, ret_sample)
```

```python
import functools
import math

import jax
import jax.numpy as jnp
import numpy as np
from jax import lax
from jax.experimental import pallas as pl
from jax.experimental.pallas import tpu as pltpu

F32 = jnp.float32
BF16 = jnp.bfloat16

N_DIFF_HEADS = 4
DIFF_QK_DIM = 64
DIFF_V_DIM = 128
ROT_DIM = 16
ROPE_THETA = 500000.0
N_RET_HEADS = 4
RET_QK_DIM = 64
RET_V_DIM = 128
RET_THETA = 10000.0
RET_CHUNK = 128
NORM_EPS = 1e-6
SUBLN_EPS = 1e-5
LAM_INIT = 0.8 - 0.6 * math.exp(-0.3 * 0)
MASK_VALUE = -1e30
Q_SCALE = DIFF_QK_DIM ** -0.5 * math.log2(math.e)

DIFF_W = N_DIFF_HEADS * 2 * DIFF_QK_DIM
DIFF_VW = N_DIFF_HEADS * DIFF_V_DIM
RET_QW = N_RET_HEADS * RET_QK_DIM
RET_VW = N_RET_HEADS * RET_V_DIM
W_IN_OFFSETS = tuple(int(o) for o in np.cumsum([0, DIFF_W, DIFF_W, DIFF_VW, RET_QW, RET_QW, RET_VW]))

LANES = 128
ACC_PAD = 16
VMEM_LIMIT = 56 * 1024 * 1024


def _params(*sem, flags=None):
    return pltpu.CompilerParams(dimension_semantics=sem, vmem_limit_bytes=VMEM_LIMIT, flags=flags)


def _const_spec(shape):
    nd = len(shape)
    return pl.BlockSpec(shape, lambda *_: (0,) * nd, pipeline_mode=pl.Buffered(1))


def _rms(x, eps):
    return x * lax.rsqrt(jnp.mean(x * x, axis=-1, keepdims=True) + eps)


def _inv_freq(freq_idx, dim, theta):
    return 1.0 / (jnp.float32(theta) ** (jnp.asarray(2 * freq_idx, F32) / dim))


def _rotary_tables(pos):
    posf = pos.astype(F32)

    def feature_tables(half, dim, theta, scale):
        ang = _inv_freq(np.arange(half), dim, theta)[:, None] * posf[None, :]
        return jnp.cos(ang) * scale, jnp.sin(ang) * scale

    return (feature_tables(ROT_DIM // 2, ROT_DIM, ROPE_THETA, Q_SCALE)
            + feature_tables(ROT_DIM // 2, ROT_DIM, ROPE_THETA, 1.0)
            + feature_tables(RET_QK_DIM // 2, RET_QK_DIM, RET_THETA, 1.0)
            + feature_tables(RET_QK_DIM // 2, RET_QK_DIM, RET_THETA, RET_QK_DIM ** -0.5))


def _log_gamma():
    return jnp.log(1.0 - 2.0 ** (-5.0 - jnp.arange(N_RET_HEADS, dtype=F32)))


def _retention_tables(chunk, n_tok):
    lg = _log_gamma()
    idx = jnp.arange(n_tok)
    loc = (idx % chunk).astype(F32)
    rel = loc[:, None] - loc[None, :]
    same = (idx[:, None] // chunk) == (idx[None, :] // chunk)
    decay = jnp.where(same[None] & (rel >= 0)[None],
                      jnp.exp(lg[:, None, None] * jnp.maximum(rel, 0.0)[None]), 0.0)
    qdec = jnp.exp(lg[:, None] * (loc[None, :] + 1.0))[:, :, None]
    kdec = jnp.exp(lg[:, None] * (chunk - 1.0 - loc[None, :]))[:, None, :]
    gc = jnp.broadcast_to(jnp.exp(lg * chunk)[:, None, None], (N_RET_HEADS, 1, LANES))
    return decay, qdec, kdec, gc


def _inproj_kernel(x_ref, g_ref, w_ref,
                   cqt_ref, sqt_ref, ckt_ref, skt_ref, crqt_ref, srqt_ref, crkt_ref, srkt_ref,
                   qt_ref, k16_ref, kt32_ref, v32_ref, vt16_ref,
                   rq_ref, rkt_ref, rv_ref, rg_ref, rqt_scr):
    x = x_ref[0]
    hb = (_rms(x, NORM_EPS) * g_ref[...]).astype(BF16)

    def mm(lo, width):
        return jnp.dot(hb, w_ref[:, lo:lo + width], preferred_element_type=F32)

    def rot_t(zt, base, width, half, c, s, rest_scale=None):
        x1, x2 = zt[base:base + half], zt[base + half:base + 2 * half]
        parts = [x1 * c - x2 * s, x2 * c + x1 * s]
        if 2 * half < width:
            rest = zt[base + 2 * half:base + width]
            parts.append(rest if rest_scale is None else rest * rest_scale)
        return jnp.concatenate(parts, axis=0)

    o_dq, o_dk, o_dv, o_rq, o_rk, o_rv, o_rg = W_IN_OFFSETS
    hr = ROT_DIM // 2
    zt = mm(o_dq, DIFF_W).T
    c, s = cqt_ref[...], sqt_ref[...]
    for g in range(N_DIFF_HEADS * 2):
        b = g * DIFF_QK_DIM
        qt_ref[0, b:b + DIFF_QK_DIM, :] = rot_t(
            zt, b, DIFF_QK_DIM, hr, c, s, rest_scale=Q_SCALE).astype(BF16)
    zt = mm(o_dk, DIFF_W).T
    c, s = ckt_ref[...], skt_ref[...]
    for g in range(N_DIFF_HEADS * 2):
        b = g * DIFF_QK_DIM
        kt32_ref[0, b:b + DIFF_QK_DIM, :] = rot_t(zt, b, DIFF_QK_DIM, hr, c, s)
    k16_ref[0] = kt32_ref[0].T.astype(BF16)
    zv = mm(o_dv, DIFF_VW)
    for h in range(N_DIFF_HEADS):
        v32_ref[0, pl.ds(h, zv.shape[0], stride=N_DIFF_HEADS), :] = (
            zv[:, h * DIFF_V_DIM:(h + 1) * DIFF_V_DIM])
    vt16_ref[0] = zv.T.astype(BF16)
    zt = mm(o_rq, RET_QW).T
    c, s = crqt_ref[...], srqt_ref[...]
    for h in range(N_RET_HEADS):
        b = h * RET_QK_DIM
        rqt_scr[b:b + RET_QK_DIM, :] = rot_t(zt, b, RET_QK_DIM, RET_QK_DIM // 2, c, s)
    rq_ref[0] = rqt_scr[...].T.astype(BF16)
    zt = mm(o_rk, RET_QW).T
    c, s = crkt_ref[...], srkt_ref[...]
    for h in range(N_RET_HEADS):
        b = h * RET_QK_DIM
        rkt_ref[0, b:b + RET_QK_DIM, :] = rot_t(zt, b, RET_QK_DIM, RET_QK_DIM // 2, c, s)
    rv_ref[0] = mm(o_rv, RET_VW).astype(BF16)
    rg_ref[0] = mm(o_rg, RET_VW)


def _inproj(x, g, w16, tables, tm):
    b, s, d = x.shape
    row = lambda width: pl.BlockSpec((1, tm, width), lambda bi, i: (bi, i, 0))
    col = lambda height: pl.BlockSpec((1, height, tm), lambda bi, i: (bi, 0, i))
    ttab = lambda height: pl.BlockSpec((height, tm), lambda bi, i: (0, i))
    out_shape = (
        jax.ShapeDtypeStruct((b, DIFF_W, s), BF16),
        jax.ShapeDtypeStruct((b, s, DIFF_W), BF16),
        jax.ShapeDtypeStruct((b, DIFF_W, s), F32),
        jax.ShapeDtypeStruct((b, s * N_DIFF_HEADS, DIFF_V_DIM), F32),
        jax.ShapeDtypeStruct((b, DIFF_VW, s), BF16),
        jax.ShapeDtypeStruct((b, s, RET_QW), BF16),
        jax.ShapeDtypeStruct((b, RET_QW, s), F32),
        jax.ShapeDtypeStruct((b, s, RET_VW), BF16),
        jax.ShapeDtypeStruct((b, s, RET_VW), F32),
    )
    v_rows = pl.BlockSpec((1, tm * N_DIFF_HEADS, DIFF_V_DIM), lambda bi, i: (bi, i, 0))
    out_specs = (col(DIFF_W), row(DIFF_W), col(DIFF_W), v_rows, col(DIFF_VW),
                 row(RET_QW), col(RET_QW), row(RET_VW), row(RET_VW))
    return pl.pallas_call(
        _inproj_kernel,
        grid=(b, s // tm),
        in_specs=[row(d), _const_spec((1, d)), _const_spec(w16.shape)]
                 + [ttab(ROT_DIM // 2)] * 4 + [ttab(RET_QK_DIM // 2)] * 4,
        out_specs=out_specs,
        out_shape=out_shape,
        scratch_shapes=[pltpu.VMEM((RET_QW, tm), F32)],
        compiler_params=_params("parallel", "parallel"),
        name="inproj",
    )(x, g, w16, *tables)


def _lambda_full(dl_ref):
    lp = dl_ref[...]
    a = jnp.sum(lp[0:1] * lp[1:2], axis=-1, keepdims=True)
    b = jnp.sum(lp[2:3] * lp[3:4], axis=-1, keepdims=True)
    return jnp.exp(a) - jnp.exp(b) + LAM_INIT


def _subln(o, g):
    return _rms(o, SUBLN_EPS) * g * (1.0 - LAM_INIT)


def _attn_kernel(dl_ref, g_ref, qt_ref, qt_next_ref, k_ref, vt_ref, o_ref, acc, s_scr, p_scr,
                 *, tq, tk):
    i = pl.program_id(2)
    first_step = (pl.program_id(0) == 0) & (pl.program_id(1) == 0) & (i == 0)

    def score_columns(qt):
        row = lax.broadcasted_iota(jnp.int32, (qt.shape[0], tk), 0)
        groups = []
        for half in range(2):
            qh = qt[:, half * tk:(half + 1) * tk]
            groups += [jnp.where(row < DIFF_QK_DIM, qh, jnp.zeros_like(qh)),
                       jnp.where(row >= DIFF_QK_DIM, qh, jnp.zeros_like(qh))]
        return jnp.concatenate(groups, axis=1)

    qtb = score_columns(qt_ref[0])
    acc[...] = jnp.zeros_like(acc)
    n_full = 2 * i
    all_cols, lo, hi = slice(0, 2 * tq), slice(0, tq), slice(tq, 2 * tq)

    def issue_scores(j, par, cols, q=None):
        off = pl.multiple_of(j * tk, tk)
        q = qtb if q is None else q
        s_scr[par, :, cols] = jnp.dot(k_ref[0, pl.ds(off, tk), :], q[:, cols],
                                      preferred_element_type=F32)

    def softmax_stage(par, m, cols, causal):
        s = s_scr[par, :, cols]
        if causal:
            c = lax.broadcasted_iota(jnp.int32, (tk, tq), 1)
            c = jnp.where(c >= tk, c - tk, c)
            s = jnp.where(lax.broadcasted_iota(jnp.int32, (tk, tq), 0) <= c, s, MASK_VALUE)
        m_new = jnp.maximum(m, jnp.max(s, axis=0, keepdims=True))
        alpha = jnp.exp2(m - m_new)
        p_scr[par, :, cols] = jnp.exp2(s - m_new).astype(BF16)
        return m_new, alpha

    ones_rows = (lax.broadcasted_iota(jnp.int32, (ACC_PAD, tk), 0) == 0).astype(BF16)

    def value_stage(j, par, alpha, cols):
        off = pl.multiple_of(j * tk, tk)
        vt = jnp.concatenate([vt_ref[0, :, pl.ds(off, tk)], ones_rows], axis=0)
        acc[:, cols] = alpha * acc[:, cols] + jnp.dot(vt, p_scr[par, :, cols],
                                                      preferred_element_type=F32)

    def substep(j, par, m, alpha_prev):
        issue_scores(j + 1, 1 - par, all_cols)
        value_stage(jnp.maximum(j - 1, 0), 1 - par, alpha_prev, all_cols)
        return softmax_stage(par, m, all_cols, False)

    def body(t, carry):
        m, alpha = substep(2 * t, 0, *carry)
        return substep(2 * t + 1, 1, m, alpha)

    @pl.when(i == 0)
    def _():
        issue_scores(0, 0, all_cols)

    @pl.when(first_step)
    def _():
        p_scr[1] = jnp.zeros((tk, 2 * tq), BF16)

    carry = (jnp.full((1, 2 * tq), MASK_VALUE, F32), jnp.ones((1, 2 * tq), F32))
    m, alpha = lax.fori_loop(0, i, body, carry)
    issue_scores(n_full + 1, 1, hi)
    value_stage(jnp.maximum(n_full - 1, 0), 1, alpha, all_cols)
    _, alpha_lo = softmax_stage(0, m[:, lo], lo, True)
    m_hi, alpha_hi = softmax_stage(0, m[:, hi], hi, False)
    value_stage(n_full, 0, jnp.concatenate([alpha_lo, alpha_hi], axis=1), all_cols)
    _, alpha_hi = softmax_stage(1, m_hi, hi, True)
    value_stage(n_full + 1, 1, alpha_hi, hi)
    issue_scores(0, 0, all_cols, q=score_columns(qt_next_ref[0]))
    p_scr[1] = jnp.zeros((tk, 2 * tq), BF16)

    lam = _lambda_full(dl_ref)
    out = []
    for half in range(2):
        c0 = slice(2 * half * tk, (2 * half + 1) * tk)
        c1 = slice((2 * half + 1) * tk, (2 * half + 2) * tk)
        out.append(acc[0:DIFF_V_DIM, c0] / acc[DIFF_V_DIM:DIFF_V_DIM + 1, c0]
                   - lam * (acc[0:DIFF_V_DIM, c1] / acc[DIFF_V_DIM:DIFF_V_DIM + 1, c1]))
    ot = jnp.concatenate(out, axis=1)
    o_ref[0] = _subln(ot.T, g_ref[...]).astype(o_ref.dtype)


def _prompt_attention(dl, subln_g, qt16, k16, vt16, tq, tk):
    b, s, _ = k16.shape
    assert tq == 2 * tk, "a query block spans two key blocks (two scratch slots)"
    n_q = s // tq
    kernel = functools.partial(_attn_kernel, tq=tq, tk=tk)
    return pl.pallas_call(
        kernel,
        grid=(b, N_DIFF_HEADS, s // tq),
        in_specs=[
            _const_spec(dl.shape),
            _const_spec(subln_g.shape),
            pl.BlockSpec((1, LANES, tq), lambda bi, h, i: (bi, h, i)),
            pl.BlockSpec((1, LANES, tq), lambda bi, h, i: (bi, h, jnp.minimum(i + 1, n_q - 1))),
            pl.BlockSpec((1, s, LANES), lambda bi, h, i: (bi, 0, h)),
            pl.BlockSpec((1, LANES, s), lambda bi, h, i: (bi, h, 0)),
        ],
        out_specs=pl.BlockSpec((1, tq, LANES), lambda bi, h, i: (bi, i, h)),
        out_shape=jax.ShapeDtypeStruct((b, s, DIFF_VW), BF16),
        scratch_shapes=[pltpu.VMEM((DIFF_V_DIM + ACC_PAD, 2 * tq), F32),
                        pltpu.VMEM((2, tk, 2 * tq), F32),
                        pltpu.VMEM((2, tk, 2 * tq), BF16)],
        compiler_params=_params("arbitrary", "arbitrary", "arbitrary"),
        name="prompt_attn",
    )(dl, subln_g, qt16, qt16, k16, vt16)


def _gated_norm(o, gate):
    return jax.nn.silu(gate) * _rms(o, NORM_EPS)


def _ret_kernel(rq_ref, rkt_ref, rv_ref, rg_ref, dec_ref, qdec_ref, kdec_ref, gc_ref,
                r_ref, st_ref, state, *, n_chunks):
    j = pl.program_id(1)

    @pl.when(j == 0)
    def _():
        state[...] = jnp.zeros_like(state)

    c = RET_CHUNK
    lane = lax.broadcasted_iota(jnp.int32, (c, LANES), 1)
    n_pair = N_RET_HEADS // 2
    st = [state[hp] for hp in range(n_pair)]
    for ci in range(n_chunks):
        rows = slice(ci * c, (ci + 1) * c)
        for hp in range(n_pair):
            cols = slice(hp * LANES, (hp + 1) * LANES)
            qp = rq_ref[0, rows, cols]
            ktp = rkt_ref[0, cols, rows]
            ktp16 = ktp.astype(BF16)
            stp = st[hp]
            stp16 = stp.astype(BF16)
            q2 = jnp.concatenate(
                [jnp.where((lane >= e * RET_QK_DIM) & (lane < (e + 1) * RET_QK_DIM),
                           qp, jnp.zeros_like(qp)) for e in range(2)], axis=0)
            sc2 = jnp.dot(q2, ktp16, preferred_element_type=F32)
            cross2 = jnp.dot(q2, stp16, preferred_element_type=F32)
            new = []
            for e in range(2):
                h = hp * 2 + e
                hs = slice(e * RET_QK_DIM, (e + 1) * RET_QK_DIM)
                hv = slice(h * RET_V_DIM, (h + 1) * RET_V_DIM)
                sc = sc2[e * c:(e + 1) * c] * dec_ref[h]
                vh = rv_ref[0, rows, hv]
                kd = (ktp[hs] * kdec_ref[h]).astype(BF16)
                both = jnp.dot(jnp.concatenate([sc.astype(BF16), kd], axis=0), vh,
                               preferred_element_type=F32)
                cross = cross2[e * c:(e + 1) * c] * qdec_ref[h]
                r_ref[0, rows, hv] = _gated_norm(both[:c] + cross, rg_ref[0, rows, hv]).astype(r_ref.dtype)
                new.append(gc_ref[h] * stp[hs] + both[c:])
            st[hp] = jnp.concatenate(new, axis=0)
    for hp in range(n_pair):
        state[hp] = st[hp]

    @pl.when(j == pl.num_programs(1) - 1)
    def _():
        for hp in range(n_pair):
            st_ref[0, hp] = st[hp]


def _prompt_retention(rq, rkt, rv, rg, ts):
    b, s, _ = rq.shape
    tables = _retention_tables(RET_CHUNK, RET_CHUNK)
    kernel = functools.partial(_ret_kernel, n_chunks=ts // RET_CHUNK)
    row = lambda width: pl.BlockSpec((1, ts, width), lambda bi, j: (bi, j, 0))
    n_pair = N_RET_HEADS // 2
    r, st = pl.pallas_call(
        kernel,
        grid=(b, s // ts),
        in_specs=[row(RET_QW), pl.BlockSpec((1, RET_QW, ts), lambda bi, j: (bi, 0, j)),
                  row(RET_VW), row(RET_VW)] + [_const_spec(t.shape) for t in tables],
        out_specs=(row(RET_VW),
                   pl.BlockSpec((1, n_pair, LANES, RET_V_DIM), lambda bi, j: (bi, 0, 0, 0))),
        out_shape=(jax.ShapeDtypeStruct((b, s, RET_VW), BF16),
                   jax.ShapeDtypeStruct((b, n_pair, LANES, RET_V_DIM), F32)),
        scratch_shapes=[pltpu.VMEM((n_pair, LANES, RET_V_DIM), F32)],
        compiler_params=_params("parallel", "arbitrary"),
        name="prompt_ret",
    )(rq, rkt, rv, rg, *tables)
    return r, st.reshape(b, N_RET_HEADS, RET_QK_DIM, RET_V_DIM)


def _sret_kernel(rq_ref, rkt_ref, rv_ref, rg_ref, st_in_ref, dec_ref, qdec_ref, kdec_ref, gc_ref,
                 r_ref, st_out_ref, *, t_seq):
    n = LANES
    nb = n // t_seq
    lane = lax.broadcasted_iota(jnp.int32, (n, LANES), 1)
    tok_r = lax.broadcasted_iota(jnp.int32, (nb, n, LANES), 1) // t_seq
    seq_r = lax.broadcasted_iota(jnp.int32, (nb, n, LANES), 0)
    row_in_seq = tok_r == seq_r
    tok_l = lax.broadcasted_iota(jnp.int32, (nb, RET_QK_DIM, n), 2) // t_seq
    seq_l = lax.broadcasted_iota(jnp.int32, (nb, RET_QK_DIM, n), 0)
    lane_in_seq = tok_l == seq_l
    for hp in range(N_RET_HEADS // 2):
        cols = slice(hp * LANES, (hp + 1) * LANES)
        qp = rq_ref[0, :, cols]
        ktp = rkt_ref[0, cols, :]
        ktp16 = ktp.astype(BF16)
        stp = st_in_ref[:, hp]
        st16 = stp.astype(BF16).reshape(nb * LANES, RET_V_DIM)
        for e in range(2):
            h = hp * 2 + e
            hs = slice(e * RET_QK_DIM, (e + 1) * RET_QK_DIM)
            hv = slice(h * RET_V_DIM, (h + 1) * RET_V_DIM)
            in_head = (lane >= e * RET_QK_DIM) & (lane < (e + 1) * RET_QK_DIM)
            qh = jnp.where(in_head, qp, jnp.zeros_like(qp))
            sc = jnp.dot(qh, ktp16, preferred_element_type=F32) * dec_ref[h]
            vh = rv_ref[0, :, hv]
            inner = jnp.dot(sc.astype(BF16), vh, preferred_element_type=F32)
            qbd = jnp.where(row_in_seq, qh[None], jnp.zeros_like(qh)[None])
            qbd = jnp.concatenate([qbd[b] for b in range(nb)], axis=1)
            cross = jnp.dot(qbd, st16, preferred_element_type=F32) * qdec_ref[h]
            r_ref[0, :, hv] = _gated_norm(inner + cross, rg_ref[0, :, hv]).astype(r_ref.dtype)
            kd = (ktp[hs] * kdec_ref[h]).astype(BF16)
            kds = jnp.where(lane_in_seq, kd[None], jnp.zeros_like(kd)[None])
            upd = jnp.dot(kds.reshape(nb * RET_QK_DIM, n), vh, preferred_element_type=F32)
            st_out_ref[:, hp, hs, :] = (gc_ref[h] * stp[:, hs, :]
                                        + upd.reshape(nb, RET_QK_DIM, RET_V_DIM))


def _sample_retention(rq, rkt, rv, rg, state, t_seq):
    n_tok = rq.shape[1]
    db = state.shape[0]
    nb = LANES // t_seq
    n_pair = N_RET_HEADS // 2
    st_pairs = state.reshape(db, n_pair, LANES, RET_V_DIM)
    tables = _retention_tables(t_seq, LANES)
    row = lambda width: pl.BlockSpec((1, LANES, width), lambda j: (0, j, 0))
    st_spec = pl.BlockSpec((nb, n_pair, LANES, RET_V_DIM), lambda j: (j, 0, 0, 0))
    r, st = pl.pallas_call(
        functools.partial(_sret_kernel, t_seq=t_seq),
        grid=(n_tok // LANES,),
        in_specs=[row(RET_QW), pl.BlockSpec((1, RET_QW, LANES), lambda j: (0, 0, j)),
                  row(RET_VW), row(RET_VW), st_spec] + [_const_spec(t.shape) for t in tables],
        out_specs=(row(RET_VW), st_spec),
        out_shape=(jax.ShapeDtypeStruct((1, n_tok, RET_VW), BF16),
                   jax.ShapeDtypeStruct(st_pairs.shape, F32)),
        compiler_params=_params("parallel"),
        name="sample_ret",
    )(rq, rkt, rv, rg, st_pairs, *tables)
    return r, st.reshape(state.shape)


def _sattn_kernel(pt_ref, dl_ref, g_ref, qb_ref, kn_ref, vn_ref, *refs, n_pages, t_seq):
    del pt_ref
    k_refs = refs[:n_pages]
    v_refs = refs[n_pages:2 * n_pages]
    o_ref = refs[2 * n_pages]
    nh = N_DIFF_HEADS
    half = nh * t_seq
    qb = qb_ref[0].astype(F32)
    s = jnp.concatenate(
        [jnp.dot(qb, k_refs[p][0], preferred_element_type=F32) for p in range(n_pages)],
        axis=1)
    kn = kn_ref[0].astype(F32)
    t_row = lax.broadcasted_iota(jnp.int32, (2 * half, 1), 0) % t_seq
    s_new = []
    for tk in range(t_seq):
        col = jnp.sum(qb * kn[tk:tk + 1, :], axis=-1, keepdims=True)
        s_new.append(jnp.where(tk <= t_row, col, MASK_VALUE))
    m = jnp.max(s, axis=-1, keepdims=True)
    for col in s_new:
        m = jnp.maximum(m, col)
    p = jnp.exp2(s - m)
    p_new = [jnp.exp2(col - m) for col in s_new]
    l = jnp.sum(p, axis=-1, keepdims=True)
    for col in p_new:
        l = l + col
    lam = _lambda_full(dl_ref)
    a = (p[:half] / l[:half] - lam * (p[half:] / l[half:])).astype(BF16)
    a_new = [(c[:half] / l[:half] - lam * (c[half:] / l[half:])).astype(BF16).astype(F32)
             for c in p_new]
    page = k_refs[0].shape[2]
    a_pages = jnp.concatenate([a[:, pg * page:(pg + 1) * page] for pg in range(n_pages)],
                              axis=0)
    spread = (lax.broadcasted_iota(jnp.int32, (page, page * nh), 1) // nh
              == lax.broadcasted_iota(jnp.int32, (page, page * nh), 0)).astype(BF16)
    ax = jnp.dot(a_pages, spread, preferred_element_type=F32)
    col_head = lax.broadcasted_iota(jnp.int32, ax.shape, 1) % nh
    row_head = (lax.broadcasted_iota(jnp.int32, ax.shape, 0) % half) // t_seq
    ax = jnp.where(col_head == row_head, ax, 0.0)
    acc = jnp.zeros((half, DIFF_V_DIM), F32)
    for pg in range(n_pages):
        acc = acc + jnp.dot(ax[pg * half:(pg + 1) * half], v_refs[pg][0],
                            preferred_element_type=F32)
    vn = vn_ref[0].astype(F32)
    for h in range(nh):
        hv = slice(h * DIFF_V_DIM, (h + 1) * DIFF_V_DIM)
        rows = slice(h * t_seq, (h + 1) * t_seq)
        o = acc[rows]
        for tk in range(t_seq):
            o = o + a_new[tk][rows] * vn[tk:tk + 1, hv]
        o_ref[0, :, hv] = _subln(o, g_ref[...]).astype(o_ref.dtype)


def _sample_attention(page_table, dl, subln_g, qblk, k_new16, v_new16, kc, vc):
    db, n_pages = page_table.shape
    t_seq = k_new16.shape[1]
    page = kc.shape[2]
    kernel = functools.partial(_sattn_kernel, n_pages=n_pages, t_seq=t_seq)

    def k_spec(p):
        return pl.BlockSpec((1, DIFF_W, page), lambda b, pt: (pt[b * n_pages + p], 0, 0))

    def v_spec(p):
        return pl.BlockSpec((1, page * N_DIFF_HEADS, DIFF_V_DIM),
                            lambda b, pt: (pt[b * n_pages + p], 0, 0))

    per_b = lambda shape: pl.BlockSpec((1,) + shape, lambda b, pt: (b, 0, 0))
    grid_spec = pltpu.PrefetchScalarGridSpec(
        num_scalar_prefetch=1,
        grid=(db,),
        in_specs=[pl.BlockSpec(dl.shape, lambda b, pt: (0, 0)),
                  pl.BlockSpec(subln_g.shape, lambda b, pt: (0, 0)),
                  per_b(qblk.shape[1:]), per_b(k_new16.shape[1:]), per_b(v_new16.shape[1:])]
                 + [k_spec(p) for p in range(n_pages)]
                 + [v_spec(p) for p in range(n_pages)],
        out_specs=per_b((t_seq, DIFF_VW)),
    )
    return pl.pallas_call(
        kernel,
        grid_spec=grid_spec,
        out_shape=jax.ShapeDtypeStruct((db, t_seq, DIFF_VW), BF16),
        compiler_params=_params("parallel"),
        name="sample_attn",
    )(page_table.reshape(-1), dl, subln_g, qblk, k_new16, v_new16,
      *([kc] * n_pages), *([vc] * n_pages))


def _ffn_kernel(x_ref, d_ref, r_ref, wo_ref, gf_ref, wi_ref, wo2_ref, gl_ref, y_ref, act, *, fc):
    d_ff = wo2_ref.shape[0]
    mix = (jnp.dot(d_ref[...], wo_ref[0:DIFF_VW, :], preferred_element_type=F32)
           + jnp.dot(r_ref[...], wo_ref[DIFF_VW:, :], preferred_element_type=F32))
    x1 = x_ref[...] + mix
    hb = (_rms(x1, NORM_EPS) * gf_ref[...]).astype(BF16)
    for c in range(d_ff // fc):
        g = jnp.dot(hb, wi_ref[:, c * fc:(c + 1) * fc], preferred_element_type=F32)
        u = jnp.dot(hb, wi_ref[:, d_ff + c * fc:d_ff + (c + 1) * fc], preferred_element_type=F32)
        act[:, c * fc:(c + 1) * fc] = (jax.nn.silu(g) * u).astype(BF16)
    x2 = x1 + jnp.dot(act[...], wo2_ref[...], preferred_element_type=F32)
    y_ref[...] = _rms(x2, NORM_EPS) * gl_ref[...]


def _merge_ffn(x, d, r, w_out, g_ffn, w_ffn_in, w_ffn_out, g_final, tm, fc):
    n, dm = x.shape
    d_ff = w_ffn_out.shape[0]
    row = lambda width: pl.BlockSpec((tm, width), lambda i: (i, 0))
    return pl.pallas_call(
        functools.partial(_ffn_kernel, fc=fc),
        grid=(n // tm,),
        in_specs=[row(dm), row(DIFF_VW), row(RET_VW), _const_spec(w_out.shape),
                  _const_spec(g_ffn.shape), _const_spec(w_ffn_in.shape),
                  _const_spec(w_ffn_out.shape), _const_spec(g_final.shape)],
        out_specs=row(dm),
        out_shape=jax.ShapeDtypeStruct((n, dm), F32),
        scratch_shapes=[pltpu.VMEM((tm, d_ff), BF16)],
        compiler_params=_params("parallel"),
        name="merge_ffn",
    )(x, d, r, w_out, g_ffn, w_ffn_in, w_ffn_out, g_final)


def _ffn_chunk(d_ff):
    for fc in (512, 256, 128):
        if d_ff % fc == 0:
            return fc
    return d_ff


def kernel(x_prompt, x_sample, cache_diff_k, cache_diff_v, state_ret, page_table, norm_mix_g, w_in, diff_lambda, diff_subln_g, w_out, norm_ffn_g, w_ffn_in, w_ffn_out, norm_final_g):
    bsz, seq, dm = x_prompt.shape
    db, t_seq, _ = x_sample.shape
    n_pages = page_table.shape[1]
    page = cache_diff_k.shape[2]
    past = n_pages * page
    assert w_in.shape[0] == 1, "single layer"
    assert LANES % t_seq == 0 and (db * t_seq) % LANES == 0

    w_in16 = w_in[0].astype(BF16)
    w_out16 = w_out[0].astype(BF16)
    w_ffn_in16 = w_ffn_in[0].astype(BF16)
    w_ffn_out16 = w_ffn_out[0].astype(BF16)
    g_mix, g_ffn = norm_mix_g, norm_ffn_g
    g_final = norm_final_g.reshape(1, dm)
    dl = diff_lambda[0]
    fc = _ffn_chunk(w_ffn_out.shape[1])

    tm = min(512, seq)
    tabs_p = _rotary_tables(jnp.arange(seq))
    qt16, k16, kt32, v32, vt16, rq, rkt, rv, rg = _inproj(x_prompt, g_mix, w_in16, tabs_p, tm)
    d_p = _prompt_attention(dl, diff_subln_g, qt16, k16, vt16, min(1024, seq), min(512, seq))
    r_p, ret_state_p = _prompt_retention(rq, rkt, rv, rg, min(512, seq))
    y_prompt = _merge_ffn(x_prompt.reshape(bsz * seq, dm), d_p.reshape(bsz * seq, DIFF_VW),
                          r_p.reshape(bsz * seq, RET_VW), w_out16, g_ffn, w_ffn_in16,
                          w_ffn_out16, g_final, tm, fc).reshape(bsz, seq, dm)
    k_prompt = jnp.swapaxes(kt32, 1, 2).reshape(1, bsz, seq, N_DIFF_HEADS, 2, DIFF_QK_DIM)
    v_prompt = v32.reshape(1, bsz, seq, N_DIFF_HEADS, DIFF_V_DIM)
    ret_prompt = ret_state_p[None]

    n_s = db * t_seq
    pos_s = past + (jnp.arange(n_s) % t_seq)
    tabs_s = _rotary_tables(pos_s)
    tm_s = min(512, n_s)
    qts, ks16, kts32, vs32, _, rqs, rkts, rvs, rgs = _inproj(
        x_sample.reshape(1, n_s, dm), g_mix, w_in16, tabs_s, tm_s)
    k_s = kts32[0].T
    qs = qts[0].T
    r_idx = np.arange(2 * N_DIFF_HEADS * t_seq)
    r_map, r_head, r_tok = r_idx // (N_DIFF_HEADS * t_seq), (r_idx // t_seq) % N_DIFF_HEADS, r_idx % t_seq
    feat_owner = np.arange(DIFF_W) // DIFF_QK_DIM
    sel = (feat_owner[None, :] == (r_head * 2 + r_map)[:, None])
    qs3 = qs.reshape(db, t_seq, DIFF_W)
    qblk = jnp.where(sel[None], qs3[:, r_tok, :], jnp.zeros((), BF16))
    kc = jnp.transpose(cache_diff_k[0], (0, 2, 3, 4, 1)).reshape(-1, DIFF_W, page)
    d_s = _sample_attention(page_table, dl, diff_subln_g, qblk,
                            ks16.reshape(db, t_seq, DIFF_W),
                            vs32.astype(BF16).reshape(db, t_seq, DIFF_VW), kc,
                            cache_diff_v[0].reshape(-1, page * N_DIFF_HEADS, DIFF_V_DIM))
    r_s, ret_state_s = _sample_retention(rqs, rkts, rvs, rgs, state_ret[0], t_seq)
    y_sample = _merge_ffn(x_sample.reshape(n_s, dm), d_s.reshape(n_s, DIFF_VW),
                          r_s.reshape(n_s, RET_VW), w_out16, g_ffn, w_ffn_in16,
                          w_ffn_out16, g_final, tm_s, fc).reshape(db, t_seq, dm)
    k_sample = k_s.reshape(1, db, t_seq, N_DIFF_HEADS, 2, DIFF_QK_DIM)
    v_sample = vs32.reshape(1, db, t_seq, N_DIFF_HEADS, DIFF_V_DIM)
    ret_sample = ret_state_s[None]

    return (y_prompt, y_sample, k_prompt, v_prompt, ret_prompt, k_sample, v_sample, ret_sample)
```

```python
import functools
import math

import jax
import jax.numpy as jnp
import numpy as np
from jax import lax
from jax.experimental import pallas as pl
from jax.experimental.pallas import tpu as pltpu

F32 = jnp.float32
BF16 = jnp.bfloat16

N_DIFF_HEADS = 4
DIFF_QK_DIM = 64
DIFF_V_DIM = 128
ROT_DIM = 16
ROPE_THETA = 500000.0
N_RET_HEADS = 4
RET_QK_DIM = 64
RET_V_DIM = 128
RET_THETA = 10000.0
RET_CHUNK = 128
NORM_EPS = 1e-6
SUBLN_EPS = 1e-5
LAM_INIT = 0.8 - 0.6 * math.exp(-0.3 * 0)
MASK_VALUE = -1e30
Q_SCALE = DIFF_QK_DIM ** -0.5 * math.log2(math.e)

DIFF_W = N_DIFF_HEADS * 2 * DIFF_QK_DIM
DIFF_VW = N_DIFF_HEADS * DIFF_V_DIM
RET_QW = N_RET_HEADS * RET_QK_DIM
RET_VW = N_RET_HEADS * RET_V_DIM
W_IN_OFFSETS = tuple(int(o) for o in np.cumsum([0, DIFF_W, DIFF_W, DIFF_VW, RET_QW, RET_QW, RET_VW]))

LANES = 128
ACC_PAD = 16
N_PAGE_SLOTS = 2
VMEM_LIMIT = 56 * 1024 * 1024


def _params(*sem, flags=None):
    return pltpu.CompilerParams(dimension_semantics=sem, vmem_limit_bytes=VMEM_LIMIT, flags=flags)


def _const_spec(shape):
    nd = len(shape)
    return pl.BlockSpec(shape, lambda *_: (0,) * nd, pipeline_mode=pl.Buffered(1))


def _rms(x, eps):
    return x * lax.rsqrt(jnp.mean(x * x, axis=-1, keepdims=True) + eps)


def _inv_freq(freq_idx, dim, theta):
    return 1.0 / (jnp.float32(theta) ** (jnp.asarray(2 * freq_idx, F32) / dim))


def _rotary_tables(pos):
    posf = pos.astype(F32)

    def feature_tables(half, dim, theta, scale):
        ang = _inv_freq(np.arange(half), dim, theta)[:, None] * posf[None, :]
        return jnp.cos(ang) * scale, jnp.sin(ang) * scale

    return (feature_tables(ROT_DIM // 2, ROT_DIM, ROPE_THETA, Q_SCALE)
            + feature_tables(ROT_DIM // 2, ROT_DIM, ROPE_THETA, 1.0)
            + feature_tables(RET_QK_DIM // 2, RET_QK_DIM, RET_THETA, 1.0)
            + feature_tables(RET_QK_DIM // 2, RET_QK_DIM, RET_THETA, RET_QK_DIM ** -0.5))


def _log_gamma():
    return jnp.log(1.0 - 2.0 ** (-5.0 - jnp.arange(N_RET_HEADS, dtype=F32)))


def _retention_tables(chunk, n_tok):
    lg = _log_gamma()
    idx = jnp.arange(n_tok)
    loc = (idx % chunk).astype(F32)
    rel = loc[:, None] - loc[None, :]
    same = (idx[:, None] // chunk) == (idx[None, :] // chunk)
    decay = jnp.where(same[None] & (rel >= 0)[None],
                      jnp.exp(lg[:, None, None] * jnp.maximum(rel, 0.0)[None]), 0.0)
    qdec = jnp.exp(lg[:, None] * (loc[None, :] + 1.0))[:, :, None]
    kdec = jnp.exp(lg[:, None] * (chunk - 1.0 - loc[None, :]))[:, None, :]
    gc = jnp.broadcast_to(jnp.exp(lg * chunk)[:, None, None], (N_RET_HEADS, 1, LANES))
    return decay, qdec, kdec, gc


def _inproj_kernel(x_ref, g_ref, w_ref,
                   cqt_ref, sqt_ref, ckt_ref, skt_ref, crqt_ref, srqt_ref, crkt_ref, srkt_ref,
                   qt_ref, k16_ref, kt32_ref, v32_ref, vt16_ref,
                   rq_ref, rkt_ref, rv_ref, rg_ref, rqt_scr):
    x = x_ref[0]
    hb = (_rms(x, NORM_EPS) * g_ref[...]).astype(BF16)

    def mm(lo, width):
        return jnp.dot(hb, w_ref[:, lo:lo + width], preferred_element_type=F32)

    def rot_t(zt, base, width, half, c, s, rest_scale=None):
        x1, x2 = zt[base:base + half], zt[base + half:base + 2 * half]
        parts = [x1 * c - x2 * s, x2 * c + x1 * s]
        if 2 * half < width:
            rest = zt[base + 2 * half:base + width]
            parts.append(rest if rest_scale is None else rest * rest_scale)
        return jnp.concatenate(parts, axis=0)

    o_dq, o_dk, o_dv, o_rq, o_rk, o_rv, o_rg = W_IN_OFFSETS
    hr = ROT_DIM // 2
    zt = mm(o_dq, DIFF_W).T
    c, s = cqt_ref[...], sqt_ref[...]
    for g in range(N_DIFF_HEADS * 2):
        b = g * DIFF_QK_DIM
        qt_ref[0, b:b + DIFF_QK_DIM, :] = rot_t(
            zt, b, DIFF_QK_DIM, hr, c, s, rest_scale=Q_SCALE).astype(BF16)
    zt = mm(o_dk, DIFF_W).T
    c, s = ckt_ref[...], skt_ref[...]
    for g in range(N_DIFF_HEADS * 2):
        b = g * DIFF_QK_DIM
        kt32_ref[0, b:b + DIFF_QK_DIM, :] = rot_t(zt, b, DIFF_QK_DIM, hr, c, s)
    k16_ref[0] = kt32_ref[0].T.astype(BF16)
    zv = mm(o_dv, DIFF_VW)
    for h in range(N_DIFF_HEADS):
        v32_ref[0, pl.ds(h, zv.shape[0], stride=N_DIFF_HEADS), :] = (
            zv[:, h * DIFF_V_DIM:(h + 1) * DIFF_V_DIM])
    vt16_ref[0] = zv.T.astype(BF16)
    zt = mm(o_rq, RET_QW).T
    c, s = crqt_ref[...], srqt_ref[...]
    for h in range(N_RET_HEADS):
        b = h * RET_QK_DIM
        rqt_scr[b:b + RET_QK_DIM, :] = rot_t(zt, b, RET_QK_DIM, RET_QK_DIM // 2, c, s)
    rq_ref[0] = rqt_scr[...].T.astype(BF16)
    zt = mm(o_rk, RET_QW).T
    c, s = crkt_ref[...], srkt_ref[...]
    for h in range(N_RET_HEADS):
        b = h * RET_QK_DIM
        rkt_ref[0, b:b + RET_QK_DIM, :] = rot_t(zt, b, RET_QK_DIM, RET_QK_DIM // 2, c, s)
    rv_ref[0] = mm(o_rv, RET_VW).astype(BF16)
    rg_ref[0] = mm(o_rg, RET_VW)


def _inproj(x, g, w16, tables, tm):
    b, s, d = x.shape
    row = lambda width: pl.BlockSpec((1, tm, width), lambda bi, i: (bi, i, 0))
    col = lambda height: pl.BlockSpec((1, height, tm), lambda bi, i: (bi, 0, i))
    ttab = lambda height: pl.BlockSpec((height, tm), lambda bi, i: (0, i))
    out_shape = (
        jax.ShapeDtypeStruct((b, DIFF_W, s), BF16),
        jax.ShapeDtypeStruct((b, s, DIFF_W), BF16),
        jax.ShapeDtypeStruct((b, DIFF_W, s), F32),
        jax.ShapeDtypeStruct((b, s * N_DIFF_HEADS, DIFF_V_DIM), F32),
        jax.ShapeDtypeStruct((b, DIFF_VW, s), BF16),
        jax.ShapeDtypeStruct((b, s, RET_QW), BF16),
        jax.ShapeDtypeStruct((b, RET_QW, s), F32),
        jax.ShapeDtypeStruct((b, s, RET_VW), BF16),
        jax.ShapeDtypeStruct((b, s, RET_VW), F32),
    )
    v_rows = pl.BlockSpec((1, tm * N_DIFF_HEADS, DIFF_V_DIM), lambda bi, i: (bi, i, 0))
    out_specs = (col(DIFF_W), row(DIFF_W), col(DIFF_W), v_rows, col(DIFF_VW),
                 row(RET_QW), col(RET_QW), row(RET_VW), row(RET_VW))
    return pl.pallas_call(
        _inproj_kernel,
        grid=(b, s // tm),
        in_specs=[row(d), _const_spec((1, d)), _const_spec(w16.shape)]
                 + [ttab(ROT_DIM // 2)] * 4 + [ttab(RET_QK_DIM // 2)] * 4,
        out_specs=out_specs,
        out_shape=out_shape,
        scratch_shapes=[pltpu.VMEM((RET_QW, tm), F32)],
        compiler_params=_params("parallel", "parallel"),
        name="inproj",
    )(x, g, w16, *tables)


def _lambda_full(dl_ref):
    lp = dl_ref[...]
    a = jnp.sum(lp[0:1] * lp[1:2], axis=-1, keepdims=True)
    b = jnp.sum(lp[2:3] * lp[3:4], axis=-1, keepdims=True)
    return jnp.exp(a) - jnp.exp(b) + LAM_INIT


def _subln(o, g):
    return _rms(o, SUBLN_EPS) * g * (1.0 - LAM_INIT)


def _attn_kernel(dl_ref, g_ref, qt_ref, qt_next_ref, k_ref, vt_ref, o_ref, acc, s_scr, p_scr,
                 *, tq, tk):
    i = pl.program_id(2)
    first_step = (pl.program_id(0) == 0) & (pl.program_id(1) == 0) & (i == 0)

    def score_columns(qt):
        row = lax.broadcasted_iota(jnp.int32, (qt.shape[0], tk), 0)
        groups = []
        for half in range(2):
            qh = qt[:, half * tk:(half + 1) * tk]
            groups += [jnp.where(row < DIFF_QK_DIM, qh, jnp.zeros_like(qh)),
                       jnp.where(row >= DIFF_QK_DIM, qh, jnp.zeros_like(qh))]
        return jnp.concatenate(groups, axis=1)

    qtb = score_columns(qt_ref[0])
    acc[...] = jnp.zeros_like(acc)
    n_full = 2 * i
    all_cols, lo, hi = slice(0, 2 * tq), slice(0, tq), slice(tq, 2 * tq)

    def issue_scores(j, par, cols, q=None):
        off = pl.multiple_of(j * tk, tk)
        q = qtb if q is None else q
        s_scr[par, :, cols] = jnp.dot(k_ref[0, pl.ds(off, tk), :], q[:, cols],
                                      preferred_element_type=F32)

    def softmax_stage(par, m, cols, causal):
        s = s_scr[par, :, cols]
        if causal:
            c = lax.broadcasted_iota(jnp.int32, (tk, tq), 1)
            c = jnp.where(c >= tk, c - tk, c)
            s = jnp.where(lax.broadcasted_iota(jnp.int32, (tk, tq), 0) <= c, s, MASK_VALUE)
        m_new = jnp.maximum(m, jnp.max(s, axis=0, keepdims=True))
        alpha = jnp.exp2(m - m_new)
        p_scr[par, :, cols] = jnp.exp2(s - m_new).astype(BF16)
        return m_new, alpha

    ones_rows = (lax.broadcasted_iota(jnp.int32, (ACC_PAD, tk), 0) == 0).astype(BF16)

    def value_stage(j, par, alpha, cols):
        off = pl.multiple_of(j * tk, tk)
        vt = jnp.concatenate([vt_ref[0, :, pl.ds(off, tk)], ones_rows], axis=0)
        acc[:, cols] = alpha * acc[:, cols] + jnp.dot(vt, p_scr[par, :, cols],
                                                      preferred_element_type=F32)

    def substep(j, par, m, alpha_prev):
        issue_scores(j + 1, 1 - par, all_cols)
        value_stage(jnp.maximum(j - 1, 0), 1 - par, alpha_prev, all_cols)
        return softmax_stage(par, m, all_cols, False)

    def body(t, carry):
        m, alpha = substep(2 * t, 0, *carry)
        return substep(2 * t + 1, 1, m, alpha)

    @pl.when(i == 0)
    def _():
        issue_scores(0, 0, all_cols)

    @pl.when(first_step)
    def _():
        p_scr[1] = jnp.zeros((tk, 2 * tq), BF16)

    carry = (jnp.full((1, 2 * tq), MASK_VALUE, F32), jnp.ones((1, 2 * tq), F32))
    m, alpha = lax.fori_loop(0, i, body, carry)
    issue_scores(n_full + 1, 1, hi)
    value_stage(jnp.maximum(n_full - 1, 0), 1, alpha, all_cols)
    _, alpha_lo = softmax_stage(0, m[:, lo], lo, True)
    m_hi, alpha_hi = softmax_stage(0, m[:, hi], hi, False)
    value_stage(n_full, 0, jnp.concatenate([alpha_lo, alpha_hi], axis=1), all_cols)
    _, alpha_hi = softmax_stage(1, m_hi, hi, True)
    value_stage(n_full + 1, 1, alpha_hi, hi)
    issue_scores(0, 0, all_cols, q=score_columns(qt_next_ref[0]))
    p_scr[1] = jnp.zeros((tk, 2 * tq), BF16)

    lam = _lambda_full(dl_ref)
    out = []
    for half in range(2):
        c0 = slice(2 * half * tk, (2 * half + 1) * tk)
        c1 = slice((2 * half + 1) * tk, (2 * half + 2) * tk)
        out.append(acc[0:DIFF_V_DIM, c0] / acc[DIFF_V_DIM:DIFF_V_DIM + 1, c0]
                   - lam * (acc[0:DIFF_V_DIM, c1] / acc[DIFF_V_DIM:DIFF_V_DIM + 1, c1]))
    ot = jnp.concatenate(out, axis=1)
    o_ref[0] = _subln(ot.T, g_ref[...]).astype(o_ref.dtype)


def _prompt_attention(dl, subln_g, qt16, k16, vt16, tq, tk):
    b, s, _ = k16.shape
    assert tq == 2 * tk, "a query block spans two key blocks (two scratch slots)"
    n_q = s // tq
    kernel = functools.partial(_attn_kernel, tq=tq, tk=tk)
    return pl.pallas_call(
        kernel,
        grid=(b, N_DIFF_HEADS, s // tq),
        in_specs=[
            _const_spec(dl.shape),
            _const_spec(subln_g.shape),
            pl.BlockSpec((1, LANES, tq), lambda bi, h, i: (bi, h, i)),
            pl.BlockSpec((1, LANES, tq), lambda bi, h, i: (bi, h, jnp.minimum(i + 1, n_q - 1))),
            pl.BlockSpec((1, s, LANES), lambda bi, h, i: (bi, 0, h)),
            pl.BlockSpec((1, LANES, s), lambda bi, h, i: (bi, h, 0)),
        ],
        out_specs=pl.BlockSpec((1, tq, LANES), lambda bi, h, i: (bi, i, h)),
        out_shape=jax.ShapeDtypeStruct((b, s, DIFF_VW), BF16),
        scratch_shapes=[pltpu.VMEM((DIFF_V_DIM + ACC_PAD, 2 * tq), F32),
                        pltpu.VMEM((2, tk, 2 * tq), F32),
                        pltpu.VMEM((2, tk, 2 * tq), BF16)],
        compiler_params=_params("arbitrary", "arbitrary", "arbitrary"),
        name="prompt_attn",
    )(dl, subln_g, qt16, qt16, k16, vt16)


def _gated_norm(o, gate):
    return jax.nn.silu(gate) * _rms(o, NORM_EPS)


def _ret_kernel(rq_ref, rkt_ref, rv_ref, rg_ref, dec_ref, qdec_ref, kdec_ref, gc_ref,
                r_ref, st_ref, state, *, n_chunks):
    j = pl.program_id(1)

    @pl.when(j == 0)
    def _():
        state[...] = jnp.zeros_like(state)

    c = RET_CHUNK
    lane = lax.broadcasted_iota(jnp.int32, (c, LANES), 1)
    n_pair = N_RET_HEADS // 2
    st = [state[hp] for hp in range(n_pair)]
    for ci in range(n_chunks):
        rows = slice(ci * c, (ci + 1) * c)
        for hp in range(n_pair):
            cols = slice(hp * LANES, (hp + 1) * LANES)
            qp = rq_ref[0, rows, cols]
            ktp = rkt_ref[0, cols, rows]
            ktp16 = ktp.astype(BF16)
            stp = st[hp]
            stp16 = stp.astype(BF16)
            q2 = jnp.concatenate(
                [jnp.where((lane >= e * RET_QK_DIM) & (lane < (e + 1) * RET_QK_DIM),
                           qp, jnp.zeros_like(qp)) for e in range(2)], axis=0)
            sc2 = jnp.dot(q2, ktp16, preferred_element_type=F32)
            cross2 = jnp.dot(q2, stp16, preferred_element_type=F32)
            new = []
            for e in range(2):
                h = hp * 2 + e
                hs = slice(e * RET_QK_DIM, (e + 1) * RET_QK_DIM)
                hv = slice(h * RET_V_DIM, (h + 1) * RET_V_DIM)
                sc = sc2[e * c:(e + 1) * c] * dec_ref[h]
                vh = rv_ref[0, rows, hv]
                kd = (ktp[hs] * kdec_ref[h]).astype(BF16)
                both = jnp.dot(jnp.concatenate([sc.astype(BF16), kd], axis=0), vh,
                               preferred_element_type=F32)
                cross = cross2[e * c:(e + 1) * c] * qdec_ref[h]
                r_ref[0, rows, hv] = _gated_norm(both[:c] + cross, rg_ref[0, rows, hv]).astype(r_ref.dtype)
                new.append(gc_ref[h] * stp[hs] + both[c:])
            st[hp] = jnp.concatenate(new, axis=0)
    for hp in range(n_pair):
        state[hp] = st[hp]

    @pl.when(j == pl.num_programs(1) - 1)
    def _():
        for hp in range(n_pair):
            st_ref[0, hp] = st[hp]


def _prompt_retention(rq, rkt, rv, rg, ts):
    b, s, _ = rq.shape
    tables = _retention_tables(RET_CHUNK, RET_CHUNK)
    kernel = functools.partial(_ret_kernel, n_chunks=ts // RET_CHUNK)
    row = lambda width: pl.BlockSpec((1, ts, width), lambda bi, j: (bi, j, 0))
    n_pair = N_RET_HEADS // 2
    r, st = pl.pallas_call(
        kernel,
        grid=(b, s // ts),
        in_specs=[row(RET_QW), pl.BlockSpec((1, RET_QW, ts), lambda bi, j: (bi, 0, j)),
                  row(RET_VW), row(RET_VW)] + [_const_spec(t.shape) for t in tables],
        out_specs=(row(RET_VW),
                   pl.BlockSpec((1, n_pair, LANES, RET_V_DIM), lambda bi, j: (bi, 0, 0, 0))),
        out_shape=(jax.ShapeDtypeStruct((b, s, RET_VW), BF16),
                   jax.ShapeDtypeStruct((b, n_pair, LANES, RET_V_DIM), F32)),
        scratch_shapes=[pltpu.VMEM((n_pair, LANES, RET_V_DIM), F32)],
        compiler_params=_params("parallel", "arbitrary"),
        name="prompt_ret",
    )(rq, rkt, rv, rg, *tables)
    return r, st.reshape(b, N_RET_HEADS, RET_QK_DIM, RET_V_DIM)


def _sret_kernel(rq_ref, rkt_ref, rv_ref, rg_ref, st_in_ref, dec_ref, qdec_ref, kdec_ref, gc_ref,
                 r_ref, st_out_ref, *, t_seq):
    n = LANES
    nb = n // t_seq
    lane = lax.broadcasted_iota(jnp.int32, (n, LANES), 1)
    tok_r = lax.broadcasted_iota(jnp.int32, (nb, n, LANES), 1) // t_seq
    seq_r = lax.broadcasted_iota(jnp.int32, (nb, n, LANES), 0)
    row_in_seq = tok_r == seq_r
    tok_l = lax.broadcasted_iota(jnp.int32, (nb, RET_QK_DIM, n), 2) // t_seq
    seq_l = lax.broadcasted_iota(jnp.int32, (nb, RET_QK_DIM, n), 0)
    lane_in_seq = tok_l == seq_l
    for hp in range(N_RET_HEADS // 2):
        cols = slice(hp * LANES, (hp + 1) * LANES)
        qp = rq_ref[0, :, cols]
        ktp = rkt_ref[0, cols, :]
        ktp16 = ktp.astype(BF16)
        stp = st_in_ref[:, hp]
        st16 = stp.astype(BF16).reshape(nb * LANES, RET_V_DIM)
        for e in range(2):
            h = hp * 2 + e
            hs = slice(e * RET_QK_DIM, (e + 1) * RET_QK_DIM)
            hv = slice(h * RET_V_DIM, (h + 1) * RET_V_DIM)
            in_head = (lane >= e * RET_QK_DIM) & (lane < (e + 1) * RET_QK_DIM)
            qh = jnp.where(in_head, qp, jnp.zeros_like(qp))
            sc = jnp.dot(qh, ktp16, preferred_element_type=F32) * dec_ref[h]
            vh = rv_ref[0, :, hv]
            inner = jnp.dot(sc.astype(BF16), vh, preferred_element_type=F32)
            qbd = jnp.where(row_in_seq, qh[None], jnp.zeros_like(qh)[None])
            qbd = jnp.concatenate([qbd[b] for b in range(nb)], axis=1)
            cross = jnp.dot(qbd, st16, preferred_element_type=F32) * qdec_ref[h]
            r_ref[0, :, hv] = _gated_norm(inner + cross, rg_ref[0, :, hv]).astype(r_ref.dtype)
            kd = (ktp[hs] * kdec_ref[h]).astype(BF16)
            kds = jnp.where(lane_in_seq, kd[None], jnp.zeros_like(kd)[None])
            upd = jnp.dot(kds.reshape(nb * RET_QK_DIM, n), vh, preferred_element_type=F32)
            st_out_ref[:, hp, hs, :] = (gc_ref[h] * stp[:, hs, :]
                                        + upd.reshape(nb, RET_QK_DIM, RET_V_DIM))


def _sample_retention(rq, rkt, rv, rg, state, t_seq):
    n_tok = rq.shape[1]
    db = state.shape[0]
    nb = LANES // t_seq
    n_pair = N_RET_HEADS // 2
    st_pairs = state.reshape(db, n_pair, LANES, RET_V_DIM)
    tables = _retention_tables(t_seq, LANES)
    row = lambda width: pl.BlockSpec((1, LANES, width), lambda j: (0, j, 0))
    st_spec = pl.BlockSpec((nb, n_pair, LANES, RET_V_DIM), lambda j: (j, 0, 0, 0))
    r, st = pl.pallas_call(
        functools.partial(_sret_kernel, t_seq=t_seq),
        grid=(n_tok // LANES,),
        in_specs=[row(RET_QW), pl.BlockSpec((1, RET_QW, LANES), lambda j: (0, 0, j)),
                  row(RET_VW), row(RET_VW), st_spec] + [_const_spec(t.shape) for t in tables],
        out_specs=(row(RET_VW), st_spec),
        out_shape=(jax.ShapeDtypeStruct((1, n_tok, RET_VW), BF16),
                   jax.ShapeDtypeStruct(st_pairs.shape, F32)),
        compiler_params=_params("parallel"),
        name="sample_ret",
    )(rq, rkt, rv, rg, st_pairs, *tables)
    return r, st.reshape(state.shape)


def _sattn_compute(dl_ref, g_ref, qb, kn, vn, k_page, v_page, n_pages, page):
    t_seq = kn.shape[0]
    nh = N_DIFF_HEADS
    half = nh * t_seq
    qb = qb.astype(F32)
    s = jnp.concatenate(
        [jnp.dot(qb, k_page(p), preferred_element_type=F32) for p in range(n_pages)],
        axis=1)
    kn = kn.astype(F32)
    t_row = lax.broadcasted_iota(jnp.int32, (2 * half, 1), 0) % t_seq
    s_new = []
    for tk in range(t_seq):
        col = jnp.sum(qb * kn[tk:tk + 1, :], axis=-1, keepdims=True)
        s_new.append(jnp.where(tk <= t_row, col, MASK_VALUE))
    m = jnp.max(s, axis=-1, keepdims=True)
    for col in s_new:
        m = jnp.maximum(m, col)
    p = jnp.exp2(s - m)
    p_new = [jnp.exp2(col - m) for col in s_new]
    l = jnp.sum(p, axis=-1, keepdims=True)
    for col in p_new:
        l = l + col
    lam = _lambda_full(dl_ref)
    a = (p[:half] / l[:half] - lam * (p[half:] / l[half:])).astype(BF16)
    a_new = [(c[:half] / l[:half] - lam * (c[half:] / l[half:])).astype(BF16).astype(F32)
             for c in p_new]
    a_pages = jnp.concatenate([a[:, pg * page:(pg + 1) * page] for pg in range(n_pages)],
                              axis=0)
    spread = (lax.broadcasted_iota(jnp.int32, (page, page * nh), 1) // nh
              == lax.broadcasted_iota(jnp.int32, (page, page * nh), 0)).astype(BF16)
    ax = jnp.dot(a_pages, spread, preferred_element_type=F32)
    col_head = lax.broadcasted_iota(jnp.int32, ax.shape, 1) % nh
    row_head = (lax.broadcasted_iota(jnp.int32, ax.shape, 0) % half) // t_seq
    ax = jnp.where(col_head == row_head, ax, 0.0)
    acc = jnp.zeros((half, DIFF_V_DIM), F32)
    for pg in range(n_pages):
        acc = acc + jnp.dot(ax[pg * half:(pg + 1) * half], v_page(pg),
                            preferred_element_type=F32)
    vn = vn.astype(F32)
    out = []
    for h in range(nh):
        hv = slice(h * DIFF_V_DIM, (h + 1) * DIFF_V_DIM)
        rows = slice(h * t_seq, (h + 1) * t_seq)
        o = acc[rows]
        for tk in range(t_seq):
            o = o + a_new[tk][rows] * vn[tk:tk + 1, hv]
        out.append(_subln(o, g_ref[...]))
    return jnp.concatenate(out, axis=1)


def _ffn_stages(x_ref, d_ref, r_ref, wo_ref, gf_ref, wi_ref, wo2_ref, gl_ref, y_ref, act, fc):
    d_ff = wo2_ref.shape[0]
    dm = x_ref.shape[1]
    live = {}

    def head():
        mix = (jnp.dot(d_ref[...], wo_ref[0:DIFF_VW, :], preferred_element_type=F32)
               + jnp.dot(r_ref[...], wo_ref[DIFF_VW:, :], preferred_element_type=F32))
        live["x1"] = x_ref[...] + mix
        live["hb"] = (_rms(live["x1"], NORM_EPS) * gf_ref[...]).astype(BF16)

    def chunk(c):
        def run():
            hb = live["hb"]
            g = jnp.dot(hb, wi_ref[:, c * fc:(c + 1) * fc], preferred_element_type=F32)
            u = jnp.dot(hb, wi_ref[:, d_ff + c * fc:d_ff + (c + 1) * fc],
                        preferred_element_type=F32)
            act[:, c * fc:(c + 1) * fc] = (jax.nn.silu(g) * u).astype(BF16)
        return run

    def tail():
        x2 = live["x1"] + jnp.dot(act[...], wo2_ref[...], preferred_element_type=F32)
        y_ref[...] = _rms(x2, NORM_EPS) * gl_ref[...]

    return ([(wo_ref.shape[0] * dm, head)]
            + [(2 * dm * fc, chunk(c)) for c in range(d_ff // fc)]
            + [(d_ff * dm, tail)])


def _ffn_kernel(*refs, fc):
    for _, stage in _ffn_stages(*refs, fc):
        stage()


def _merge_ffn(x, d, r, w_out, g_ffn, w_ffn_in, w_ffn_out, g_final, tm, fc):
    n, dm = x.shape
    d_ff = w_ffn_out.shape[0]
    row = lambda width: pl.BlockSpec((tm, width), lambda i: (i, 0))
    return pl.pallas_call(
        functools.partial(_ffn_kernel, fc=fc),
        grid=(n // tm,),
        in_specs=[row(dm), row(DIFF_VW), row(RET_VW), _const_spec(w_out.shape),
                  _const_spec(g_ffn.shape), _const_spec(w_ffn_in.shape),
                  _const_spec(w_ffn_out.shape), _const_spec(g_final.shape)],
        out_specs=row(dm),
        out_shape=jax.ShapeDtypeStruct((n, dm), F32),
        scratch_shapes=[pltpu.VMEM((tm, d_ff), BF16)],
        compiler_params=_params("parallel"),
        name="merge_ffn",
    )(x, d, r, w_out, g_ffn, w_ffn_in, w_ffn_out, g_final)


def _ffn_sattn_kernel(pt_ref, x_ref, d_ref, r_ref, wo_ref, gf_ref, wi_ref, wo2_ref, gl_ref,
                      dl_ref, g_ref, qb_ref, kn_ref, vn_ref, kc_hbm, vc_hbm,
                      y_ref, ds_ref, act, kbuf, vbuf, sem, *, fc, n_seq, n_pages):
    step = pl.program_id(0)
    per_step = qb_ref.shape[0]
    page = kbuf.shape[3]

    def page_copies(seq, slot):
        copies = []
        for p in range(n_pages):
            pg = pt_ref[seq * n_pages + p]
            copies.append(pltpu.make_async_copy(kc_hbm.at[pg], kbuf.at[slot, p], sem.at[slot]))
            copies.append(pltpu.make_async_copy(vc_hbm.at[pg], vbuf.at[slot, p], sem.at[slot]))
        return copies

    def start(seq, slot):
        for c in page_copies(seq, slot):
            c.start()

    def wait(seq, slot):
        for c in page_copies(seq, slot):
            c.wait()

    @pl.when(step == 0)
    def _():
        for t in range(N_PAGE_SLOTS):
            start(t, t)

    stages = _ffn_stages(x_ref, d_ref, r_ref, wo_ref, gf_ref, wi_ref, wo2_ref, gl_ref, y_ref, act, fc)
    total = sum(w for w, _ in stages)
    shares = [[] for _ in range(per_step)]
    done = 0
    for w, stage in stages:
        shares[min(per_step - 1, (2 * done + w) * per_step // (2 * total))].append(stage)
        done += w

    for t in range(per_step):
        seq = step * per_step + t
        slot = t % N_PAGE_SLOTS
        wait(seq, slot)
        o = _sattn_compute(dl_ref, g_ref, qb_ref[t], kn_ref[t], vn_ref[t],
                           lambda p: kbuf[slot, p], lambda p: vbuf[slot, p], n_pages, page)
        ds_ref[t] = o.astype(ds_ref.dtype)
        for stage in shares[t]:
            stage()

        @pl.when(seq + N_PAGE_SLOTS < n_seq)
        def _():
            start(seq + N_PAGE_SLOTS, slot)


def _merge_ffn_sample_attention(x, d, r, w_out, g_ffn, w_ffn_in, w_ffn_out, g_final, tm, fc,
                                page_table, dl, subln_g, qblk, k_new16, v_new16, kc, vc):
    n, dm = x.shape
    d_ff = w_ffn_out.shape[0]
    db, n_pages = page_table.shape
    t_seq = k_new16.shape[1]
    page = kc.shape[2]
    n_steps = n // tm
    assert db % n_steps == 0 and (db // n_steps) % N_PAGE_SLOTS == 0, "sequences per FFN step"
    per_step = db // n_steps
    row = lambda width: pl.BlockSpec((tm, width), lambda i, pt: (i, 0))
    const = lambda a: pl.BlockSpec(a.shape, lambda i, pt: (0,) * a.ndim,
                                   pipeline_mode=pl.Buffered(1))
    per_seq = lambda a: pl.BlockSpec((per_step,) + a.shape[1:], lambda i, pt: (i, 0, 0))
    hbm = pl.BlockSpec(memory_space=pl.ANY)
    grid_spec = pltpu.PrefetchScalarGridSpec(
        num_scalar_prefetch=1,
        grid=(n_steps,),
        in_specs=[row(dm), row(DIFF_VW), row(RET_VW), const(w_out), const(g_ffn),
                  const(w_ffn_in), const(w_ffn_out), const(g_final), const(dl), const(subln_g),
                  per_seq(qblk), per_seq(k_new16), per_seq(v_new16), hbm, hbm],
        out_specs=(row(dm), pl.BlockSpec((per_step, t_seq, DIFF_VW), lambda i, pt: (i, 0, 0))),
        scratch_shapes=[pltpu.VMEM((tm, d_ff), BF16),
                        pltpu.VMEM((N_PAGE_SLOTS, n_pages, DIFF_W, page), F32),
                        pltpu.VMEM((N_PAGE_SLOTS, n_pages, page * N_DIFF_HEADS, DIFF_V_DIM), F32),
                        pltpu.SemaphoreType.DMA((N_PAGE_SLOTS,))],
    )
    return pl.pallas_call(
        functools.partial(_ffn_sattn_kernel, fc=fc, n_seq=db, n_pages=n_pages),
        grid_spec=grid_spec,
        out_shape=(jax.ShapeDtypeStruct((n, dm), F32),
                   jax.ShapeDtypeStruct((db, t_seq, DIFF_VW), BF16)),
        compiler_params=_params("arbitrary"),
        name="merge_ffn_sample_attn",
    )(page_table.reshape(-1), x, d, r, w_out, g_ffn, w_ffn_in, w_ffn_out, g_final,
      dl, subln_g, qblk, k_new16, v_new16, kc, vc)


def _ffn_chunk(d_ff):
    for fc in (512, 256, 128):
        if d_ff % fc == 0:
            return fc
    return d_ff


def kernel(x_prompt, x_sample, cache_diff_k, cache_diff_v, state_ret, page_table, norm_mix_g, w_in, diff_lambda, diff_subln_g, w_out, norm_ffn_g, w_ffn_in, w_ffn_out, norm_final_g):
    bsz, seq, dm = x_prompt.shape
    db, t_seq, _ = x_sample.shape
    n_pages = page_table.shape[1]
    page = cache_diff_k.shape[2]
    past = n_pages * page
    assert w_in.shape[0] == 1, "single layer"
    assert LANES % t_seq == 0 and (db * t_seq) % LANES == 0

    w_in16 = w_in[0].astype(BF16)
    w_out16 = w_out[0].astype(BF16)
    w_ffn_in16 = w_ffn_in[0].astype(BF16)
    w_ffn_out16 = w_ffn_out[0].astype(BF16)
    g_mix, g_ffn = norm_mix_g, norm_ffn_g
    g_final = norm_final_g.reshape(1, dm)
    dl = diff_lambda[0]
    fc = _ffn_chunk(w_ffn_out.shape[1])

    tm = min(512, seq)
    tabs_p = _rotary_tables(jnp.arange(seq))
    qt16, k16, kt32, v32, vt16, rq, rkt, rv, rg = _inproj(x_prompt, g_mix, w_in16, tabs_p, tm)
    d_p = _prompt_attention(dl, diff_subln_g, qt16, k16, vt16, min(1024, seq), min(512, seq))
    r_p, ret_state_p = _prompt_retention(rq, rkt, rv, rg, min(512, seq))
    k_prompt = jnp.swapaxes(kt32, 1, 2).reshape(1, bsz, seq, N_DIFF_HEADS, 2, DIFF_QK_DIM)
    v_prompt = v32.reshape(1, bsz, seq, N_DIFF_HEADS, DIFF_V_DIM)
    ret_prompt = ret_state_p[None]

    n_s = db * t_seq
    pos_s = past + (jnp.arange(n_s) % t_seq)
    tabs_s = _rotary_tables(pos_s)
    tm_s = min(512, n_s)
    qts, ks16, kts32, vs32, _, rqs, rkts, rvs, rgs = _inproj(
        x_sample.reshape(1, n_s, dm), g_mix, w_in16, tabs_s, tm_s)
    k_s = kts32[0].T
    qs = qts[0].T
    r_idx = np.arange(2 * N_DIFF_HEADS * t_seq)
    r_map, r_head, r_tok = r_idx // (N_DIFF_HEADS * t_seq), (r_idx // t_seq) % N_DIFF_HEADS, r_idx % t_seq
    feat_owner = np.arange(DIFF_W) // DIFF_QK_DIM
    sel = (feat_owner[None, :] == (r_head * 2 + r_map)[:, None])
    qs3 = qs.reshape(db, t_seq, DIFF_W)
    qblk = jnp.where(sel[None], qs3[:, r_tok, :], jnp.zeros((), BF16))
    kc = jnp.transpose(cache_diff_k[0], (0, 2, 3, 4, 1)).reshape(-1, DIFF_W, page)
    y_prompt, d_s = _merge_ffn_sample_attention(
        x_prompt.reshape(bsz * seq, dm), d_p.reshape(bsz * seq, DIFF_VW),
        r_p.reshape(bsz * seq, RET_VW), w_out16, g_ffn, w_ffn_in16, w_ffn_out16, g_final, tm, fc,
        page_table, dl, diff_subln_g, qblk, ks16.reshape(db, t_seq, DIFF_W),
        vs32.astype(BF16).reshape(db, t_seq, DIFF_VW), kc,
        cache_diff_v[0].reshape(-1, page * N_DIFF_HEADS, DIFF_V_DIM))
    y_prompt = y_prompt.reshape(bsz, seq, dm)
    r_s, ret_state_s = _sample_retention(rqs, rkts, rvs, rgs, state_ret[0], t_seq)
    y_sample = _merge_ffn(x_sample.reshape(n_s, dm), d_s.reshape(n_s, DIFF_VW),
                          r_s.reshape(n_s, RET_VW), w_out16, g_ffn, w_ffn_in16,
                          w_ffn_out16, g_final, tm_s, fc).reshape(db, t_seq, dm)
    k_sample = k_s.reshape(1, db, t_seq, N_DIFF_HEADS, 2, DIFF_QK_DIM)
    v_sample = vs32.reshape(1, db, t_seq, N_DIFF_HEADS, DIFF_V_DIM)
    ret_sample = ret_state_s[None]

    return (y_prompt, y_sample, k_prompt, v_prompt, ret_prompt, k_sample, v_sample, ret_sample)
```

```python
import functools
import math

import jax
import jax.numpy as jnp
import numpy as np
from jax import lax
from jax.experimental import pallas as pl
from jax.experimental.pallas import tpu as pltpu

F32 = jnp.float32
BF16 = jnp.bfloat16

N_DIFF_HEADS = 4
DIFF_QK_DIM = 64
DIFF_V_DIM = 128
ROT_DIM = 16
ROPE_THETA = 500000.0
N_RET_HEADS = 4
RET_QK_DIM = 64
RET_V_DIM = 128
RET_THETA = 10000.0
RET_CHUNK = 128
NORM_EPS = 1e-6
SUBLN_EPS = 1e-5
LAM_INIT = 0.8 - 0.6 * math.exp(-0.3 * 0)
MASK_VALUE = -1e30
Q_SCALE = DIFF_QK_DIM ** -0.5 * math.log2(math.e)

DIFF_W = N_DIFF_HEADS * 2 * DIFF_QK_DIM
DIFF_VW = N_DIFF_HEADS * DIFF_V_DIM
RET_QW = N_RET_HEADS * RET_QK_DIM
RET_VW = N_RET_HEADS * RET_V_DIM
W_IN_OFFSETS = tuple(int(o) for o in np.cumsum([0, DIFF_W, DIFF_W, DIFF_VW, RET_QW, RET_QW, RET_VW]))

LANES = 128
ACC_PAD = 16
N_PAGE_SLOTS = 2
VMEM_LIMIT = 56 * 1024 * 1024


def _params(*sem, flags=None):
    return pltpu.CompilerParams(dimension_semantics=sem, vmem_limit_bytes=VMEM_LIMIT, flags=flags)


def _const_spec(shape):
    nd = len(shape)
    return pl.BlockSpec(shape, lambda *_: (0,) * nd, pipeline_mode=pl.Buffered(1))


def _rms(x, eps):
    return x * lax.rsqrt(jnp.mean(x * x, axis=-1, keepdims=True) + eps)


def _inv_freq(freq_idx, dim, theta):
    return 1.0 / (jnp.float32(theta) ** (jnp.asarray(2 * freq_idx, F32) / dim))


def _rotary_tables(pos):
    posf = pos.astype(F32)

    def feature_tables(half, dim, theta, scale):
        ang = _inv_freq(np.arange(half), dim, theta)[:, None] * posf[None, :]
        return jnp.cos(ang) * scale, jnp.sin(ang) * scale

    return (feature_tables(ROT_DIM // 2, ROT_DIM, ROPE_THETA, Q_SCALE)
            + feature_tables(ROT_DIM // 2, ROT_DIM, ROPE_THETA, 1.0)
            + feature_tables(RET_QK_DIM // 2, RET_QK_DIM, RET_THETA, 1.0)
            + feature_tables(RET_QK_DIM // 2, RET_QK_DIM, RET_THETA, RET_QK_DIM ** -0.5))


def _log_gamma():
    return jnp.log(1.0 - 2.0 ** (-5.0 - jnp.arange(N_RET_HEADS, dtype=F32)))


def _retention_tables(chunk, n_tok):
    lg = _log_gamma()
    idx = jnp.arange(n_tok)
    loc = (idx % chunk).astype(F32)
    rel = loc[:, None] - loc[None, :]
    same = (idx[:, None] // chunk) == (idx[None, :] // chunk)
    decay = jnp.where(same[None] & (rel >= 0)[None],
                      jnp.exp(lg[:, None, None] * jnp.maximum(rel, 0.0)[None]), 0.0)
    qdec = jnp.exp(lg[:, None] * (loc[None, :] + 1.0))[:, :, None]
    kdec = jnp.exp(lg[:, None] * (chunk - 1.0 - loc[None, :]))[:, None, :]
    gc = jnp.broadcast_to(jnp.exp(lg * chunk)[:, None, None], (N_RET_HEADS, 1, LANES))
    return decay, qdec, kdec, gc


def _inproj_kernel(x_ref, g_ref, w_ref,
                   cqt_ref, sqt_ref, ckt_ref, skt_ref, crqt_ref, srqt_ref, crkt_ref, srkt_ref,
                   *refs, with_retention):
    if with_retention:
        (dec_ref, qdec_ref, kdec_ref, gc_ref,
         qt_ref, k16_ref, kt32_ref, v32_ref, vt16_ref, r_ref, st_ref,
         rqt_scr, rq_dst, rkt_dst, rv_dst, rg_dst, state) = refs
    else:
        (qt_ref, k16_ref, kt32_ref, v32_ref, vt16_ref, rq_ref, rkt_ref, rv_ref, rg_ref,
         rqt_scr) = refs
        rq_dst, rkt_dst, rv_dst, rg_dst = rq_ref.at[0], rkt_ref.at[0], rv_ref.at[0], rg_ref.at[0]
    x = x_ref[0]
    hb = (_rms(x, NORM_EPS) * g_ref[...]).astype(BF16)

    def mm(lo, width):
        return jnp.dot(hb, w_ref[:, lo:lo + width].astype(BF16), preferred_element_type=F32)

    def rot_t(zt, base, width, half, c, s, rest_scale=None):
        x1, x2 = zt[base:base + half], zt[base + half:base + 2 * half]
        parts = [x1 * c - x2 * s, x2 * c + x1 * s]
        if 2 * half < width:
            rest = zt[base + 2 * half:base + width]
            parts.append(rest if rest_scale is None else rest * rest_scale)
        return jnp.concatenate(parts, axis=0)

    o_dq, o_dk, o_dv, o_rq, o_rk, o_rv, o_rg = W_IN_OFFSETS

    def retention_group():
        zt = mm(o_rq, RET_QW).T
        c, s = crqt_ref[...], srqt_ref[...]
        for h in range(N_RET_HEADS):
            b = h * RET_QK_DIM
            rqt_scr[b:b + RET_QK_DIM, :] = rot_t(zt, b, RET_QK_DIM, RET_QK_DIM // 2, c, s)
        rq_dst[...] = rqt_scr[...].T.astype(BF16)
        zt = mm(o_rk, RET_QW).T
        c, s = crkt_ref[...], srkt_ref[...]
        for h in range(N_RET_HEADS):
            b = h * RET_QK_DIM
            rkt_dst[b:b + RET_QK_DIM, :] = rot_t(zt, b, RET_QK_DIM, RET_QK_DIM // 2, c, s)
        rv_dst[...] = mm(o_rv, RET_VW).astype(BF16)
        rg_dst[...] = mm(o_rg, RET_VW)

    def diff_group():
        hr = ROT_DIM // 2
        zt = mm(o_dq, DIFF_W).T
        c, s = cqt_ref[...], sqt_ref[...]
        for g in range(N_DIFF_HEADS * 2):
            b = g * DIFF_QK_DIM
            qt_ref[0, b:b + DIFF_QK_DIM, :] = rot_t(
                zt, b, DIFF_QK_DIM, hr, c, s, rest_scale=Q_SCALE).astype(BF16)
        zt = mm(o_dk, DIFF_W).T
        c, s = ckt_ref[...], skt_ref[...]
        for g in range(N_DIFF_HEADS * 2):
            b = g * DIFF_QK_DIM
            kt32_ref[0, b:b + DIFF_QK_DIM, :] = rot_t(zt, b, DIFF_QK_DIM, hr, c, s)
        k16_ref[0] = kt32_ref[0].T.astype(BF16)
        zv = mm(o_dv, DIFF_VW)
        for h in range(N_DIFF_HEADS):
            v32_ref[0, pl.ds(h, zv.shape[0], stride=N_DIFF_HEADS), :] = (
                zv[:, h * DIFF_V_DIM:(h + 1) * DIFF_V_DIM])
        vt16_ref[0] = zv.T.astype(BF16)

    retention_group()
    if with_retention:
        _retention_step(rq_dst, rkt_dst, rv_dst, rg_dst, dec_ref, qdec_ref, kdec_ref, gc_ref,
                        r_ref.at[0], st_ref.at[0], state, pl.program_id(1))
    diff_group()


def _inproj(x, g, w16, tables, tm, with_retention):
    b, s, d = x.shape
    row = lambda width: pl.BlockSpec((1, tm, width), lambda bi, i: (bi, i, 0))
    col = lambda height: pl.BlockSpec((1, height, tm), lambda bi, i: (bi, 0, i))
    ttab = lambda height: pl.BlockSpec((height, tm), lambda bi, i: (0, i))
    n_pair = N_RET_HEADS // 2
    out_shape = [
        jax.ShapeDtypeStruct((b, DIFF_W, s), BF16),
        jax.ShapeDtypeStruct((b, s, DIFF_W), BF16),
        jax.ShapeDtypeStruct((b, DIFF_W, s), F32),
        jax.ShapeDtypeStruct((b, s * N_DIFF_HEADS, DIFF_V_DIM), F32),
        jax.ShapeDtypeStruct((b, DIFF_VW, s), BF16),
    ]
    v_rows = pl.BlockSpec((1, tm * N_DIFF_HEADS, DIFF_V_DIM), lambda bi, i: (bi, i, 0))
    out_specs = [col(DIFF_W), row(DIFF_W), col(DIFF_W), v_rows, col(DIFF_VW)]
    in_specs = ([row(d), _const_spec((1, d)), _const_spec(w16.shape)]
                + [ttab(ROT_DIM // 2)] * 4 + [ttab(RET_QK_DIM // 2)] * 4)
    scratch = [pltpu.VMEM((RET_QW, tm), F32)]
    operands = [x, g, w16, *tables]
    ret_inputs = [((tm, RET_QW), BF16), ((RET_QW, tm), F32), ((tm, RET_VW), BF16), ((tm, RET_VW), F32)]
    if with_retention:
        ret_tables = _retention_tables(RET_CHUNK, RET_CHUNK)
        in_specs += [_const_spec(t.shape) for t in ret_tables]
        operands += list(ret_tables)
        out_shape += [jax.ShapeDtypeStruct((b, s, RET_VW), BF16),
                      jax.ShapeDtypeStruct((b, n_pair, LANES, RET_V_DIM), F32)]
        out_specs += [row(RET_VW),
                      pl.BlockSpec((1, n_pair, LANES, RET_V_DIM), lambda bi, i: (bi, 0, 0, 0))]
        scratch += [pltpu.VMEM(shape, dt) for shape, dt in ret_inputs]
        scratch += [pltpu.VMEM((n_pair, LANES, RET_V_DIM), F32)]
    else:
        out_shape += [jax.ShapeDtypeStruct((b, s, RET_QW), BF16),
                      jax.ShapeDtypeStruct((b, RET_QW, s), F32),
                      jax.ShapeDtypeStruct((b, s, RET_VW), BF16),
                      jax.ShapeDtypeStruct((b, s, RET_VW), F32)]
        out_specs += [row(RET_QW), col(RET_QW), row(RET_VW), row(RET_VW)]
    return pl.pallas_call(
        functools.partial(_inproj_kernel, with_retention=with_retention),
        grid=(b, s // tm),
        in_specs=in_specs,
        out_specs=out_specs,
        out_shape=out_shape,
        scratch_shapes=scratch,
        compiler_params=_params("parallel", "arbitrary" if with_retention else "parallel"),
        name="inproj_ret" if with_retention else "inproj",
    )(*operands)


def _lambda_full(dl_ref):
    lp = dl_ref[...]
    a = jnp.sum(lp[0:1] * lp[1:2], axis=-1, keepdims=True)
    b = jnp.sum(lp[2:3] * lp[3:4], axis=-1, keepdims=True)
    return jnp.exp(a) - jnp.exp(b) + LAM_INIT


def _subln(o, g):
    return _rms(o, SUBLN_EPS) * g * (1.0 - LAM_INIT)


def _attn_kernel(dl_ref, g_ref, qt_ref, qt_next_ref, k_ref, vt_ref, o_ref, acc, s_scr, p_scr,
                 *, tq, tk):
    i = pl.program_id(2)
    first_step = (pl.program_id(0) == 0) & (pl.program_id(1) == 0) & (i == 0)

    def score_columns(qt):
        row = lax.broadcasted_iota(jnp.int32, (qt.shape[0], tk), 0)
        groups = []
        for half in range(2):
            qh = qt[:, half * tk:(half + 1) * tk]
            groups += [jnp.where(row < DIFF_QK_DIM, qh, jnp.zeros_like(qh)),
                       jnp.where(row >= DIFF_QK_DIM, qh, jnp.zeros_like(qh))]
        return jnp.concatenate(groups, axis=1)

    qtb = score_columns(qt_ref[0])
    acc[...] = jnp.zeros_like(acc)
    n_full = 2 * i
    all_cols, lo, hi = slice(0, 2 * tq), slice(0, tq), slice(tq, 2 * tq)

    def issue_scores(j, par, cols, q=None):
        off = pl.multiple_of(j * tk, tk)
        q = qtb if q is None else q
        s_scr[par, :, cols] = jnp.dot(k_ref[0, pl.ds(off, tk), :], q[:, cols],
                                      preferred_element_type=F32)

    def softmax_stage(par, m, cols, causal):
        s = s_scr[par, :, cols]
        if causal:
            c = lax.broadcasted_iota(jnp.int32, (tk, tq), 1)
            c = jnp.where(c >= tk, c - tk, c)
            s = jnp.where(lax.broadcasted_iota(jnp.int32, (tk, tq), 0) <= c, s, MASK_VALUE)
        m_new = jnp.maximum(m, jnp.max(s, axis=0, keepdims=True))
        alpha = jnp.exp2(m - m_new)
        p_scr[par, :, cols] = jnp.exp2(s - m_new).astype(BF16)
        return m_new, alpha

    ones_rows = (lax.broadcasted_iota(jnp.int32, (ACC_PAD, tk), 0) == 0).astype(BF16)

    def value_stage(j, par, alpha, cols):
        off = pl.multiple_of(j * tk, tk)
        vt = jnp.concatenate([vt_ref[0, :, pl.ds(off, tk)], ones_rows], axis=0)
        acc[:, cols] = alpha * acc[:, cols] + jnp.dot(vt, p_scr[par, :, cols],
                                                      preferred_element_type=F32)

    def substep(j, par, m, alpha_prev):
        issue_scores(j + 1, 1 - par, all_cols)
        value_stage(jnp.maximum(j - 1, 0), 1 - par, alpha_prev, all_cols)
        return softmax_stage(par, m, all_cols, False)

    def body(t, carry):
        m, alpha = substep(2 * t, 0, *carry)
        return substep(2 * t + 1, 1, m, alpha)

    @pl.when(i == 0)
    def _():
        issue_scores(0, 0, all_cols)

    @pl.when(first_step)
    def _():
        p_scr[1] = jnp.zeros((tk, 2 * tq), BF16)

    carry = (jnp.full((1, 2 * tq), MASK_VALUE, F32), jnp.ones((1, 2 * tq), F32))
    m, alpha = lax.fori_loop(0, i, body, carry)
    issue_scores(n_full + 1, 1, hi)
    value_stage(jnp.maximum(n_full - 1, 0), 1, alpha, all_cols)
    _, alpha_lo = softmax_stage(0, m[:, lo], lo, True)
    m_hi, alpha_hi = softmax_stage(0, m[:, hi], hi, False)
    value_stage(n_full, 0, jnp.concatenate([alpha_lo, alpha_hi], axis=1), all_cols)
    _, alpha_hi = softmax_stage(1, m_hi, hi, True)
    value_stage(n_full + 1, 1, alpha_hi, hi)
    issue_scores(0, 0, all_cols, q=score_columns(qt_next_ref[0]))
    p_scr[1] = jnp.zeros((tk, 2 * tq), BF16)

    lam = _lambda_full(dl_ref)
    out = []
    for half in range(2):
        c0 = slice(2 * half * tk, (2 * half + 1) * tk)
        c1 = slice((2 * half + 1) * tk, (2 * half + 2) * tk)
        out.append(acc[0:DIFF_V_DIM, c0] / acc[DIFF_V_DIM:DIFF_V_DIM + 1, c0]
                   - lam * (acc[0:DIFF_V_DIM, c1] / acc[DIFF_V_DIM:DIFF_V_DIM + 1, c1]))
    ot = jnp.concatenate(out, axis=1)
    o_ref[0] = _subln(ot.T, g_ref[...]).astype(o_ref.dtype)


def _prompt_attention(dl, subln_g, qt16, k16, vt16, tq, tk):
    b, s, _ = k16.shape
    assert tq == 2 * tk, "a query block spans two key blocks (two scratch slots)"
    n_q = s // tq
    kernel = functools.partial(_attn_kernel, tq=tq, tk=tk)
    return pl.pallas_call(
        kernel,
        grid=(b, N_DIFF_HEADS, s // tq),
        in_specs=[
            _const_spec(dl.shape),
            _const_spec(subln_g.shape),
            pl.BlockSpec((1, LANES, tq), lambda bi, h, i: (bi, h, i)),
            pl.BlockSpec((1, LANES, tq), lambda bi, h, i: (bi, h, jnp.minimum(i + 1, n_q - 1))),
            pl.BlockSpec((1, s, LANES), lambda bi, h, i: (bi, 0, h)),
            pl.BlockSpec((1, LANES, s), lambda bi, h, i: (bi, h, 0)),
        ],
        out_specs=pl.BlockSpec((1, tq, LANES), lambda bi, h, i: (bi, i, h)),
        out_shape=jax.ShapeDtypeStruct((b, s, DIFF_VW), BF16),
        scratch_shapes=[pltpu.VMEM((DIFF_V_DIM + ACC_PAD, 2 * tq), F32),
                        pltpu.VMEM((2, tk, 2 * tq), F32),
                        pltpu.VMEM((2, tk, 2 * tq), BF16)],
        compiler_params=_params("arbitrary", "arbitrary", "arbitrary"),
        name="prompt_attn",
    )(dl, subln_g, qt16, qt16, k16, vt16)


def _gated_norm(o, gate):
    return jax.nn.silu(gate) * _rms(o, NORM_EPS)


def _retention_step(rq_ref, rkt_ref, rv_ref, rg_ref, dec_ref, qdec_ref, kdec_ref, gc_ref,
                    r_ref, st_ref, state, j):
    n_chunks = rq_ref.shape[0] // RET_CHUNK
    c = RET_CHUNK
    lane = lax.broadcasted_iota(jnp.int32, (c, LANES), 1)
    n_pair = N_RET_HEADS // 2
    st = [jnp.where(j == 0, 0.0, state[hp]) for hp in range(n_pair)]
    for ci in range(n_chunks):
        rows = slice(ci * c, (ci + 1) * c)
        for hp in range(n_pair):
            cols = slice(hp * LANES, (hp + 1) * LANES)
            qp = rq_ref[rows, cols]
            ktp = rkt_ref[cols, rows]
            ktp16 = ktp.astype(BF16)
            stp = st[hp]
            stp16 = stp.astype(BF16)
            q2 = jnp.concatenate(
                [jnp.where((lane >= e * RET_QK_DIM) & (lane < (e + 1) * RET_QK_DIM),
                           qp, jnp.zeros_like(qp)) for e in range(2)], axis=0)
            sc2 = jnp.dot(q2, ktp16, preferred_element_type=F32)
            cross2 = jnp.dot(q2, stp16, preferred_element_type=F32)
            new = []
            for e in range(2):
                h = hp * 2 + e
                hs = slice(e * RET_QK_DIM, (e + 1) * RET_QK_DIM)
                hv = slice(h * RET_V_DIM, (h + 1) * RET_V_DIM)
                sc = sc2[e * c:(e + 1) * c] * dec_ref[h]
                vh = rv_ref[rows, hv]
                kd = (ktp[hs] * kdec_ref[h]).astype(BF16)
                both = jnp.dot(jnp.concatenate([sc.astype(BF16), kd], axis=0), vh,
                               preferred_element_type=F32)
                cross = cross2[e * c:(e + 1) * c] * qdec_ref[h]
                r_ref[rows, hv] = _gated_norm(both[:c] + cross, rg_ref[rows, hv]).astype(r_ref.dtype)
                new.append(gc_ref[h] * stp[hs] + both[c:])
            st[hp] = jnp.concatenate(new, axis=0)
    for hp in range(n_pair):
        state[hp] = st[hp]
        st_ref[hp] = st[hp]


def _sret_kernel(rq_ref, rkt_ref, rv_ref, rg_ref, st_in_ref, dec_ref, qdec_ref, kdec_ref, gc_ref,
                 r_ref, st_out_ref, *, t_seq):
    n = LANES
    nb = n // t_seq
    lane = lax.broadcasted_iota(jnp.int32, (n, LANES), 1)
    tok_r = lax.broadcasted_iota(jnp.int32, (nb, n, LANES), 1) // t_seq
    seq_r = lax.broadcasted_iota(jnp.int32, (nb, n, LANES), 0)
    row_in_seq = tok_r == seq_r
    tok_l = lax.broadcasted_iota(jnp.int32, (nb, RET_QK_DIM, n), 2) // t_seq
    seq_l = lax.broadcasted_iota(jnp.int32, (nb, RET_QK_DIM, n), 0)
    lane_in_seq = tok_l == seq_l
    for hp in range(N_RET_HEADS // 2):
        cols = slice(hp * LANES, (hp + 1) * LANES)
        qp = rq_ref[0, :, cols]
        ktp = rkt_ref[0, cols, :]
        ktp16 = ktp.astype(BF16)
        stp = st_in_ref[:, hp]
        st16 = stp.astype(BF16).reshape(nb * LANES, RET_V_DIM)
        for e in range(2):
            h = hp * 2 + e
            hs = slice(e * RET_QK_DIM, (e + 1) * RET_QK_DIM)
            hv = slice(h * RET_V_DIM, (h + 1) * RET_V_DIM)
            in_head = (lane >= e * RET_QK_DIM) & (lane < (e + 1) * RET_QK_DIM)
            qh = jnp.where(in_head, qp, jnp.zeros_like(qp))
            sc = jnp.dot(qh, ktp16, preferred_element_type=F32) * dec_ref[h]
            vh = rv_ref[0, :, hv]
            inner = jnp.dot(sc.astype(BF16), vh, preferred_element_type=F32)
            qbd = jnp.where(row_in_seq, qh[None], jnp.zeros_like(qh)[None])
            qbd = jnp.concatenate([qbd[b] for b in range(nb)], axis=1)
            cross = jnp.dot(qbd, st16, preferred_element_type=F32) * qdec_ref[h]
            r_ref[0, :, hv] = _gated_norm(inner + cross, rg_ref[0, :, hv]).astype(r_ref.dtype)
            kd = (ktp[hs] * kdec_ref[h]).astype(BF16)
            kds = jnp.where(lane_in_seq, kd[None], jnp.zeros_like(kd)[None])
            upd = jnp.dot(kds.reshape(nb * RET_QK_DIM, n), vh, preferred_element_type=F32)
            st_out_ref[:, hp, hs, :] = (gc_ref[h] * stp[:, hs, :]
                                        + upd.reshape(nb, RET_QK_DIM, RET_V_DIM))


def _sample_retention(rq, rkt, rv, rg, state, t_seq):
    n_tok = rq.shape[1]
    db = state.shape[0]
    nb = LANES // t_seq
    n_pair = N_RET_HEADS // 2
    st_pairs = state.reshape(db, n_pair, LANES, RET_V_DIM)
    tables = _retention_tables(t_seq, LANES)
    row = lambda width: pl.BlockSpec((1, LANES, width), lambda j: (0, j, 0))
    st_spec = pl.BlockSpec((nb, n_pair, LANES, RET_V_DIM), lambda j: (j, 0, 0, 0))
    r, st = pl.pallas_call(
        functools.partial(_sret_kernel, t_seq=t_seq),
        grid=(n_tok // LANES,),
        in_specs=[row(RET_QW), pl.BlockSpec((1, RET_QW, LANES), lambda j: (0, 0, j)),
                  row(RET_VW), row(RET_VW), st_spec] + [_const_spec(t.shape) for t in tables],
        out_specs=(row(RET_VW), st_spec),
        out_shape=(jax.ShapeDtypeStruct((1, n_tok, RET_VW), BF16),
                   jax.ShapeDtypeStruct(st_pairs.shape, F32)),
        compiler_params=_params("parallel"),
        name="sample_ret",
    )(rq, rkt, rv, rg, st_pairs, *tables)
    return r, st.reshape(state.shape)


def _sattn_compute(dl_ref, g_ref, qb, kn, vn, k_page, v_page, n_pages, page):
    t_seq = kn.shape[0]
    nh = N_DIFF_HEADS
    half = nh * t_seq
    qb = qb.astype(F32)
    s = jnp.concatenate(
        [jnp.dot(qb, k_page(p), preferred_element_type=F32) for p in range(n_pages)],
        axis=1)
    kn = kn.astype(F32)
    t_row = lax.broadcasted_iota(jnp.int32, (2 * half, 1), 0) % t_seq
    s_new = []
    for tk in range(t_seq):
        col = jnp.sum(qb * kn[tk:tk + 1, :], axis=-1, keepdims=True)
        s_new.append(jnp.where(tk <= t_row, col, MASK_VALUE))
    m = jnp.max(s, axis=-1, keepdims=True)
    for col in s_new:
        m = jnp.maximum(m, col)
    p = jnp.exp2(s - m)
    p_new = [jnp.exp2(col - m) for col in s_new]
    l = jnp.sum(p, axis=-1, keepdims=True)
    for col in p_new:
        l = l + col
    lam = _lambda_full(dl_ref)
    a = (p[:half] / l[:half] - lam * (p[half:] / l[half:])).astype(BF16)
    a_new = [(c[:half] / l[:half] - lam * (c[half:] / l[half:])).astype(BF16).astype(F32)
             for c in p_new]
    a_pages = jnp.concatenate([a[:, pg * page:(pg + 1) * page] for pg in range(n_pages)],
                              axis=0)
    spread = (lax.broadcasted_iota(jnp.int32, (page, page * nh), 1) // nh
              == lax.broadcasted_iota(jnp.int32, (page, page * nh), 0)).astype(BF16)
    ax = jnp.dot(a_pages, spread, preferred_element_type=F32)
    col_head = lax.broadcasted_iota(jnp.int32, ax.shape, 1) % nh
    row_head = (lax.broadcasted_iota(jnp.int32, ax.shape, 0) % half) // t_seq
    ax = jnp.where(col_head == row_head, ax, 0.0)
    acc = jnp.zeros((half, DIFF_V_DIM), F32)
    for pg in range(n_pages):
        acc = acc + jnp.dot(ax[pg * half:(pg + 1) * half], v_page(pg),
                            preferred_element_type=F32)
    vn = vn.astype(F32)
    out = []
    for h in range(nh):
        hv = slice(h * DIFF_V_DIM, (h + 1) * DIFF_V_DIM)
        rows = slice(h * t_seq, (h + 1) * t_seq)
        o = acc[rows]
        for tk in range(t_seq):
            o = o + a_new[tk][rows] * vn[tk:tk + 1, hv]
        out.append(_subln(o, g_ref[...]))
    return jnp.concatenate(out, axis=1)


def _ffn_stages(x_ref, d_ref, r_ref, wo_ref, gf_ref, wi_ref, wo2_ref, gl_ref, y_ref, act, fc):
    d_ff = wo2_ref.shape[0]
    dm = x_ref.shape[1]
    live = {}

    def head():
        mix = (jnp.dot(d_ref[...], wo_ref[0:DIFF_VW, :], preferred_element_type=F32)
               + jnp.dot(r_ref[...], wo_ref[DIFF_VW:, :], preferred_element_type=F32))
        live["x1"] = x_ref[...] + mix
        live["hb"] = (_rms(live["x1"], NORM_EPS) * gf_ref[...]).astype(BF16)

    def chunk(c):
        def run():
            hb = live["hb"]
            g = jnp.dot(hb, wi_ref[:, c * fc:(c + 1) * fc], preferred_element_type=F32)
            u = jnp.dot(hb, wi_ref[:, d_ff + c * fc:d_ff + (c + 1) * fc],
                        preferred_element_type=F32)
            act[:, c * fc:(c + 1) * fc] = (jax.nn.silu(g) * u).astype(BF16)
        return run

    def tail():
        x2 = live["x1"] + jnp.dot(act[...], wo2_ref[...], preferred_element_type=F32)
        y_ref[...] = _rms(x2, NORM_EPS) * gl_ref[...]

    return ([(wo_ref.shape[0] * dm, head)]
            + [(2 * dm * fc, chunk(c)) for c in range(d_ff // fc)]
            + [(d_ff * dm, tail)])


def _ffn_kernel(*refs, fc):
    for _, stage in _ffn_stages(*refs, fc):
        stage()


def _merge_ffn(x, d, r, w_out, g_ffn, w_ffn_in, w_ffn_out, g_final, tm, fc):
    n, dm = x.shape
    d_ff = w_ffn_out.shape[0]
    row = lambda width: pl.BlockSpec((tm, width), lambda i: (i, 0))
    return pl.pallas_call(
        functools.partial(_ffn_kernel, fc=fc),
        grid=(n // tm,),
        in_specs=[row(dm), row(DIFF_VW), row(RET_VW), _const_spec(w_out.shape),
                  _const_spec(g_ffn.shape), _const_spec(w_ffn_in.shape),
                  _const_spec(w_ffn_out.shape), _const_spec(g_final.shape)],
        out_specs=row(dm),
        out_shape=jax.ShapeDtypeStruct((n, dm), F32),
        scratch_shapes=[pltpu.VMEM((tm, d_ff), BF16)],
        compiler_params=_params("parallel"),
        name="merge_ffn",
    )(x, d, r, w_out, g_ffn, w_ffn_in, w_ffn_out, g_final)


def _ffn_sattn_kernel(pt_ref, x_ref, d_ref, r_ref, wo_ref, gf_ref, wi_ref, wo2_ref, gl_ref,
                      dl_ref, g_ref, qb_ref, kn_ref, vn_ref, kc_hbm, vc_hbm,
                      y_ref, ds_ref, act, kbuf, vbuf, sem, *, fc, n_seq, n_pages):
    step = pl.program_id(0)
    per_step = qb_ref.shape[0]
    page = kbuf.shape[3]

    def page_copies(seq, slot):
        copies = []
        for p in range(n_pages):
            pg = pt_ref[seq * n_pages + p]
            copies.append(pltpu.make_async_copy(kc_hbm.at[pg], kbuf.at[slot, p], sem.at[slot]))
            copies.append(pltpu.make_async_copy(vc_hbm.at[pg], vbuf.at[slot, p], sem.at[slot]))
        return copies

    def start(seq, slot):
        for c in page_copies(seq, slot):
            c.start()

    def wait(seq, slot):
        for c in page_copies(seq, slot):
            c.wait()

    @pl.when(step == 0)
    def _():
        for t in range(N_PAGE_SLOTS):
            start(t, t)

    stages = _ffn_stages(x_ref, d_ref, r_ref, wo_ref, gf_ref, wi_ref, wo2_ref, gl_ref, y_ref, act, fc)
    total = sum(w for w, _ in stages)
    shares = [[] for _ in range(per_step)]
    done = 0
    for w, stage in stages:
        shares[min(per_step - 1, (2 * done + w) * per_step // (2 * total))].append(stage)
        done += w

    for t in range(per_step):
        seq = step * per_step + t
        slot = t % N_PAGE_SLOTS
        wait(seq, slot)
        o = _sattn_compute(dl_ref, g_ref, qb_ref[t], kn_ref[t], vn_ref[t],
                           lambda p: kbuf[slot, p], lambda p: vbuf[slot, p], n_pages, page)
        ds_ref[t] = o.astype(ds_ref.dtype)
        for stage in shares[t]:
            stage()

        @pl.when(seq + N_PAGE_SLOTS < n_seq)
        def _():
            start(seq + N_PAGE_SLOTS, slot)


def _merge_ffn_sample_attention(x, d, r, w_out, g_ffn, w_ffn_in, w_ffn_out, g_final, tm, fc,
                                page_table, dl, subln_g, qblk, k_new16, v_new16, kc, vc):
    n, dm = x.shape
    d_ff = w_ffn_out.shape[0]
    db, n_pages = page_table.shape
    t_seq = k_new16.shape[1]
    page = kc.shape[2]
    n_steps = n // tm
    assert db % n_steps == 0 and (db // n_steps) % N_PAGE_SLOTS == 0, "sequences per FFN step"
    per_step = db // n_steps
    row = lambda width: pl.BlockSpec((tm, width), lambda i, pt: (i, 0))
    const = lambda a: pl.BlockSpec(a.shape, lambda i, pt: (0,) * a.ndim,
                                   pipeline_mode=pl.Buffered(1))
    per_seq = lambda a: pl.BlockSpec((per_step,) + a.shape[1:], lambda i, pt: (i, 0, 0))
    hbm = pl.BlockSpec(memory_space=pl.ANY)
    grid_spec = pltpu.PrefetchScalarGridSpec(
        num_scalar_prefetch=1,
        grid=(n_steps,),
        in_specs=[row(dm), row(DIFF_VW), row(RET_VW), const(w_out), const(g_ffn),
                  const(w_ffn_in), const(w_ffn_out), const(g_final), const(dl), const(subln_g),
                  per_seq(qblk), per_seq(k_new16), per_seq(v_new16), hbm, hbm],
        out_specs=(row(dm), pl.BlockSpec((per_step, t_seq, DIFF_VW), lambda i, pt: (i, 0, 0))),
        scratch_shapes=[pltpu.VMEM((tm, d_ff), BF16),
                        pltpu.VMEM((N_PAGE_SLOTS, n_pages, DIFF_W, page), F32),
                        pltpu.VMEM((N_PAGE_SLOTS, n_pages, page * N_DIFF_HEADS, DIFF_V_DIM), F32),
                        pltpu.SemaphoreType.DMA((N_PAGE_SLOTS,))],
    )
    return pl.pallas_call(
        functools.partial(_ffn_sattn_kernel, fc=fc, n_seq=db, n_pages=n_pages),
        grid_spec=grid_spec,
        out_shape=(jax.ShapeDtypeStruct((n, dm), F32),
                   jax.ShapeDtypeStruct((db, t_seq, DIFF_VW), BF16)),
        compiler_params=_params("arbitrary"),
        name="merge_ffn_sample_attn",
    )(page_table.reshape(-1), x, d, r, w_out, g_ffn, w_ffn_in, w_ffn_out, g_final,
      dl, subln_g, qblk, k_new16, v_new16, kc, vc)


def _ffn_chunk(d_ff):
    for fc in (512, 256, 128):
        if d_ff % fc == 0:
            return fc
    return d_ff


def kernel(x_prompt, x_sample, cache_diff_k, cache_diff_v, state_ret, page_table, norm_mix_g, w_in, diff_lambda, diff_subln_g, w_out, norm_ffn_g, w_ffn_in, w_ffn_out, norm_final_g):
    bsz, seq, dm = x_prompt.shape
    db, t_seq, _ = x_sample.shape
    n_pages = page_table.shape[1]
    page = cache_diff_k.shape[2]
    past = n_pages * page
    assert w_in.shape[0] == 1, "single layer"
    assert LANES % t_seq == 0 and (db * t_seq) % LANES == 0

    w_in0 = w_in[0]
    w_out16 = w_out[0].astype(BF16)
    w_ffn_in16 = w_ffn_in[0].astype(BF16)
    w_ffn_out16 = w_ffn_out[0].astype(BF16)
    g_mix, g_ffn = norm_mix_g, norm_ffn_g
    g_final = norm_final_g.reshape(1, dm)
    dl = diff_lambda[0]
    fc = _ffn_chunk(w_ffn_out.shape[1])

    tm = min(512, seq)
    tabs_p = _rotary_tables(jnp.arange(seq))
    qt16, k16, kt32, v32, vt16, r_p, ret_state_p = _inproj(
        x_prompt, g_mix, w_in0, tabs_p, tm, with_retention=True)
    d_p = _prompt_attention(dl, diff_subln_g, qt16, k16, vt16, min(1024, seq), min(512, seq))
    k_prompt = jnp.swapaxes(kt32, 1, 2).reshape(1, bsz, seq, N_DIFF_HEADS, 2, DIFF_QK_DIM)
    v_prompt = v32.reshape(1, bsz, seq, N_DIFF_HEADS, DIFF_V_DIM)
    ret_prompt = ret_state_p.reshape(1, bsz, N_RET_HEADS, RET_QK_DIM, RET_V_DIM)

    n_s = db * t_seq
    pos_s = past + (jnp.arange(n_s) % t_seq)
    tabs_s = _rotary_tables(pos_s)
    tm_s = min(512, n_s)
    qts, ks16, kts32, vs32, _, rqs, rkts, rvs, rgs = _inproj(
        x_sample.reshape(1, n_s, dm), g_mix, w_in0, tabs_s, tm_s, with_retention=False)
    k_s = kts32[0].T
    qs = qts[0].T
    r_idx = np.arange(2 * N_DIFF_HEADS * t_seq)
    r_map, r_head, r_tok = r_idx // (N_DIFF_HEADS * t_seq), (r_idx // t_seq) % N_DIFF_HEADS, r_idx % t_seq
    feat_owner = np.arange(DIFF_W) // DIFF_QK_DIM
    sel = (feat_owner[None, :] == (r_head * 2 + r_map)[:, None])
    qs3 = qs.reshape(db, t_seq, DIFF_W)
    qblk = jnp.where(sel[None], qs3[:, r_tok, :], jnp.zeros((), BF16))
    kc = jnp.transpose(cache_diff_k[0], (0, 2, 3, 4, 1)).reshape(-1, DIFF_W, page)
    y_prompt, d_s = _merge_ffn_sample_attention(
        x_prompt.reshape(bsz * seq, dm), d_p.reshape(bsz * seq, DIFF_VW),
        r_p.reshape(bsz * seq, RET_VW), w_out16, g_ffn, w_ffn_in16, w_ffn_out16, g_final, tm, fc,
        page_table, dl, diff_subln_g, qblk, ks16.reshape(db, t_seq, DIFF_W),
        vs32.astype(BF16).reshape(db, t_seq, DIFF_VW), kc,
        cache_diff_v[0].reshape(-1, page * N_DIFF_HEADS, DIFF_V_DIM))
    y_prompt = y_prompt.reshape(bsz, seq, dm)
    r_s, ret_state_s = _sample_retention(rqs, rkts, rvs, rgs, state_ret[0], t_seq)
    y_sample = _merge_ffn(x_sample.reshape(n_s, dm), d_s.reshape(n_s, DIFF_VW),
                          r_s.reshape(n_s, RET_VW), w_out16, g_ffn, w_ffn_in16,
                          w_ffn_out16, g_final, tm_s, fc).reshape(db, t_seq, dm)
    k_sample = k_s.reshape(1, db, t_seq, N_DIFF_HEADS, 2, DIFF_QK_DIM)
    v_sample = vs32.reshape(1, db, t_seq, N_DIFF_HEADS, DIFF_V_DIM)
    ret_sample = ret_state_s[None]

    return (y_prompt, y_sample, k_prompt, v_prompt, ret_prompt, k_sample, v_sample, ret_sample)
```

```python
import functools
import math

import jax
import jax.numpy as jnp
import numpy as np
from jax import lax
from jax.experimental import pallas as pl
from jax.experimental.pallas import tpu as pltpu

F32 = jnp.float32
BF16 = jnp.bfloat16

N_DIFF_HEADS = 4
DIFF_QK_DIM = 64
DIFF_V_DIM = 128
ROT_DIM = 16
ROPE_THETA = 500000.0
N_RET_HEADS = 4
RET_QK_DIM = 64
RET_V_DIM = 128
RET_THETA = 10000.0
RET_CHUNK = 128
NORM_EPS = 1e-6
SUBLN_EPS = 1e-5
LAM_INIT = 0.8 - 0.6 * math.exp(-0.3 * 0)
MASK_VALUE = -1e30
Q_SCALE = DIFF_QK_DIM ** -0.5 * math.log2(math.e)

DIFF_W = N_DIFF_HEADS * 2 * DIFF_QK_DIM
DIFF_VW = N_DIFF_HEADS * DIFF_V_DIM
RET_QW = N_RET_HEADS * RET_QK_DIM
RET_VW = N_RET_HEADS * RET_V_DIM
W_IN_OFFSETS = tuple(int(o) for o in np.cumsum([0, DIFF_W, DIFF_W, DIFF_VW, RET_QW, RET_QW, RET_VW]))

LANES = 128
ACC_PAD = 16
N_PAGE_SLOTS = 2
VMEM_LIMIT = 56 * 1024 * 1024


def _params(*sem, flags=None):
    return pltpu.CompilerParams(dimension_semantics=sem, vmem_limit_bytes=VMEM_LIMIT, flags=flags)


def _const_spec(shape):
    nd = len(shape)
    return pl.BlockSpec(shape, lambda *_: (0,) * nd, pipeline_mode=pl.Buffered(1))


def _rms(x, eps):
    return x * lax.rsqrt(jnp.mean(x * x, axis=-1, keepdims=True) + eps)


def _inv_freq(freq_idx, dim, theta):
    return 1.0 / (jnp.float32(theta) ** (jnp.asarray(2 * freq_idx, F32) / dim))


def _rotary_tables(pos):
    posf = pos.astype(F32)

    def feature_tables(half, dim, theta, scale):
        ang = _inv_freq(np.arange(half), dim, theta)[:, None] * posf[None, :]
        return jnp.cos(ang) * scale, jnp.sin(ang) * scale

    return (feature_tables(ROT_DIM // 2, ROT_DIM, ROPE_THETA, Q_SCALE)
            + feature_tables(ROT_DIM // 2, ROT_DIM, ROPE_THETA, 1.0)
            + feature_tables(RET_QK_DIM // 2, RET_QK_DIM, RET_THETA, 1.0)
            + feature_tables(RET_QK_DIM // 2, RET_QK_DIM, RET_THETA, RET_QK_DIM ** -0.5))


def _log_gamma():
    return jnp.log(1.0 - 2.0 ** (-5.0 - jnp.arange(N_RET_HEADS, dtype=F32)))


def _retention_tables(chunk, n_tok):
    lg = _log_gamma()
    idx = jnp.arange(n_tok)
    loc = (idx % chunk).astype(F32)
    rel = loc[:, None] - loc[None, :]
    same = (idx[:, None] // chunk) == (idx[None, :] // chunk)
    decay = jnp.where(same[None] & (rel >= 0)[None],
                      jnp.exp(lg[:, None, None] * jnp.maximum(rel, 0.0)[None]), 0.0)
    qdec = jnp.exp(lg[:, None] * (loc[None, :] + 1.0))[:, :, None]
    kdec = jnp.exp(lg[:, None] * (chunk - 1.0 - loc[None, :]))[:, None, :]
    gc = jnp.broadcast_to(jnp.exp(lg * chunk)[:, None, None], (N_RET_HEADS, 1, LANES))
    return decay, qdec, kdec, gc


def _inproj_kernel(x_ref, g_ref, w_ref,
                   cqt_ref, sqt_ref, ckt_ref, skt_ref, crqt_ref, srqt_ref, crkt_ref, srkt_ref,
                   *refs, with_retention):
    if with_retention:
        (dec_ref, qdec_ref, kdec_ref, gc_ref,
         qt_ref, k16_ref, kt32_ref, v32_ref, vt16_ref, r_ref, st_ref,
         rqt_scr, rq_dst, rkt_dst, rv_dst, rg_dst, state) = refs
    else:
        (qt_ref, k16_ref, kt32_ref, v32_ref, vt16_ref, rq_ref, rkt_ref, rv_ref, rg_ref,
         rqt_scr) = refs
        rq_dst, rkt_dst, rv_dst, rg_dst = rq_ref.at[0], rkt_ref.at[0], rv_ref.at[0], rg_ref.at[0]
    x = x_ref[0]
    hb = (_rms(x, NORM_EPS) * g_ref[...]).astype(BF16)

    def mm(lo, width):
        return jnp.dot(hb, w_ref[:, lo:lo + width].astype(BF16), preferred_element_type=F32)

    def rot_t(zt, base, width, half, c, s, rest_scale=None):
        x1, x2 = zt[base:base + half], zt[base + half:base + 2 * half]
        parts = [x1 * c - x2 * s, x2 * c + x1 * s]
        if 2 * half < width:
            rest = zt[base + 2 * half:base + width]
            parts.append(rest if rest_scale is None else rest * rest_scale)
        return jnp.concatenate(parts, axis=0)

    o_dq, o_dk, o_dv, o_rq, o_rk, o_rv, o_rg = W_IN_OFFSETS

    def retention_group():
        zt = mm(o_rq, RET_QW).T
        c, s = crqt_ref[...], srqt_ref[...]
        for h in range(N_RET_HEADS):
            b = h * RET_QK_DIM
            rqt_scr[b:b + RET_QK_DIM, :] = rot_t(zt, b, RET_QK_DIM, RET_QK_DIM // 2, c, s)
        rq_dst[...] = rqt_scr[...].T.astype(BF16)
        zt = mm(o_rk, RET_QW).T
        c, s = crkt_ref[...], srkt_ref[...]
        for h in range(N_RET_HEADS):
            b = h * RET_QK_DIM
            rkt_dst[b:b + RET_QK_DIM, :] = rot_t(zt, b, RET_QK_DIM, RET_QK_DIM // 2, c, s)
        rv_dst[...] = mm(o_rv, RET_VW).astype(BF16)
        rg_dst[...] = mm(o_rg, RET_VW)

    def diff_group():
        hr = ROT_DIM // 2
        zt = mm(o_dq, DIFF_W).T
        c, s = cqt_ref[...], sqt_ref[...]
        for g in range(N_DIFF_HEADS * 2):
            b = g * DIFF_QK_DIM
            qt_ref[0, b:b + DIFF_QK_DIM, :] = rot_t(
                zt, b, DIFF_QK_DIM, hr, c, s, rest_scale=Q_SCALE).astype(BF16)
        zt = mm(o_dk, DIFF_W).T
        c, s = ckt_ref[...], skt_ref[...]
        for g in range(N_DIFF_HEADS * 2):
            b = g * DIFF_QK_DIM
            kt32_ref[0, b:b + DIFF_QK_DIM, :] = rot_t(zt, b, DIFF_QK_DIM, hr, c, s)
        k16_ref[0] = kt32_ref[0].T.astype(BF16)
        zv = mm(o_dv, DIFF_VW)
        for h in range(N_DIFF_HEADS):
            v32_ref[0, pl.ds(h, zv.shape[0], stride=N_DIFF_HEADS), :] = (
                zv[:, h * DIFF_V_DIM:(h + 1) * DIFF_V_DIM])
        vt16_ref[0] = zv.T.astype(BF16)

    retention_group()
    if with_retention:
        _retention_step(rq_dst, rkt_dst, rv_dst, rg_dst, dec_ref, qdec_ref, kdec_ref, gc_ref,
                        r_ref.at[0], st_ref.at[0], state, pl.program_id(1))
    diff_group()


def _inproj(x, g, w_in, tables, tm, with_retention):
    b, s, d = x.shape
    row = lambda width: pl.BlockSpec((1, tm, width), lambda bi, i: (bi, i, 0))
    col = lambda height: pl.BlockSpec((1, height, tm), lambda bi, i: (bi, 0, i))
    ttab = lambda height: pl.BlockSpec((height, tm), lambda bi, i: (0, i))
    n_pair = N_RET_HEADS // 2
    out_shape = [
        jax.ShapeDtypeStruct((b, DIFF_W, s), BF16),
        jax.ShapeDtypeStruct((b, s, DIFF_W), BF16),
        jax.ShapeDtypeStruct((b, DIFF_W, s), F32),
        jax.ShapeDtypeStruct((b, s * N_DIFF_HEADS, DIFF_V_DIM), F32),
        jax.ShapeDtypeStruct((b, DIFF_VW, s), BF16),
    ]
    v_rows = pl.BlockSpec((1, tm * N_DIFF_HEADS, DIFF_V_DIM), lambda bi, i: (bi, i, 0))
    out_specs = [col(DIFF_W), row(DIFF_W), col(DIFF_W), v_rows, col(DIFF_VW)]
    in_specs = ([row(d), _const_spec((1, d)), _const_spec(w_in.shape)]
                + [ttab(ROT_DIM // 2)] * 4 + [ttab(RET_QK_DIM // 2)] * 4)
    scratch = [pltpu.VMEM((RET_QW, tm), F32)]
    operands = [x, g, w_in, *tables]
    ret_inputs = [((tm, RET_QW), BF16), ((RET_QW, tm), F32), ((tm, RET_VW), BF16), ((tm, RET_VW), F32)]
    if with_retention:
        ret_tables = _retention_tables(RET_CHUNK, RET_CHUNK)
        in_specs += [_const_spec(t.shape) for t in ret_tables]
        operands += list(ret_tables)
        out_shape += [jax.ShapeDtypeStruct((b, s, RET_VW), BF16),
                      jax.ShapeDtypeStruct((b, n_pair, LANES, RET_V_DIM), F32)]
        out_specs += [row(RET_VW),
                      pl.BlockSpec((1, n_pair, LANES, RET_V_DIM), lambda bi, i: (bi, 0, 0, 0))]
        scratch += [pltpu.VMEM(shape, dt) for shape, dt in ret_inputs]
        scratch += [pltpu.VMEM((n_pair, LANES, RET_V_DIM), F32)]
    else:
        out_shape += [jax.ShapeDtypeStruct((b, s, RET_QW), BF16),
                      jax.ShapeDtypeStruct((b, RET_QW, s), F32),
                      jax.ShapeDtypeStruct((b, s, RET_VW), BF16),
                      jax.ShapeDtypeStruct((b, s, RET_VW), F32)]
        out_specs += [row(RET_QW), col(RET_QW), row(RET_VW), row(RET_VW)]
    return pl.pallas_call(
        functools.partial(_inproj_kernel, with_retention=with_retention),
        grid=(b, s // tm),
        in_specs=in_specs,
        out_specs=out_specs,
        out_shape=out_shape,
        scratch_shapes=scratch,
        compiler_params=_params("parallel", "arbitrary" if with_retention else "parallel"),
        name="inproj_ret" if with_retention else "inproj",
    )(*operands)


def _lambda_full(dl_ref):
    lp = dl_ref[...]
    a = jnp.sum(lp[0:1] * lp[1:2], axis=-1, keepdims=True)
    b = jnp.sum(lp[2:3] * lp[3:4], axis=-1, keepdims=True)
    return jnp.exp(a) - jnp.exp(b) + LAM_INIT


def _subln(o, g):
    return _rms(o, SUBLN_EPS) * g * (1.0 - LAM_INIT)


def _attn_kernel(dl_ref, g_ref, qt_ref, qt_next_ref, k_ref, vt_ref, o_ref, acc, s_scr, p_scr,
                 *, tq, tk):
    i = pl.program_id(2)
    first_step = (pl.program_id(0) == 0) & (pl.program_id(1) == 0) & (i == 0)

    def score_columns(qt):
        row = lax.broadcasted_iota(jnp.int32, (qt.shape[0], tk), 0)
        groups = []
        for half in range(2):
            qh = qt[:, half * tk:(half + 1) * tk]
            groups += [jnp.where(row < DIFF_QK_DIM, qh, jnp.zeros_like(qh)),
                       jnp.where(row >= DIFF_QK_DIM, qh, jnp.zeros_like(qh))]
        return jnp.concatenate(groups, axis=1)

    qtb = score_columns(qt_ref[0])
    acc[...] = jnp.zeros_like(acc)
    n_full = 2 * i
    all_cols, lo, hi = slice(0, 2 * tq), slice(0, tq), slice(tq, 2 * tq)

    def issue_scores(j, par, cols, q=None):
        off = pl.multiple_of(j * tk, tk)
        q = qtb if q is None else q
        s_scr[par, :, cols] = jnp.dot(k_ref[0, pl.ds(off, tk), :], q[:, cols],
                                      preferred_element_type=F32)

    def softmax_stage(par, m, cols, causal):
        s = s_scr[par, :, cols]
        if causal:
            c = lax.broadcasted_iota(jnp.int32, (tk, tq), 1)
            c = jnp.where(c >= tk, c - tk, c)
            s = jnp.where(lax.broadcasted_iota(jnp.int32, (tk, tq), 0) <= c, s, MASK_VALUE)
        m_new = jnp.maximum(m, jnp.max(s, axis=0, keepdims=True))
        alpha = jnp.exp2(m - m_new)
        p_scr[par, :, cols] = jnp.exp2(s - m_new).astype(BF16)
        return m_new, alpha

    ones_rows = (lax.broadcasted_iota(jnp.int32, (ACC_PAD, tk), 0) == 0).astype(BF16)

    def value_stage(j, par, alpha, cols):
        off = pl.multiple_of(j * tk, tk)
        vt = jnp.concatenate([vt_ref[0, :, pl.ds(off, tk)], ones_rows], axis=0)
        acc[:, cols] = alpha * acc[:, cols] + jnp.dot(vt, p_scr[par, :, cols],
                                                      preferred_element_type=F32)

    def substep(j, par, m, alpha_prev):
        issue_scores(j + 1, 1 - par, all_cols)
        value_stage(jnp.maximum(j - 1, 0), 1 - par, alpha_prev, all_cols)
        return softmax_stage(par, m, all_cols, False)

    def body(t, carry):
        m, alpha = substep(2 * t, 0, *carry)
        return substep(2 * t + 1, 1, m, alpha)

    @pl.when(i == 0)
    def _():
        issue_scores(0, 0, all_cols)

    @pl.when(first_step)
    def _():
        p_scr[1] = jnp.zeros((tk, 2 * tq), BF16)

    carry = (jnp.full((1, 2 * tq), MASK_VALUE, F32), jnp.ones((1, 2 * tq), F32))
    m, alpha = lax.fori_loop(0, i, body, carry)
    issue_scores(n_full + 1, 1, hi)
    value_stage(jnp.maximum(n_full - 1, 0), 1, alpha, all_cols)
    _, alpha_lo = softmax_stage(0, m[:, lo], lo, True)
    m_hi, alpha_hi = softmax_stage(0, m[:, hi], hi, False)
    value_stage(n_full, 0, jnp.concatenate([alpha_lo, alpha_hi], axis=1), all_cols)
    _, alpha_hi = softmax_stage(1, m_hi, hi, True)
    value_stage(n_full + 1, 1, alpha_hi, hi)
    issue_scores(0, 0, all_cols, q=score_columns(qt_next_ref[0]))
    p_scr[1] = jnp.zeros((tk, 2 * tq), BF16)

    lam = _lambda_full(dl_ref)
    out = []
    for half in range(2):
        c0 = slice(2 * half * tk, (2 * half + 1) * tk)
        c1 = slice((2 * half + 1) * tk, (2 * half + 2) * tk)
        out.append(acc[0:DIFF_V_DIM, c0] / acc[DIFF_V_DIM:DIFF_V_DIM + 1, c0]
                   - lam * (acc[0:DIFF_V_DIM, c1] / acc[DIFF_V_DIM:DIFF_V_DIM + 1, c1]))
    ot = jnp.concatenate(out, axis=1)
    o_ref[0] = _subln(ot.T, g_ref[...]).astype(o_ref.dtype)


def _prompt_attention(dl, subln_g, qt16, k16, vt16, tq, tk):
    b, s, _ = k16.shape
    assert tq == 2 * tk, "a query block spans two key blocks (two scratch slots)"
    n_q = s // tq
    kernel = functools.partial(_attn_kernel, tq=tq, tk=tk)
    return pl.pallas_call(
        kernel,
        grid=(b, N_DIFF_HEADS, s // tq),
        in_specs=[
            _const_spec(dl.shape),
            _const_spec(subln_g.shape),
            pl.BlockSpec((1, LANES, tq), lambda bi, h, i: (bi, h, i)),
            pl.BlockSpec((1, LANES, tq), lambda bi, h, i: (bi, h, jnp.minimum(i + 1, n_q - 1))),
            pl.BlockSpec((1, s, LANES), lambda bi, h, i: (bi, 0, h)),
            pl.BlockSpec((1, LANES, s), lambda bi, h, i: (bi, h, 0)),
        ],
        out_specs=pl.BlockSpec((1, tq, LANES), lambda bi, h, i: (bi, i, h)),
        out_shape=jax.ShapeDtypeStruct((b, s, DIFF_VW), BF16),
        scratch_shapes=[pltpu.VMEM((DIFF_V_DIM + ACC_PAD, 2 * tq), F32),
                        pltpu.VMEM((2, tk, 2 * tq), F32),
                        pltpu.VMEM((2, tk, 2 * tq), BF16)],
        compiler_params=_params("arbitrary", "arbitrary", "arbitrary"),
        name="prompt_attn",
    )(dl, subln_g, qt16, qt16, k16, vt16)


def _gated_norm(o, gate):
    return jax.nn.silu(gate) * _rms(o, NORM_EPS)


def _retention_step(rq_ref, rkt_ref, rv_ref, rg_ref, dec_ref, qdec_ref, kdec_ref, gc_ref,
                    r_ref, st_ref, state, j):
    n_chunks = rq_ref.shape[0] // RET_CHUNK
    c = RET_CHUNK
    lane = lax.broadcasted_iota(jnp.int32, (c, LANES), 1)
    n_pair = N_RET_HEADS // 2
    st = [jnp.where(j == 0, 0.0, state[hp]) for hp in range(n_pair)]
    for ci in range(n_chunks):
        rows = slice(ci * c, (ci + 1) * c)
        for hp in range(n_pair):
            cols = slice(hp * LANES, (hp + 1) * LANES)
            qp = rq_ref[rows, cols]
            ktp = rkt_ref[cols, rows]
            ktp16 = ktp.astype(BF16)
            stp = st[hp]
            stp16 = stp.astype(BF16)
            q2 = jnp.concatenate(
                [jnp.where((lane >= e * RET_QK_DIM) & (lane < (e + 1) * RET_QK_DIM),
                           qp, jnp.zeros_like(qp)) for e in range(2)], axis=0)
            sc2 = jnp.dot(q2, ktp16, preferred_element_type=F32)
            cross2 = jnp.dot(q2, stp16, preferred_element_type=F32)
            new = []
            for e in range(2):
                h = hp * 2 + e
                hs = slice(e * RET_QK_DIM, (e + 1) * RET_QK_DIM)
                hv = slice(h * RET_V_DIM, (h + 1) * RET_V_DIM)
                sc = sc2[e * c:(e + 1) * c] * dec_ref[h]
                vh = rv_ref[rows, hv]
                kd = (ktp[hs] * kdec_ref[h]).astype(BF16)
                both = jnp.dot(jnp.concatenate([sc.astype(BF16), kd], axis=0), vh,
                               preferred_element_type=F32)
                cross = cross2[e * c:(e + 1) * c] * qdec_ref[h]
                r_ref[rows, hv] = _gated_norm(both[:c] + cross, rg_ref[rows, hv]).astype(r_ref.dtype)
                new.append(gc_ref[h] * stp[hs] + both[c:])
            st[hp] = jnp.concatenate(new, axis=0)
    for hp in range(n_pair):
        state[hp] = st[hp]
        st_ref[hp] = st[hp]


def _sret_kernel(rq_ref, rkt_ref, rv_ref, rg_ref, st_in_ref, dec_ref, qdec_ref, kdec_ref, gc_ref,
                 r_ref, st_out_ref, *, t_seq):
    n = LANES
    nb = n // t_seq
    lane = lax.broadcasted_iota(jnp.int32, (n, LANES), 1)
    tok_r = lax.broadcasted_iota(jnp.int32, (nb, n, LANES), 1) // t_seq
    seq_r = lax.broadcasted_iota(jnp.int32, (nb, n, LANES), 0)
    row_in_seq = tok_r == seq_r
    tok_l = lax.broadcasted_iota(jnp.int32, (nb, RET_QK_DIM, n), 2) // t_seq
    seq_l = lax.broadcasted_iota(jnp.int32, (nb, RET_QK_DIM, n), 0)
    lane_in_seq = tok_l == seq_l
    for hp in range(N_RET_HEADS // 2):
        cols = slice(hp * LANES, (hp + 1) * LANES)
        qp = rq_ref[0, :, cols]
        ktp = rkt_ref[0, cols, :]
        ktp16 = ktp.astype(BF16)
        stp = st_in_ref[:, hp]
        st16 = stp.astype(BF16).reshape(nb * LANES, RET_V_DIM)
        for e in range(2):
            h = hp * 2 + e
            hs = slice(e * RET_QK_DIM, (e + 1) * RET_QK_DIM)
            hv = slice(h * RET_V_DIM, (h + 1) * RET_V_DIM)
            in_head = (lane >= e * RET_QK_DIM) & (lane < (e + 1) * RET_QK_DIM)
            qh = jnp.where(in_head, qp, jnp.zeros_like(qp))
            sc = jnp.dot(qh, ktp16, preferred_element_type=F32) * dec_ref[h]
            vh = rv_ref[0, :, hv]
            inner = jnp.dot(sc.astype(BF16), vh, preferred_element_type=F32)
            qbd = jnp.where(row_in_seq, qh[None], jnp.zeros_like(qh)[None])
            qbd = jnp.concatenate([qbd[b] for b in range(nb)], axis=1)
            cross = jnp.dot(qbd, st16, preferred_element_type=F32) * qdec_ref[h]
            r_ref[0, :, hv] = _gated_norm(inner + cross, rg_ref[0, :, hv]).astype(r_ref.dtype)
            kd = (ktp[hs] * kdec_ref[h]).astype(BF16)
            kds = jnp.where(lane_in_seq, kd[None], jnp.zeros_like(kd)[None])
            upd = jnp.dot(kds.reshape(nb * RET_QK_DIM, n), vh, preferred_element_type=F32)
            st_out_ref[:, hp, hs, :] = (gc_ref[h] * stp[:, hs, :]
                                        + upd.reshape(nb, RET_QK_DIM, RET_V_DIM))


def _sample_retention(rq, rkt, rv, rg, state, t_seq):
    n_tok = rq.shape[1]
    db = state.shape[0]
    nb = LANES // t_seq
    n_pair = N_RET_HEADS // 2
    st_pairs = state.reshape(db, n_pair, LANES, RET_V_DIM)
    tables = _retention_tables(t_seq, LANES)
    row = lambda width: pl.BlockSpec((1, LANES, width), lambda j: (0, j, 0))
    st_spec = pl.BlockSpec((nb, n_pair, LANES, RET_V_DIM), lambda j: (j, 0, 0, 0))
    r, st = pl.pallas_call(
        functools.partial(_sret_kernel, t_seq=t_seq),
        grid=(n_tok // LANES,),
        in_specs=[row(RET_QW), pl.BlockSpec((1, RET_QW, LANES), lambda j: (0, 0, j)),
                  row(RET_VW), row(RET_VW), st_spec] + [_const_spec(t.shape) for t in tables],
        out_specs=(row(RET_VW), st_spec),
        out_shape=(jax.ShapeDtypeStruct((1, n_tok, RET_VW), BF16),
                   jax.ShapeDtypeStruct(st_pairs.shape, F32)),
        compiler_params=_params("parallel"),
        name="sample_ret",
    )(rq, rkt, rv, rg, st_pairs, *tables)
    return r, st.reshape(state.shape)


def _sattn_compute(dl_ref, g_ref, qb, kn, vn, k_page, v_page, n_pages, page):
    t_seq = kn.shape[0]
    nh = N_DIFF_HEADS
    half = nh * t_seq
    qb = qb.astype(F32)
    s = jnp.concatenate(
        [jnp.dot(qb, k_page(p), preferred_element_type=F32) for p in range(n_pages)],
        axis=1)
    kn = kn.astype(F32)
    t_row = lax.broadcasted_iota(jnp.int32, (2 * half, 1), 0) % t_seq
    s_new = []
    for tk in range(t_seq):
        col = jnp.sum(qb * kn[tk:tk + 1, :], axis=-1, keepdims=True)
        s_new.append(jnp.where(tk <= t_row, col, MASK_VALUE))
    m = jnp.max(s, axis=-1, keepdims=True)
    for col in s_new:
        m = jnp.maximum(m, col)
    p = jnp.exp2(s - m)
    p_new = [jnp.exp2(col - m) for col in s_new]
    l = jnp.sum(p, axis=-1, keepdims=True)
    for col in p_new:
        l = l + col
    lam = _lambda_full(dl_ref)
    a = (p[:half] / l[:half] - lam * (p[half:] / l[half:])).astype(BF16)
    a_new = [(c[:half] / l[:half] - lam * (c[half:] / l[half:])).astype(BF16).astype(F32)
             for c in p_new]
    a_pages = jnp.concatenate([a[:, pg * page:(pg + 1) * page] for pg in range(n_pages)],
                              axis=0)
    spread = (lax.broadcasted_iota(jnp.int32, (page, page * nh), 1) // nh
              == lax.broadcasted_iota(jnp.int32, (page, page * nh), 0)).astype(BF16)
    ax = jnp.dot(a_pages, spread, preferred_element_type=F32)
    col_head = lax.broadcasted_iota(jnp.int32, ax.shape, 1) % nh
    row_head = (lax.broadcasted_iota(jnp.int32, ax.shape, 0) % half) // t_seq
    ax = jnp.where(col_head == row_head, ax, 0.0)
    acc = jnp.zeros((half, DIFF_V_DIM), F32)
    for pg in range(n_pages):
        acc = acc + jnp.dot(ax[pg * half:(pg + 1) * half], v_page(pg),
                            preferred_element_type=F32)
    vn = vn.astype(F32)
    out = []
    for h in range(nh):
        hv = slice(h * DIFF_V_DIM, (h + 1) * DIFF_V_DIM)
        rows = slice(h * t_seq, (h + 1) * t_seq)
        o = acc[rows]
        for tk in range(t_seq):
            o = o + a_new[tk][rows] * vn[tk:tk + 1, hv]
        out.append(_subln(o, g_ref[...]))
    return jnp.concatenate(out, axis=1)


def _ffn_stages(x_ref, d_ref, r_ref, wo_ref, gf_ref, wi_ref, wo2_ref, gl_ref, y_ref, act, fc):
    d_ff = wo2_ref.shape[0]
    dm = x_ref.shape[1]
    live = {}

    def head():
        mix = (jnp.dot(d_ref[...], wo_ref[0:DIFF_VW, :], preferred_element_type=F32)
               + jnp.dot(r_ref[...], wo_ref[DIFF_VW:, :], preferred_element_type=F32))
        live["x1"] = x_ref[...] + mix
        live["hb"] = (_rms(live["x1"], NORM_EPS) * gf_ref[...]).astype(BF16)

    def chunk(c):
        def run():
            hb = live["hb"]
            g = jnp.dot(hb, wi_ref[:, c * fc:(c + 1) * fc], preferred_element_type=F32)
            u = jnp.dot(hb, wi_ref[:, d_ff + c * fc:d_ff + (c + 1) * fc],
                        preferred_element_type=F32)
            act[:, c * fc:(c + 1) * fc] = (jax.nn.silu(g) * u).astype(BF16)
        return run

    def tail():
        x2 = live["x1"] + jnp.dot(act[...], wo2_ref[...], preferred_element_type=F32)
        y_ref[...] = _rms(x2, NORM_EPS) * gl_ref[...]

    return ([(wo_ref.shape[0] * dm, head)]
            + [(2 * dm * fc, chunk(c)) for c in range(d_ff // fc)]
            + [(d_ff * dm, tail)])


def _ffn_kernel(*refs, fc):
    for _, stage in _ffn_stages(*refs, fc):
        stage()


def _merge_ffn(x, d, r, w_out, g_ffn, w_ffn_in, w_ffn_out, g_final, tm, fc):
    n, dm = x.shape
    d_ff = w_ffn_out.shape[0]
    row = lambda width: pl.BlockSpec((tm, width), lambda i: (i, 0))
    return pl.pallas_call(
        functools.partial(_ffn_kernel, fc=fc),
        grid=(n // tm,),
        in_specs=[row(dm), row(DIFF_VW), row(RET_VW), _const_spec(w_out.shape),
                  _const_spec(g_ffn.shape), _const_spec(w_ffn_in.shape),
                  _const_spec(w_ffn_out.shape), _const_spec(g_final.shape)],
        out_specs=row(dm),
        out_shape=jax.ShapeDtypeStruct((n, dm), F32),
        scratch_shapes=[pltpu.VMEM((tm, d_ff), BF16)],
        compiler_params=_params("parallel"),
        name="merge_ffn",
    )(x, d, r, w_out, g_ffn, w_ffn_in, w_ffn_out, g_final)


def _ffn_sattn_kernel(pt_ref, x_ref, d_ref, r_ref, wo_ref, gf_ref, wi_ref, wo2_ref, gl_ref,
                      dl_ref, g_ref, qb_ref, kn_ref, vn_ref, kc_hbm, vc_hbm,
                      y_ref, ds_ref, act, kbuf, vbuf, sem, *, fc, n_seq, n_pages):
    step = pl.program_id(0)
    per_step = qb_ref.shape[0]
    page = kbuf.shape[3]

    def page_copies(seq, slot):
        copies = []
        for p in range(n_pages):
            pg = pt_ref[seq * n_pages + p]
            copies.append(pltpu.make_async_copy(kc_hbm.at[pg], kbuf.at[slot, p], sem.at[slot]))
            copies.append(pltpu.make_async_copy(vc_hbm.at[pg], vbuf.at[slot, p], sem.at[slot]))
        return copies

    def start(seq, slot):
        for c in page_copies(seq, slot):
            c.start()

    def wait(seq, slot):
        for c in page_copies(seq, slot):
            c.wait()

    @pl.when(step == 0)
    def _():
        for t in range(N_PAGE_SLOTS):
            start(t, t)

    stages = _ffn_stages(x_ref, d_ref, r_ref, wo_ref, gf_ref, wi_ref, wo2_ref, gl_ref, y_ref, act, fc)
    total = sum(w for w, _ in stages)
    shares = [[] for _ in range(per_step)]
    done = 0
    for w, stage in stages:
        shares[min(per_step - 1, (2 * done + w) * per_step // (2 * total))].append(stage)
        done += w

    for t in range(per_step):
        seq = step * per_step + t
        slot = t % N_PAGE_SLOTS
        wait(seq, slot)
        o = _sattn_compute(dl_ref, g_ref, qb_ref[t], kn_ref[t], vn_ref[t],
                           lambda p: kbuf[slot, p], lambda p: vbuf[slot, p], n_pages, page)
        ds_ref[t] = o.astype(ds_ref.dtype)
        for stage in shares[t]:
            stage()

        @pl.when(seq + N_PAGE_SLOTS < n_seq)
        def _():
            start(seq + N_PAGE_SLOTS, slot)


def _merge_ffn_sample_attention(x, d, r, w_out, g_ffn, w_ffn_in, w_ffn_out, g_final, tm, fc,
                                page_table, dl, subln_g, qblk, k_new, v_new, kc, vc):
    n, dm = x.shape
    d_ff = w_ffn_out.shape[0]
    db, n_pages = page_table.shape
    t_seq = k_new.shape[1]
    page = kc.shape[2]
    n_steps = n // tm
    assert db % n_steps == 0 and (db // n_steps) % N_PAGE_SLOTS == 0, "sequences per FFN step"
    per_step = db // n_steps
    row = lambda width: pl.BlockSpec((tm, width), lambda i, pt: (i, 0))
    const = lambda a: pl.BlockSpec(a.shape, lambda i, pt: (0,) * a.ndim,
                                   pipeline_mode=pl.Buffered(1))
    per_seq = lambda a: pl.BlockSpec((per_step,) + a.shape[1:], lambda i, pt: (i, 0, 0))
    hbm = pl.BlockSpec(memory_space=pl.ANY)
    grid_spec = pltpu.PrefetchScalarGridSpec(
        num_scalar_prefetch=1,
        grid=(n_steps,),
        in_specs=[row(dm), row(DIFF_VW), row(RET_VW), const(w_out), const(g_ffn),
                  const(w_ffn_in), const(w_ffn_out), const(g_final), const(dl), const(subln_g),
                  per_seq(qblk), per_seq(k_new), per_seq(v_new), hbm, hbm],
        out_specs=(row(dm), pl.BlockSpec((per_step, t_seq, DIFF_VW), lambda i, pt: (i, 0, 0))),
        scratch_shapes=[pltpu.VMEM((tm, d_ff), BF16),
                        pltpu.VMEM((N_PAGE_SLOTS, n_pages, DIFF_W, page), F32),
                        pltpu.VMEM((N_PAGE_SLOTS, n_pages, page * N_DIFF_HEADS, DIFF_V_DIM), F32),
                        pltpu.SemaphoreType.DMA((N_PAGE_SLOTS,))],
    )
    return pl.pallas_call(
        functools.partial(_ffn_sattn_kernel, fc=fc, n_seq=db, n_pages=n_pages),
        grid_spec=grid_spec,
        out_shape=(jax.ShapeDtypeStruct((n, dm), F32),
                   jax.ShapeDtypeStruct((db, t_seq, DIFF_VW), BF16)),
        compiler_params=_params("arbitrary"),
        name="merge_ffn_sample_attn",
    )(page_table.reshape(-1), x, d, r, w_out, g_ffn, w_ffn_in, w_ffn_out, g_final,
      dl, subln_g, qblk, k_new, v_new, kc, vc)


def _ffn_chunk(d_ff):
    for fc in (512, 256, 128):
        if d_ff % fc == 0:
            return fc
    return d_ff


def kernel(x_prompt, x_sample, cache_diff_k, cache_diff_v, state_ret, page_table, norm_mix_g, w_in, diff_lambda, diff_subln_g, w_out, norm_ffn_g, w_ffn_in, w_ffn_out, norm_final_g):
    bsz, seq, dm = x_prompt.shape
    db, t_seq, _ = x_sample.shape
    n_pages = page_table.shape[1]
    page = cache_diff_k.shape[2]
    past = n_pages * page
    assert w_in.shape[0] == 1, "single layer"
    assert LANES % t_seq == 0 and (db * t_seq) % LANES == 0

    w_in0 = w_in[0]
    w_out16 = w_out[0].astype(BF16)
    w_ffn_in16 = w_ffn_in[0].astype(BF16)
    w_ffn_out16 = w_ffn_out[0].astype(BF16)
    g_mix, g_ffn = norm_mix_g, norm_ffn_g
    g_final = norm_final_g.reshape(1, dm)
    dl = diff_lambda[0]
    fc = _ffn_chunk(w_ffn_out.shape[1])

    tm = min(512, seq)
    tabs_p = _rotary_tables(jnp.arange(seq))
    qt16, k16, kt32, v32, vt16, r_p, ret_state_p = _inproj(
        x_prompt, g_mix, w_in0, tabs_p, min(1024, seq), with_retention=True)
    d_p = _prompt_attention(dl, diff_subln_g, qt16, k16, vt16, min(1024, seq), min(512, seq))
    k_prompt = jnp.swapaxes(kt32, 1, 2).reshape(1, bsz, seq, N_DIFF_HEADS, 2, DIFF_QK_DIM)
    v_prompt = v32.reshape(1, bsz, seq, N_DIFF_HEADS, DIFF_V_DIM)
    ret_prompt = ret_state_p.reshape(1, bsz, N_RET_HEADS, RET_QK_DIM, RET_V_DIM)

    n_s = db * t_seq
    pos_s = past + (jnp.arange(n_s) % t_seq)
    tabs_s = _rotary_tables(pos_s)
    tm_s = min(512, n_s)
    qts, ks16, kts32, vs32, _, rqs, rkts, rvs, rgs = _inproj(
        x_sample.reshape(1, n_s, dm), g_mix, w_in0, tabs_s, tm_s, with_retention=False)
    k_s = kts32[0].T
    qs = qts[0].T
    r_idx = np.arange(2 * N_DIFF_HEADS * t_seq)
    r_map, r_head, r_tok = r_idx // (N_DIFF_HEADS * t_seq), (r_idx // t_seq) % N_DIFF_HEADS, r_idx % t_seq
    feat_owner = np.arange(DIFF_W) // DIFF_QK_DIM
    sel = (feat_owner[None, :] == (r_head * 2 + r_map)[:, None])
    qs3 = qs.reshape(db, t_seq, DIFF_W)
    qblk = jnp.where(sel[None], qs3[:, r_tok, :], jnp.zeros((), BF16))
    kc = jnp.transpose(cache_diff_k[0], (0, 2, 3, 4, 1)).reshape(-1, DIFF_W, page)
    y_prompt, d_s = _merge_ffn_sample_attention(
        x_prompt.reshape(bsz * seq, dm), d_p.reshape(bsz * seq, DIFF_VW),
        r_p.reshape(bsz * seq, RET_VW), w_out16, g_ffn, w_ffn_in16, w_ffn_out16, g_final, tm, fc,
        page_table, dl, diff_subln_g, qblk, ks16.reshape(db, t_seq, DIFF_W),
        vs32.astype(BF16).reshape(db, t_seq, DIFF_VW), kc,
        cache_diff_v[0].reshape(-1, page * N_DIFF_HEADS, DIFF_V_DIM))
    y_prompt = y_prompt.reshape(bsz, seq, dm)
    r_s, ret_state_s = _sample_retention(rqs, rkts, rvs, rgs, state_ret[0], t_seq)
    y_sample = _merge_ffn(x_sample.reshape(n_s, dm), d_s.reshape(n_s, DIFF_VW),
                          r_s.reshape(n_s, RET_VW), w_out16, g_ffn, w_ffn_in16,
                          w_ffn_out16, g_final, tm_s, fc).reshape(db, t_seq, dm)
    k_sample = k_s.reshape(1, db, t_seq, N_DIFF_HEADS, 2, DIFF_QK_DIM)
    v_sample = vs32.reshape(1, db, t_seq, N_DIFF_HEADS, DIFF_V_DIM)
    ret_sample = ret_state_s[None]

    return (y_prompt, y_sample, k_prompt, v_prompt, ret_prompt, k_sample, v_sample, ret_sample)
```

```python
import functools
import math

import jax
import jax.numpy as jnp
import numpy as np
from jax import lax
from jax.experimental import pallas as pl
from jax.experimental.pallas import tpu as pltpu

F32 = jnp.float32
BF16 = jnp.bfloat16

N_DIFF_HEADS = 4
DIFF_QK_DIM = 64
DIFF_V_DIM = 128
ROT_DIM = 16
ROPE_THETA = 500000.0
N_RET_HEADS = 4
RET_QK_DIM = 64
RET_V_DIM = 128
RET_THETA = 10000.0
RET_CHUNK = 128
NORM_EPS = 1e-6
SUBLN_EPS = 1e-5
LAM_INIT = 0.8 - 0.6 * math.exp(-0.3 * 0)
MASK_VALUE = -1e30
Q_SCALE = DIFF_QK_DIM ** -0.5 * math.log2(math.e)

DIFF_W = N_DIFF_HEADS * 2 * DIFF_QK_DIM
DIFF_VW = N_DIFF_HEADS * DIFF_V_DIM
RET_QW = N_RET_HEADS * RET_QK_DIM
RET_VW = N_RET_HEADS * RET_V_DIM
W_IN_OFFSETS = tuple(int(o) for o in np.cumsum([0, DIFF_W, DIFF_W, DIFF_VW, RET_QW, RET_QW, RET_VW]))

LANES = 128
ACC_PAD = 16
N_PAGE_SLOTS = 2
VMEM_LIMIT = 56 * 1024 * 1024


def _params(*sem, flags=None):
    return pltpu.CompilerParams(dimension_semantics=sem, vmem_limit_bytes=VMEM_LIMIT, flags=flags)


def _const_spec(shape):
    nd = len(shape)
    return pl.BlockSpec(shape, lambda *_: (0,) * nd, pipeline_mode=pl.Buffered(1))


def _rms(x, eps):
    return x * lax.rsqrt(jnp.mean(x * x, axis=-1, keepdims=True) + eps)


def _inv_freq(freq_idx, dim, theta):
    return 1.0 / (jnp.float32(theta) ** (jnp.asarray(2 * freq_idx, F32) / dim))


def _rotary_tables(pos):
    posf = pos.astype(F32)

    def feature_tables(half, dim, theta, scale):
        ang = _inv_freq(np.arange(half), dim, theta)[:, None] * posf[None, :]
        return jnp.cos(ang) * scale, jnp.sin(ang) * scale

    return (feature_tables(ROT_DIM // 2, ROT_DIM, ROPE_THETA, Q_SCALE)
            + feature_tables(ROT_DIM // 2, ROT_DIM, ROPE_THETA, 1.0)
            + feature_tables(RET_QK_DIM // 2, RET_QK_DIM, RET_THETA, 1.0)
            + feature_tables(RET_QK_DIM // 2, RET_QK_DIM, RET_THETA, RET_QK_DIM ** -0.5))


def _log_gamma():
    return jnp.log(1.0 - 2.0 ** (-5.0 - jnp.arange(N_RET_HEADS, dtype=F32)))


def _retention_tables(chunk, n_tok):
    lg = _log_gamma()
    idx = jnp.arange(n_tok)
    loc = (idx % chunk).astype(F32)
    rel = loc[:, None] - loc[None, :]
    same = (idx[:, None] // chunk) == (idx[None, :] // chunk)
    decay = jnp.where(same[None] & (rel >= 0)[None],
                      jnp.exp(lg[:, None, None] * jnp.maximum(rel, 0.0)[None]), 0.0)
    qdec = jnp.exp(lg[:, None] * (loc[None, :] + 1.0))[:, :, None]
    kdec = jnp.exp(lg[:, None] * (chunk - 1.0 - loc[None, :]))[:, None, :]
    gc = jnp.broadcast_to(jnp.exp(lg * chunk)[:, None, None], (N_RET_HEADS, 1, LANES))
    return decay, qdec, kdec, gc


def _inproj_kernel(x_ref, g_ref, w_ref,
                   cqt_ref, sqt_ref, ckt_ref, skt_ref, crqt_ref, srqt_ref, crkt_ref, srkt_ref,
                   *refs, with_retention):
    if with_retention:
        (dec_ref, qdec_ref, kdec_ref, gc_ref,
         qt_ref, k16_ref, kt32_ref, v32_ref, vt16_ref, r_ref, st_ref,
         rqt_scr, rq_dst, rkt_dst, rv_dst, rg_dst, state) = refs
    else:
        (qt_ref, k16_ref, kt32_ref, v32_ref, vt16_ref, rq_ref, rkt_ref, rv_ref, rg_ref,
         rqt_scr) = refs
        rq_dst, rkt_dst, rv_dst, rg_dst = rq_ref.at[0], rkt_ref.at[0], rv_ref.at[0], rg_ref.at[0]
    x = x_ref[0]
    hb = (_rms(x, NORM_EPS) * g_ref[...]).astype(BF16)

    def mm(lo, width):
        return jnp.dot(hb, w_ref[:, lo:lo + width].astype(BF16), preferred_element_type=F32)

    def rot_t(zt, base, width, half, c, s, rest_scale=None):
        x1, x2 = zt[base:base + half], zt[base + half:base + 2 * half]
        parts = [x1 * c - x2 * s, x2 * c + x1 * s]
        if 2 * half < width:
            rest = zt[base + 2 * half:base + width]
            parts.append(rest if rest_scale is None else rest * rest_scale)
        return jnp.concatenate(parts, axis=0)

    o_dq, o_dk, o_dv, o_rq, o_rk, o_rv, o_rg = W_IN_OFFSETS

    def retention_group():
        zt = mm(o_rq, RET_QW).T
        c, s = crqt_ref[...], srqt_ref[...]
        for h in range(N_RET_HEADS):
            b = h * RET_QK_DIM
            rqt_scr[b:b + RET_QK_DIM, :] = rot_t(zt, b, RET_QK_DIM, RET_QK_DIM // 2, c, s)
        rq_dst[...] = rqt_scr[...].T.astype(BF16)
        zt = mm(o_rk, RET_QW).T
        c, s = crkt_ref[...], srkt_ref[...]
        for h in range(N_RET_HEADS):
            b = h * RET_QK_DIM
            rkt_dst[b:b + RET_QK_DIM, :] = rot_t(zt, b, RET_QK_DIM, RET_QK_DIM // 2, c, s)
        rv_dst[...] = mm(o_rv, RET_VW).astype(BF16)
        rg_dst[...] = mm(o_rg, RET_VW)

    def diff_group():
        hr = ROT_DIM // 2
        zt = mm(o_dq, DIFF_W).T
        c, s = cqt_ref[...], sqt_ref[...]
        for g in range(N_DIFF_HEADS * 2):
            b = g * DIFF_QK_DIM
            qt_ref[0, b:b + DIFF_QK_DIM, :] = rot_t(
                zt, b, DIFF_QK_DIM, hr, c, s, rest_scale=Q_SCALE).astype(BF16)
        zt = mm(o_dk, DIFF_W).T
        c, s = ckt_ref[...], skt_ref[...]
        for g in range(N_DIFF_HEADS * 2):
            b = g * DIFF_QK_DIM
            kt32_ref[0, b:b + DIFF_QK_DIM, :] = rot_t(zt, b, DIFF_QK_DIM, hr, c, s)
        k16_ref[0] = kt32_ref[0].T.astype(BF16)
        zv = mm(o_dv, DIFF_VW)
        for h in range(N_DIFF_HEADS):
            v32_ref[0, pl.ds(h, zv.shape[0], stride=N_DIFF_HEADS), :] = (
                zv[:, h * DIFF_V_DIM:(h + 1) * DIFF_V_DIM])
        vt16_ref[0] = zv.T.astype(BF16)

    retention_group()
    if with_retention:
        _retention_step(rq_dst, rkt_dst, rv_dst, rg_dst, dec_ref, qdec_ref, kdec_ref, gc_ref,
                        r_ref.at[0], st_ref.at[0], state, pl.program_id(1))
    diff_group()


def _inproj(x, g, w_in, tables, tm, with_retention):
    b, s, d = x.shape
    row = lambda width: pl.BlockSpec((1, tm, width), lambda bi, i: (bi, i, 0))
    col = lambda height: pl.BlockSpec((1, height, tm), lambda bi, i: (bi, 0, i))
    ttab = lambda height: pl.BlockSpec((height, tm), lambda bi, i: (0, i))
    n_pair = N_RET_HEADS // 2
    out_shape = [
        jax.ShapeDtypeStruct((b, DIFF_W, s), BF16),
        jax.ShapeDtypeStruct((b, s, DIFF_W), BF16),
        jax.ShapeDtypeStruct((b, DIFF_W, s), F32),
        jax.ShapeDtypeStruct((b, s * N_DIFF_HEADS, DIFF_V_DIM), F32),
        jax.ShapeDtypeStruct((b, DIFF_VW, s), BF16),
    ]
    v_rows = pl.BlockSpec((1, tm * N_DIFF_HEADS, DIFF_V_DIM), lambda bi, i: (bi, i, 0))
    out_specs = [col(DIFF_W), row(DIFF_W), col(DIFF_W), v_rows, col(DIFF_VW)]
    in_specs = ([row(d), _const_spec((1, d)), _const_spec(w_in.shape)]
                + [ttab(ROT_DIM // 2)] * 4 + [ttab(RET_QK_DIM // 2)] * 4)
    scratch = [pltpu.VMEM((RET_QW, tm), F32)]
    operands = [x, g, w_in, *tables]
    ret_inputs = [((tm, RET_QW), BF16), ((RET_QW, tm), F32), ((tm, RET_VW), BF16), ((tm, RET_VW), F32)]
    if with_retention:
        ret_tables = _retention_tables(RET_CHUNK, RET_CHUNK)
        in_specs += [_const_spec(t.shape) for t in ret_tables]
        operands += list(ret_tables)
        out_shape += [jax.ShapeDtypeStruct((b, s, RET_VW), BF16),
                      jax.ShapeDtypeStruct((b, n_pair, LANES, RET_V_DIM), F32)]
        out_specs += [row(RET_VW),
                      pl.BlockSpec((1, n_pair, LANES, RET_V_DIM), lambda bi, i: (bi, 0, 0, 0))]
        scratch += [pltpu.VMEM(shape, dt) for shape, dt in ret_inputs]
        scratch += [pltpu.VMEM((n_pair, LANES, RET_V_DIM), F32)]
    else:
        out_shape += [jax.ShapeDtypeStruct((b, s, RET_QW), BF16),
                      jax.ShapeDtypeStruct((b, RET_QW, s), F32),
                      jax.ShapeDtypeStruct((b, s, RET_VW), BF16),
                      jax.ShapeDtypeStruct((b, s, RET_VW), F32)]
        out_specs += [row(RET_QW), col(RET_QW), row(RET_VW), row(RET_VW)]
    return pl.pallas_call(
        functools.partial(_inproj_kernel, with_retention=with_retention),
        grid=(b, s // tm),
        in_specs=in_specs,
        out_specs=out_specs,
        out_shape=out_shape,
        scratch_shapes=scratch,
        compiler_params=_params("parallel", "arbitrary" if with_retention else "parallel"),
        name="inproj_ret" if with_retention else "inproj",
    )(*operands)


def _lambda_full(dl_ref):
    lp = dl_ref[...]
    a = jnp.sum(lp[0:1] * lp[1:2], axis=-1, keepdims=True)
    b = jnp.sum(lp[2:3] * lp[3:4], axis=-1, keepdims=True)
    return jnp.exp(a) - jnp.exp(b) + LAM_INIT


def _subln(o, g):
    return _rms(o, SUBLN_EPS) * g * (1.0 - LAM_INIT)


def _attn_kernel(dl_ref, g_ref, qt_ref, qt_next_ref, k_ref, vt_ref, o_ref, acc, s_scr, p_scr,
                 *, tq, tk):
    i = pl.program_id(2)
    first_step = (pl.program_id(0) == 0) & (pl.program_id(1) == 0) & (i == 0)

    def score_columns(qt):
        row = lax.broadcasted_iota(jnp.int32, (qt.shape[0], tk), 0)
        groups = []
        for half in range(2):
            qh = qt[:, half * tk:(half + 1) * tk]
            groups += [jnp.where(row < DIFF_QK_DIM, qh, jnp.zeros_like(qh)),
                       jnp.where(row >= DIFF_QK_DIM, qh, jnp.zeros_like(qh))]
        return jnp.concatenate(groups, axis=1)

    qtb = score_columns(qt_ref[0])
    n_full = 2 * i
    all_cols, lo, hi = slice(0, 2 * tq), slice(0, tq), slice(tq, 2 * tq)

    def issue_scores(j, par, cols, q=None):
        off = pl.multiple_of(j * tk, tk)
        q = qtb if q is None else q
        s_scr[par, :, cols] = jnp.dot(k_ref[0, pl.ds(off, tk), :], q[:, cols],
                                      preferred_element_type=F32)

    def softmax_stage(par, m, cols, causal):
        s = s_scr[par, :, cols]
        if causal:
            c = lax.broadcasted_iota(jnp.int32, (tk, tq), 1)
            c = jnp.where(c >= tk, c - tk, c)
            s = jnp.where(lax.broadcasted_iota(jnp.int32, (tk, tq), 0) <= c, s, MASK_VALUE)
        m_new = jnp.maximum(m, jnp.max(s, axis=0, keepdims=True))
        alpha = jnp.exp2(m - m_new)
        p_scr[par, :, cols] = jnp.exp2(s - m_new).astype(BF16)
        return m_new, alpha

    ones_rows = (lax.broadcasted_iota(jnp.int32, (ACC_PAD, tk), 0) == 0).astype(BF16)

    def value_stage(j, par, alpha, cols):
        off = pl.multiple_of(j * tk, tk)
        vt = jnp.concatenate([vt_ref[0, :, pl.ds(off, tk)], ones_rows], axis=0)
        acc[:, cols] = alpha * acc[:, cols] + jnp.dot(vt, p_scr[par, :, cols],
                                                      preferred_element_type=F32)

    def substep(j, par, m, alpha_prev):
        issue_scores(j + 1, 1 - par, all_cols)
        value_stage(jnp.maximum(j - 1, 0), 1 - par, alpha_prev, all_cols)
        return softmax_stage(par, m, all_cols, False)

    def body(t, carry):
        m, alpha = substep(2 * t, 0, *carry)
        return substep(2 * t + 1, 1, m, alpha)

    @pl.when(i == 0)
    def _():
        issue_scores(0, 0, all_cols)

    @pl.when(first_step)
    def _():
        acc[...] = jnp.zeros_like(acc)
        p_scr[1] = jnp.zeros((tk, 2 * tq), BF16)

    carry = (jnp.full((1, 2 * tq), MASK_VALUE, F32), jnp.ones((1, 2 * tq), F32))
    m, alpha = lax.fori_loop(0, i, body, carry)
    issue_scores(n_full + 1, 1, hi)
    value_stage(jnp.maximum(n_full - 1, 0), 1, alpha, all_cols)
    _, alpha_lo = softmax_stage(0, m[:, lo], lo, True)
    m_hi, alpha_hi = softmax_stage(0, m[:, hi], hi, False)
    value_stage(n_full, 0, jnp.concatenate([alpha_lo, alpha_hi], axis=1), all_cols)
    _, alpha_hi = softmax_stage(1, m_hi, hi, True)
    value_stage(n_full + 1, 1, alpha_hi, hi)
    issue_scores(0, 0, all_cols, q=score_columns(qt_next_ref[0]))

    lam = _lambda_full(dl_ref)
    out = []
    for half in range(2):
        c0 = slice(2 * half * tk, (2 * half + 1) * tk)
        c1 = slice((2 * half + 1) * tk, (2 * half + 2) * tk)
        out.append(acc[0:DIFF_V_DIM, c0] / acc[DIFF_V_DIM:DIFF_V_DIM + 1, c0]
                   - lam * (acc[0:DIFF_V_DIM, c1] / acc[DIFF_V_DIM:DIFF_V_DIM + 1, c1]))
    ot = jnp.concatenate(out, axis=1)
    o_ref[0] = _subln(ot.T, g_ref[...]).astype(o_ref.dtype)


def _prompt_attention(dl, subln_g, qt16, k16, vt16, tq, tk):
    b, s, _ = k16.shape
    assert tq == 2 * tk, "a query block spans two key blocks (two scratch slots)"
    n_q = s // tq
    kernel = functools.partial(_attn_kernel, tq=tq, tk=tk)
    return pl.pallas_call(
        kernel,
        grid=(b, N_DIFF_HEADS, s // tq),
        in_specs=[
            _const_spec(dl.shape),
            _const_spec(subln_g.shape),
            pl.BlockSpec((1, LANES, tq), lambda bi, h, i: (bi, h, i)),
            pl.BlockSpec((1, LANES, tq), lambda bi, h, i: (bi, h, jnp.minimum(i + 1, n_q - 1))),
            pl.BlockSpec((1, s, LANES), lambda bi, h, i: (bi, 0, h)),
            pl.BlockSpec((1, LANES, s), lambda bi, h, i: (bi, h, 0)),
        ],
        out_specs=pl.BlockSpec((1, tq, LANES), lambda bi, h, i: (bi, i, h)),
        out_shape=jax.ShapeDtypeStruct((b, s, DIFF_VW), BF16),
        scratch_shapes=[pltpu.VMEM((DIFF_V_DIM + ACC_PAD, 2 * tq), F32),
                        pltpu.VMEM((2, tk, 2 * tq), F32),
                        pltpu.VMEM((2, tk, 2 * tq), BF16)],
        compiler_params=_params("arbitrary", "arbitrary", "arbitrary"),
        name="prompt_attn",
    )(dl, subln_g, qt16, qt16, k16, vt16)


def _gated_norm(o, gate):
    return jax.nn.silu(gate) * _rms(o, NORM_EPS)


def _retention_step(rq_ref, rkt_ref, rv_ref, rg_ref, dec_ref, qdec_ref, kdec_ref, gc_ref,
                    r_ref, st_ref, state, j):
    n_chunks = rq_ref.shape[0] // RET_CHUNK
    c = RET_CHUNK
    lane = lax.broadcasted_iota(jnp.int32, (c, LANES), 1)
    n_pair = N_RET_HEADS // 2
    st = [jnp.where(j == 0, 0.0, state[hp]) for hp in range(n_pair)]
    for ci in range(n_chunks):
        rows = slice(ci * c, (ci + 1) * c)
        for hp in range(n_pair):
            cols = slice(hp * LANES, (hp + 1) * LANES)
            qp = rq_ref[rows, cols]
            ktp = rkt_ref[cols, rows]
            ktp16 = ktp.astype(BF16)
            stp = st[hp]
            stp16 = stp.astype(BF16)
            q2 = jnp.concatenate(
                [jnp.where((lane >= e * RET_QK_DIM) & (lane < (e + 1) * RET_QK_DIM),
                           qp, jnp.zeros_like(qp)) for e in range(2)], axis=0)
            sc2 = jnp.dot(q2, ktp16, preferred_element_type=F32)
            cross2 = jnp.dot(q2, stp16, preferred_element_type=F32)
            new = []
            for e in range(2):
                h = hp * 2 + e
                hs = slice(e * RET_QK_DIM, (e + 1) * RET_QK_DIM)
                hv = slice(h * RET_V_DIM, (h + 1) * RET_V_DIM)
                sc = sc2[e * c:(e + 1) * c] * dec_ref[h]
                vh = rv_ref[rows, hv]
                kd = (ktp[hs] * kdec_ref[h]).astype(BF16)
                both = jnp.dot(jnp.concatenate([sc.astype(BF16), kd], axis=0), vh,
                               preferred_element_type=F32)
                cross = cross2[e * c:(e + 1) * c] * qdec_ref[h]
                r_ref[rows, hv] = _gated_norm(both[:c] + cross, rg_ref[rows, hv]).astype(r_ref.dtype)
                new.append(gc_ref[h] * stp[hs] + both[c:])
            st[hp] = jnp.concatenate(new, axis=0)
    for hp in range(n_pair):
        state[hp] = st[hp]
        st_ref[hp] = st[hp]


def _sret_kernel(rq_ref, rkt_ref, rv_ref, rg_ref, st_in_ref, dec_ref, qdec_ref, kdec_ref, gc_ref,
                 r_ref, st_out_ref, *, t_seq):
    n = LANES
    nb = n // t_seq
    lane = lax.broadcasted_iota(jnp.int32, (n, LANES), 1)
    tok_r = lax.broadcasted_iota(jnp.int32, (nb, n, LANES), 1) // t_seq
    seq_r = lax.broadcasted_iota(jnp.int32, (nb, n, LANES), 0)
    row_in_seq = tok_r == seq_r
    tok_l = lax.broadcasted_iota(jnp.int32, (nb, RET_QK_DIM, n), 2) // t_seq
    seq_l = lax.broadcasted_iota(jnp.int32, (nb, RET_QK_DIM, n), 0)
    lane_in_seq = tok_l == seq_l
    for hp in range(N_RET_HEADS // 2):
        cols = slice(hp * LANES, (hp + 1) * LANES)
        qp = rq_ref[0, :, cols]
        ktp = rkt_ref[0, cols, :]
        ktp16 = ktp.astype(BF16)
        stp = st_in_ref[:, hp]
        st16 = stp.astype(BF16).reshape(nb * LANES, RET_V_DIM)
        for e in range(2):
            h = hp * 2 + e
            hs = slice(e * RET_QK_DIM, (e + 1) * RET_QK_DIM)
            hv = slice(h * RET_V_DIM, (h + 1) * RET_V_DIM)
            in_head = (lane >= e * RET_QK_DIM) & (lane < (e + 1) * RET_QK_DIM)
            qh = jnp.where(in_head, qp, jnp.zeros_like(qp))
            sc = jnp.dot(qh, ktp16, preferred_element_type=F32) * dec_ref[h]
            vh = rv_ref[0, :, hv]
            inner = jnp.dot(sc.astype(BF16), vh, preferred_element_type=F32)
            qbd = jnp.where(row_in_seq, qh[None], jnp.zeros_like(qh)[None])
            qbd = jnp.concatenate([qbd[b] for b in range(nb)], axis=1)
            cross = jnp.dot(qbd, st16, preferred_element_type=F32) * qdec_ref[h]
            r_ref[0, :, hv] = _gated_norm(inner + cross, rg_ref[0, :, hv]).astype(r_ref.dtype)
            kd = (ktp[hs] * kdec_ref[h]).astype(BF16)
            kds = jnp.where(lane_in_seq, kd[None], jnp.zeros_like(kd)[None])
            upd = jnp.dot(kds.reshape(nb * RET_QK_DIM, n), vh, preferred_element_type=F32)
            st_out_ref[:, hp, hs, :] = (gc_ref[h] * stp[:, hs, :]
                                        + upd.reshape(nb, RET_QK_DIM, RET_V_DIM))


def _sample_retention(rq, rkt, rv, rg, state, t_seq):
    n_tok = rq.shape[1]
    db = state.shape[0]
    nb = LANES // t_seq
    n_pair = N_RET_HEADS // 2
    st_pairs = state.reshape(db, n_pair, LANES, RET_V_DIM)
    tables = _retention_tables(t_seq, LANES)
    row = lambda width: pl.BlockSpec((1, LANES, width), lambda j: (0, j, 0))
    st_spec = pl.BlockSpec((nb, n_pair, LANES, RET_V_DIM), lambda j: (j, 0, 0, 0))
    r, st = pl.pallas_call(
        functools.partial(_sret_kernel, t_seq=t_seq),
        grid=(n_tok // LANES,),
        in_specs=[row(RET_QW), pl.BlockSpec((1, RET_QW, LANES), lambda j: (0, 0, j)),
                  row(RET_VW), row(RET_VW), st_spec] + [_const_spec(t.shape) for t in tables],
        out_specs=(row(RET_VW), st_spec),
        out_shape=(jax.ShapeDtypeStruct((1, n_tok, RET_VW), BF16),
                   jax.ShapeDtypeStruct(st_pairs.shape, F32)),
        compiler_params=_params("parallel"),
        name="sample_ret",
    )(rq, rkt, rv, rg, st_pairs, *tables)
    return r, st.reshape(state.shape)


def _sattn_compute(dl_ref, g_ref, qb, kn, vn, k_page, v_page, n_pages, page):
    t_seq = kn.shape[0]
    nh = N_DIFF_HEADS
    half = nh * t_seq
    qb = qb.astype(F32)
    s = jnp.concatenate(
        [jnp.dot(qb, k_page(p), preferred_element_type=F32) for p in range(n_pages)],
        axis=1)
    kn = kn.astype(F32)
    t_row = lax.broadcasted_iota(jnp.int32, (2 * half, 1), 0) % t_seq
    s_new = []
    for tk in range(t_seq):
        col = jnp.sum(qb * kn[tk:tk + 1, :], axis=-1, keepdims=True)
        s_new.append(jnp.where(tk <= t_row, col, MASK_VALUE))
    m = jnp.max(s, axis=-1, keepdims=True)
    for col in s_new:
        m = jnp.maximum(m, col)
    p = jnp.exp2(s - m)
    p_new = [jnp.exp2(col - m) for col in s_new]
    l = jnp.sum(p, axis=-1, keepdims=True)
    for col in p_new:
        l = l + col
    lam = _lambda_full(dl_ref)
    a = (p[:half] / l[:half] - lam * (p[half:] / l[half:])).astype(BF16)
    a_new = [(c[:half] / l[:half] - lam * (c[half:] / l[half:])).astype(BF16).astype(F32)
             for c in p_new]
    a_pages = jnp.concatenate([a[:, pg * page:(pg + 1) * page] for pg in range(n_pages)],
                              axis=0)
    spread = (lax.broadcasted_iota(jnp.int32, (page, page * nh), 1) // nh
              == lax.broadcasted_iota(jnp.int32, (page, page * nh), 0)).astype(BF16)
    ax = jnp.dot(a_pages, spread, preferred_element_type=F32)
    col_head = lax.broadcasted_iota(jnp.int32, ax.shape, 1) % nh
    row_head = (lax.broadcasted_iota(jnp.int32, ax.shape, 0) % half) // t_seq
    ax = jnp.where(col_head == row_head, ax, 0.0)
    acc = jnp.zeros((half, DIFF_V_DIM), F32)
    for pg in range(n_pages):
        acc = acc + jnp.dot(ax[pg * half:(pg + 1) * half], v_page(pg),
                            preferred_element_type=F32)
    vn = vn.astype(F32)
    out = []
    for h in range(nh):
        hv = slice(h * DIFF_V_DIM, (h + 1) * DIFF_V_DIM)
        rows = slice(h * t_seq, (h + 1) * t_seq)
        o = acc[rows]
        for tk in range(t_seq):
            o = o + a_new[tk][rows] * vn[tk:tk + 1, hv]
        out.append(_subln(o, g_ref[...]))
    return jnp.concatenate(out, axis=1)


def _ffn_stages(x_ref, d_ref, r_ref, wo_ref, gf_ref, wi_ref, wo2_ref, gl_ref, y_ref, act, fc):
    d_ff = wo2_ref.shape[0]
    dm = x_ref.shape[1]
    live = {}

    def head():
        mix = (jnp.dot(d_ref[...], wo_ref[0:DIFF_VW, :], preferred_element_type=F32)
               + jnp.dot(r_ref[...], wo_ref[DIFF_VW:, :], preferred_element_type=F32))
        live["x1"] = x_ref[...] + mix
        live["hb"] = (_rms(live["x1"], NORM_EPS) * gf_ref[...]).astype(BF16)

    def chunk(c):
        def run():
            hb = live["hb"]
            g = jnp.dot(hb, wi_ref[:, c * fc:(c + 1) * fc], preferred_element_type=F32)
            u = jnp.dot(hb, wi_ref[:, d_ff + c * fc:d_ff + (c + 1) * fc],
                        preferred_element_type=F32)
            act[:, c * fc:(c + 1) * fc] = (jax.nn.silu(g) * u).astype(BF16)
        return run

    def tail():
        x2 = live["x1"] + jnp.dot(act[...], wo2_ref[...], preferred_element_type=F32)
        y_ref[...] = _rms(x2, NORM_EPS) * gl_ref[...]

    return ([(wo_ref.shape[0] * dm, head)]
            + [(2 * dm * fc, chunk(c)) for c in range(d_ff // fc)]
            + [(d_ff * dm, tail)])


def _ffn_kernel(*refs, fc):
    for _, stage in _ffn_stages(*refs, fc):
        stage()


def _merge_ffn(x, d, r, w_out, g_ffn, w_ffn_in, w_ffn_out, g_final, tm, fc):
    n, dm = x.shape
    d_ff = w_ffn_out.shape[0]
    row = lambda width: pl.BlockSpec((tm, width), lambda i: (i, 0))
    return pl.pallas_call(
        functools.partial(_ffn_kernel, fc=fc),
        grid=(n // tm,),
        in_specs=[row(dm), row(DIFF_VW), row(RET_VW), _const_spec(w_out.shape),
                  _const_spec(g_ffn.shape), _const_spec(w_ffn_in.shape),
                  _const_spec(w_ffn_out.shape), _const_spec(g_final.shape)],
        out_specs=row(dm),
        out_shape=jax.ShapeDtypeStruct((n, dm), F32),
        scratch_shapes=[pltpu.VMEM((tm, d_ff), BF16)],
        compiler_params=_params("parallel"),
        name="merge_ffn",
    )(x, d, r, w_out, g_ffn, w_ffn_in, w_ffn_out, g_final)


def _ffn_sattn_kernel(pt_ref, x_ref, d_ref, r_ref, wo_ref, gf_ref, wi_ref, wo2_ref, gl_ref,
                      dl_ref, g_ref, qb_ref, kn_ref, vn_ref, kc_hbm, vc_hbm,
                      y_ref, ds_ref, act, kbuf, vbuf, sem, *, fc, n_seq, n_pages):
    step = pl.program_id(0)
    per_step = qb_ref.shape[0]
    page = kbuf.shape[3]

    def page_copies(seq, slot):
        copies = []
        for p in range(n_pages):
            pg = pt_ref[seq * n_pages + p]
            copies.append(pltpu.make_async_copy(kc_hbm.at[pg], kbuf.at[slot, p], sem.at[slot]))
            copies.append(pltpu.make_async_copy(vc_hbm.at[pg], vbuf.at[slot, p], sem.at[slot]))
        return copies

    def start(seq, slot):
        for c in page_copies(seq, slot):
            c.start()

    def wait(seq, slot):
        for c in page_copies(seq, slot):
            c.wait()

    @pl.when(step == 0)
    def _():
        for t in range(N_PAGE_SLOTS):
            start(t, t)

    stages = _ffn_stages(x_ref, d_ref, r_ref, wo_ref, gf_ref, wi_ref, wo2_ref, gl_ref, y_ref, act, fc)
    total = sum(w for w, _ in stages)
    shares = [[] for _ in range(per_step)]
    done = 0
    for w, stage in stages:
        shares[min(per_step - 1, (2 * done + w) * per_step // (2 * total))].append(stage)
        done += w

    for t in range(per_step):
        seq = step * per_step + t
        slot = t % N_PAGE_SLOTS
        wait(seq, slot)
        o = _sattn_compute(dl_ref, g_ref, qb_ref[t], kn_ref[t], vn_ref[t],
                           lambda p: kbuf[slot, p], lambda p: vbuf[slot, p], n_pages, page)
        ds_ref[t] = o.astype(ds_ref.dtype)
        for stage in shares[t]:
            stage()

        @pl.when(seq + N_PAGE_SLOTS < n_seq)
        def _():
            start(seq + N_PAGE_SLOTS, slot)


def _merge_ffn_sample_attention(x, d, r, w_out, g_ffn, w_ffn_in, w_ffn_out, g_final, tm, fc,
                                page_table, dl, subln_g, qblk, k_new, v_new, kc, vc):
    n, dm = x.shape
    d_ff = w_ffn_out.shape[0]
    db, n_pages = page_table.shape
    t_seq = k_new.shape[1]
    page = kc.shape[2]
    n_steps = n // tm
    assert db % n_steps == 0 and (db // n_steps) % N_PAGE_SLOTS == 0, "sequences per FFN step"
    per_step = db // n_steps
    row = lambda width: pl.BlockSpec((tm, width), lambda i, pt: (i, 0))
    const = lambda a: pl.BlockSpec(a.shape, lambda i, pt: (0,) * a.ndim,
                                   pipeline_mode=pl.Buffered(1))
    per_seq = lambda a: pl.BlockSpec((per_step,) + a.shape[1:], lambda i, pt: (i, 0, 0))
    hbm = pl.BlockSpec(memory_space=pl.ANY)
    grid_spec = pltpu.PrefetchScalarGridSpec(
        num_scalar_prefetch=1,
        grid=(n_steps,),
        in_specs=[row(dm), row(DIFF_VW), row(RET_VW), const(w_out), const(g_ffn),
                  const(w_ffn_in), const(w_ffn_out), const(g_final), const(dl), const(subln_g),
                  per_seq(qblk), per_seq(k_new), per_seq(v_new), hbm, hbm],
        out_specs=(row(dm), pl.BlockSpec((per_step, t_seq, DIFF_VW), lambda i, pt: (i, 0, 0))),
        scratch_shapes=[pltpu.VMEM((tm, d_ff), BF16),
                        pltpu.VMEM((N_PAGE_SLOTS, n_pages, DIFF_W, page), F32),
                        pltpu.VMEM((N_PAGE_SLOTS, n_pages, page * N_DIFF_HEADS, DIFF_V_DIM), F32),
                        pltpu.SemaphoreType.DMA((N_PAGE_SLOTS,))],
    )
    return pl.pallas_call(
        functools.partial(_ffn_sattn_kernel, fc=fc, n_seq=db, n_pages=n_pages),
        grid_spec=grid_spec,
        out_shape=(jax.ShapeDtypeStruct((n, dm), F32),
                   jax.ShapeDtypeStruct((db, t_seq, DIFF_VW), BF16)),
        compiler_params=_params("arbitrary"),
        name="merge_ffn_sample_attn",
    )(page_table.reshape(-1), x, d, r, w_out, g_ffn, w_ffn_in, w_ffn_out, g_final,
      dl, subln_g, qblk, k_new, v_new, kc, vc)


def _ffn_chunk(d_ff):
    for fc in (512, 256, 128):
        if d_ff % fc == 0:
            return fc
    return d_ff


def kernel(x_prompt, x_sample, cache_diff_k, cache_diff_v, state_ret, page_table, norm_mix_g, w_in, diff_lambda, diff_subln_g, w_out, norm_ffn_g, w_ffn_in, w_ffn_out, norm_final_g):
    bsz, seq, dm = x_prompt.shape
    db, t_seq, _ = x_sample.shape
    n_pages = page_table.shape[1]
    page = cache_diff_k.shape[2]
    past = n_pages * page
    assert w_in.shape[0] == 1, "single layer"
    assert LANES % t_seq == 0 and (db * t_seq) % LANES == 0

    w_in0 = w_in[0]
    w_out16 = w_out[0].astype(BF16)
    w_ffn_in16 = w_ffn_in[0].astype(BF16)
    w_ffn_out16 = w_ffn_out[0].astype(BF16)
    g_mix, g_ffn = norm_mix_g, norm_ffn_g
    g_final = norm_final_g.reshape(1, dm)
    dl = diff_lambda[0]
    fc = _ffn_chunk(w_ffn_out.shape[1])

    tm = min(512, seq)
    tabs_p = _rotary_tables(jnp.arange(seq))
    qt16, k16, kt32, v32, vt16, r_p, ret_state_p = _inproj(
        x_prompt, g_mix, w_in0, tabs_p, min(1024, seq), with_retention=True)
    d_p = _prompt_attention(dl, diff_subln_g, qt16, k16, vt16, min(1024, seq), min(512, seq))
    k_prompt = jnp.swapaxes(kt32, 1, 2).reshape(1, bsz, seq, N_DIFF_HEADS, 2, DIFF_QK_DIM)
    v_prompt = v32.reshape(1, bsz, seq, N_DIFF_HEADS, DIFF_V_DIM)
    ret_prompt = ret_state_p.reshape(1, bsz, N_RET_HEADS, RET_QK_DIM, RET_V_DIM)

    n_s = db * t_seq
    pos_s = past + (jnp.arange(n_s) % t_seq)
    tabs_s = _rotary_tables(pos_s)
    tm_s = min(512, n_s)
    qts, ks16, kts32, vs32, _, rqs, rkts, rvs, rgs = _inproj(
        x_sample.reshape(1, n_s, dm), g_mix, w_in0, tabs_s, tm_s, with_retention=False)
    k_s = kts32[0].T
    qs = qts[0].T
    r_idx = np.arange(2 * N_DIFF_HEADS * t_seq)
    r_map, r_head, r_tok = r_idx // (N_DIFF_HEADS * t_seq), (r_idx // t_seq) % N_DIFF_HEADS, r_idx % t_seq
    feat_owner = np.arange(DIFF_W) // DIFF_QK_DIM
    sel = (feat_owner[None, :] == (r_head * 2 + r_map)[:, None])
    qs3 = qs.reshape(db, t_seq, DIFF_W)
    qblk = jnp.where(sel[None], qs3[:, r_tok, :], jnp.zeros((), BF16))
    kc = jnp.transpose(cache_diff_k[0], (0, 2, 3, 4, 1)).reshape(-1, DIFF_W, page)
    y_prompt, d_s = _merge_ffn_sample_attention(
        x_prompt.reshape(bsz * seq, dm), d_p.reshape(bsz * seq, DIFF_VW),
        r_p.reshape(bsz * seq, RET_VW), w_out16, g_ffn, w_ffn_in16, w_ffn_out16, g_final, tm, fc,
        page_table, dl, diff_subln_g, qblk, ks16.reshape(db, t_seq, DIFF_W),
        vs32.astype(BF16).reshape(db, t_seq, DIFF_VW), kc,
        cache_diff_v[0].reshape(-1, page * N_DIFF_HEADS, DIFF_V_DIM))
    y_prompt = y_prompt.reshape(bsz, seq, dm)
    r_s, ret_state_s = _sample_retention(rqs, rkts, rvs, rgs, state_ret[0], t_seq)
    y_sample = _merge_ffn(x_sample.reshape(n_s, dm), d_s.reshape(n_s, DIFF_VW),
                          r_s.reshape(n_s, RET_VW), w_out16, g_ffn, w_ffn_in16,
                          w_ffn_out16, g_final, tm_s, fc).reshape(db, t_seq, dm)
    k_sample = k_s.reshape(1, db, t_seq, N_DIFF_HEADS, 2, DIFF_QK_DIM)
    v_sample = vs32.reshape(1, db, t_seq, N_DIFF_HEADS, DIFF_V_DIM)
    ret_sample = ret_state_s[None]

    return (y_prompt, y_sample, k_prompt, v_prompt, ret_prompt, k_sample, v_sample, ret_sample)
```

```python
import functools
import math

import jax
import jax.numpy as jnp
import numpy as np
from jax import lax
from jax.experimental import pallas as pl
from jax.experimental.pallas import tpu as pltpu

F32 = jnp.float32
BF16 = jnp.bfloat16

N_DIFF_HEADS = 4
DIFF_QK_DIM = 64
DIFF_V_DIM = 128
ROT_DIM = 16
ROPE_THETA = 500000.0
N_RET_HEADS = 4
RET_QK_DIM = 64
RET_V_DIM = 128
RET_THETA = 10000.0
RET_CHUNK = 128
NORM_EPS = 1e-6
SUBLN_EPS = 1e-5
LAM_INIT = 0.8 - 0.6 * math.exp(-0.3 * 0)
MASK_VALUE = -1e30
Q_SCALE = DIFF_QK_DIM ** -0.5 * math.log2(math.e)

DIFF_W = N_DIFF_HEADS * 2 * DIFF_QK_DIM
DIFF_VW = N_DIFF_HEADS * DIFF_V_DIM
RET_QW = N_RET_HEADS * RET_QK_DIM
RET_VW = N_RET_HEADS * RET_V_DIM
W_IN_OFFSETS = tuple(int(o) for o in np.cumsum([0, DIFF_W, DIFF_W, DIFF_VW, RET_QW, RET_QW, RET_VW]))

LANES = 128
ACC_PAD = 16
N_PAGE_SLOTS = 2
VMEM_LIMIT = 56 * 1024 * 1024


def _params(*sem, flags=None):
    return pltpu.CompilerParams(dimension_semantics=sem, vmem_limit_bytes=VMEM_LIMIT, flags=flags)


def _const_spec(shape):
    nd = len(shape)
    return pl.BlockSpec(shape, lambda *_: (0,) * nd, pipeline_mode=pl.Buffered(1))


def _rms(x, eps):
    return x * lax.rsqrt(jnp.mean(x * x, axis=-1, keepdims=True) + eps)


def _inv_freq(freq_idx, dim, theta):
    return 1.0 / (jnp.float32(theta) ** (jnp.asarray(2 * freq_idx, F32) / dim))


def _rotary_tables(pos):
    posf = pos.astype(F32)

    def feature_tables(half, dim, theta, scale):
        ang = _inv_freq(np.arange(half), dim, theta)[:, None] * posf[None, :]
        return jnp.cos(ang) * scale, jnp.sin(ang) * scale

    return (feature_tables(ROT_DIM // 2, ROT_DIM, ROPE_THETA, Q_SCALE)
            + feature_tables(ROT_DIM // 2, ROT_DIM, ROPE_THETA, 1.0)
            + feature_tables(RET_QK_DIM // 2, RET_QK_DIM, RET_THETA, 1.0))


def _log_gamma():
    return jnp.log(1.0 - 2.0 ** (-5.0 - jnp.arange(N_RET_HEADS, dtype=F32)))


def _retention_tables(chunk, n_tok):
    lg = _log_gamma()
    idx = jnp.arange(n_tok)
    loc = (idx % chunk).astype(F32)
    rel = loc[:, None] - loc[None, :]
    same = (idx[:, None] // chunk) == (idx[None, :] // chunk)
    decay = jnp.where(same[None] & (rel >= 0)[None],
                      jnp.exp(lg[:, None, None] * jnp.maximum(rel, 0.0)[None]), 0.0)
    qdec = jnp.exp(lg[:, None] * (loc[None, :] + 1.0))[:, :, None]
    kdec = jnp.exp(lg[:, None] * (chunk - 1.0 - loc[None, :]))[:, None, :]
    gc = jnp.broadcast_to(jnp.exp(lg * chunk)[:, None, None], (N_RET_HEADS, 1, LANES))
    return decay, qdec, kdec, gc


def _inproj_kernel(x_ref, g_ref, w_ref,
                   cqt_ref, sqt_ref, ckt_ref, skt_ref, crt_ref, srt_ref,
                   *refs, with_retention):
    if with_retention:
        (dec_ref, qdec_ref, kdec_ref, gc_ref,
         qt_ref, k16_ref, kt32_ref, v32_ref, vt16_ref, r_ref, st_ref,
         rqt_scr, rq_dst, rkt_dst, rv_dst, rg_dst, state) = refs
    else:
        (qt_ref, k16_ref, kt32_ref, v32_ref, vt16_ref, rq_ref, rkt_ref, rv_ref, rg_ref,
         q16_tok_ref, k32_tok_ref, rqt_scr) = refs
        rq_dst, rkt_dst, rv_dst, rg_dst = rq_ref.at[0], rkt_ref.at[0], rv_ref.at[0], rg_ref.at[0]
    x = x_ref[0]
    hb = (_rms(x, NORM_EPS) * g_ref[...]).astype(BF16)

    def mm(lo, width):
        return jnp.dot(hb, w_ref[:, lo:lo + width].astype(BF16), preferred_element_type=F32)

    def rot_t(zt, base, width, half, c, s, rest_scale=None):
        x1, x2 = zt[base:base + half], zt[base + half:base + 2 * half]
        parts = [x1 * c - x2 * s, x2 * c + x1 * s]
        if 2 * half < width:
            rest = zt[base + 2 * half:base + width]
            parts.append(rest if rest_scale is None else rest * rest_scale)
        return jnp.concatenate(parts, axis=0)

    o_dq, o_dk, o_dv, o_rq, o_rk, o_rv, o_rg = W_IN_OFFSETS

    def retention_group():
        zt = mm(o_rq, RET_QW).T
        c, s = crt_ref[...], srt_ref[...]
        for h in range(N_RET_HEADS):
            b = h * RET_QK_DIM
            rqt_scr[b:b + RET_QK_DIM, :] = rot_t(zt, b, RET_QK_DIM, RET_QK_DIM // 2, c, s)
        rq_dst[...] = rqt_scr[...].T.astype(BF16)
        zt = mm(o_rk, RET_QW).T
        for h in range(N_RET_HEADS):
            b = h * RET_QK_DIM
            rkt_dst[b:b + RET_QK_DIM, :] = (
                rot_t(zt, b, RET_QK_DIM, RET_QK_DIM // 2, c, s) * RET_QK_DIM ** -0.5)
        rv_dst[...] = mm(o_rv, RET_VW).astype(BF16)
        rg_dst[...] = mm(o_rg, RET_VW)

    def diff_group():
        hr = ROT_DIM // 2
        zt = mm(o_dq, DIFF_W).T
        c, s = cqt_ref[...], sqt_ref[...]
        q_groups = []
        for g in range(N_DIFF_HEADS * 2):
            b = g * DIFF_QK_DIM
            q_groups.append(rot_t(zt, b, DIFF_QK_DIM, hr, c, s, rest_scale=Q_SCALE))
            qt_ref[0, b:b + DIFF_QK_DIM, :] = q_groups[-1].astype(BF16)
        zt = mm(o_dk, DIFF_W).T
        c, s = ckt_ref[...], skt_ref[...]
        for g in range(N_DIFF_HEADS * 2):
            b = g * DIFF_QK_DIM
            kt32_ref[0, b:b + DIFF_QK_DIM, :] = rot_t(zt, b, DIFF_QK_DIM, hr, c, s)
        k_tok = kt32_ref[0].T
        k16_ref[0] = k_tok.astype(BF16)
        if not with_retention:
            k32_tok_ref[0] = k_tok
            q16_tok_ref[0] = jnp.concatenate(q_groups, axis=0).T.astype(BF16)
        zv = mm(o_dv, DIFF_VW)
        for h in range(N_DIFF_HEADS):
            v32_ref[0, pl.ds(h, zv.shape[0], stride=N_DIFF_HEADS), :] = (
                zv[:, h * DIFF_V_DIM:(h + 1) * DIFF_V_DIM])
        vt16_ref[0] = zv.T.astype(BF16)

    retention_group()
    if with_retention:
        _retention_step(rq_dst, rkt_dst, rv_dst, rg_dst, dec_ref, qdec_ref, kdec_ref, gc_ref,
                        r_ref.at[0], st_ref.at[0], state, pl.program_id(1))
    diff_group()


def _inproj(x, g, w_in, tables, tm, with_retention):
    b, s, d = x.shape
    row = lambda width: pl.BlockSpec((1, tm, width), lambda bi, i: (bi, i, 0))
    col = lambda height: pl.BlockSpec((1, height, tm), lambda bi, i: (bi, 0, i))
    ttab = lambda height: pl.BlockSpec((height, tm), lambda bi, i: (0, i))
    n_pair = N_RET_HEADS // 2
    out_shape = [
        jax.ShapeDtypeStruct((b, DIFF_W, s), BF16),
        jax.ShapeDtypeStruct((b, s, DIFF_W), BF16),
        jax.ShapeDtypeStruct((b, DIFF_W, s), F32),
        jax.ShapeDtypeStruct((b, s * N_DIFF_HEADS, DIFF_V_DIM), F32),
        jax.ShapeDtypeStruct((b, DIFF_VW, s), BF16),
    ]
    v_rows = pl.BlockSpec((1, tm * N_DIFF_HEADS, DIFF_V_DIM), lambda bi, i: (bi, i, 0))
    out_specs = [col(DIFF_W), row(DIFF_W), col(DIFF_W), v_rows, col(DIFF_VW)]
    in_specs = ([row(d), _const_spec((1, d)), _const_spec(w_in.shape)]
                + [ttab(ROT_DIM // 2)] * 4 + [ttab(RET_QK_DIM // 2)] * 2)
    scratch = [pltpu.VMEM((RET_QW, tm), F32)]
    operands = [x, g, w_in, *tables]
    ret_inputs = [((tm, RET_QW), BF16), ((RET_QW, tm), F32), ((tm, RET_VW), BF16), ((tm, RET_VW), F32)]
    if with_retention:
        ret_tables = _retention_tables(RET_CHUNK, RET_CHUNK)
        in_specs += [_const_spec(t.shape) for t in ret_tables]
        operands += list(ret_tables)
        out_shape += [jax.ShapeDtypeStruct((b, s, RET_VW), BF16),
                      jax.ShapeDtypeStruct((b, n_pair, LANES, RET_V_DIM), F32)]
        out_specs += [row(RET_VW),
                      pl.BlockSpec((1, n_pair, LANES, RET_V_DIM), lambda bi, i: (bi, 0, 0, 0))]
        scratch += [pltpu.VMEM(shape, dt) for shape, dt in ret_inputs]
        scratch += [pltpu.VMEM((n_pair, LANES, RET_V_DIM), F32)]
    else:
        out_shape += [jax.ShapeDtypeStruct((b, s, RET_QW), BF16),
                      jax.ShapeDtypeStruct((b, RET_QW, s), F32),
                      jax.ShapeDtypeStruct((b, s, RET_VW), BF16),
                      jax.ShapeDtypeStruct((b, s, RET_VW), F32),
                      jax.ShapeDtypeStruct((b, s, DIFF_W), BF16),
                      jax.ShapeDtypeStruct((b, s, DIFF_W), F32)]
        out_specs += [row(RET_QW), col(RET_QW), row(RET_VW), row(RET_VW), row(DIFF_W), row(DIFF_W)]
    return pl.pallas_call(
        functools.partial(_inproj_kernel, with_retention=with_retention),
        grid=(b, s // tm),
        in_specs=in_specs,
        out_specs=out_specs,
        out_shape=out_shape,
        scratch_shapes=scratch,
        compiler_params=_params("parallel", "arbitrary" if with_retention else "parallel"),
        name="inproj_ret" if with_retention else "inproj",
    )(*operands)


def _lambda_full(dl_ref):
    lp = dl_ref[...]
    a = jnp.sum(lp[0:1] * lp[1:2], axis=-1, keepdims=True)
    b = jnp.sum(lp[2:3] * lp[3:4], axis=-1, keepdims=True)
    return jnp.exp(a) - jnp.exp(b) + LAM_INIT


def _subln(o, g):
    return _rms(o, SUBLN_EPS) * g * (1.0 - LAM_INIT)


def _attn_kernel(dl_ref, g_ref, qt_ref, qt_next_ref, k_ref, vt_ref, o_ref, acc, s_scr, p_scr,
                 *, tq, tk):
    i = pl.program_id(2)
    first_step = (pl.program_id(0) == 0) & (pl.program_id(1) == 0) & (i == 0)

    def score_columns(qt):
        row = lax.broadcasted_iota(jnp.int32, (qt.shape[0], tk), 0)
        groups = []
        for half in range(2):
            qh = qt[:, half * tk:(half + 1) * tk]
            groups += [jnp.where(row < DIFF_QK_DIM, qh, jnp.zeros_like(qh)),
                       jnp.where(row >= DIFF_QK_DIM, qh, jnp.zeros_like(qh))]
        return jnp.concatenate(groups, axis=1)

    qtb = score_columns(qt_ref[0])
    n_full = 2 * i
    all_cols, lo, hi = slice(0, 2 * tq), slice(0, tq), slice(tq, 2 * tq)

    def issue_scores(j, par, cols, q=None):
        off = pl.multiple_of(j * tk, tk)
        q = qtb if q is None else q
        s_scr[par, :, cols] = jnp.dot(k_ref[0, pl.ds(off, tk), :], q[:, cols],
                                      preferred_element_type=F32)

    def softmax_stage(par, m, cols, causal):
        s = s_scr[par, :, cols]
        if causal:
            c = lax.broadcasted_iota(jnp.int32, (tk, tq), 1)
            c = jnp.where(c >= tk, c - tk, c)
            s = jnp.where(lax.broadcasted_iota(jnp.int32, (tk, tq), 0) <= c, s, MASK_VALUE)
        m_new = jnp.maximum(m, jnp.max(s, axis=0, keepdims=True))
        alpha = jnp.exp2(m - m_new)
        p_scr[par, :, cols] = jnp.exp2(s - m_new).astype(BF16)
        return m_new, alpha

    ones_rows = (lax.broadcasted_iota(jnp.int32, (ACC_PAD, tk), 0) == 0).astype(BF16)

    def value_stage(j, par, alpha, cols):
        off = pl.multiple_of(j * tk, tk)
        vt = jnp.concatenate([vt_ref[0, :, pl.ds(off, tk)], ones_rows], axis=0)
        acc[:, cols] = alpha * acc[:, cols] + jnp.dot(vt, p_scr[par, :, cols],
                                                      preferred_element_type=F32)

    def substep(j, par, m, alpha_prev):
        issue_scores(j + 1, 1 - par, all_cols)
        value_stage(jnp.maximum(j - 1, 0), 1 - par, alpha_prev, all_cols)
        return softmax_stage(par, m, all_cols, False)

    def body(t, carry):
        m, alpha = substep(2 * t, 0, *carry)
        return substep(2 * t + 1, 1, m, alpha)

    @pl.when(i == 0)
    def _():
        issue_scores(0, 0, all_cols)

    @pl.when(first_step)
    def _():
        acc[...] = jnp.zeros_like(acc)
        p_scr[1] = jnp.zeros((tk, 2 * tq), BF16)

    carry = (jnp.full((1, 2 * tq), MASK_VALUE, F32), jnp.ones((1, 2 * tq), F32))
    m, alpha = lax.fori_loop(0, i, body, carry)
    issue_scores(n_full + 1, 1, hi)
    value_stage(jnp.maximum(n_full - 1, 0), 1, alpha, all_cols)
    _, alpha_lo = softmax_stage(0, m[:, lo], lo, True)
    m_hi, alpha_hi = softmax_stage(0, m[:, hi], hi, False)
    value_stage(n_full, 0, jnp.concatenate([alpha_lo, alpha_hi], axis=1), all_cols)
    _, alpha_hi = softmax_stage(1, m_hi, hi, True)
    value_stage(n_full + 1, 1, alpha_hi, hi)
    issue_scores(0, 0, all_cols, q=score_columns(qt_next_ref[0]))

    lam = _lambda_full(dl_ref)
    out = []
    for half in range(2):
        c0 = slice(2 * half * tk, (2 * half + 1) * tk)
        c1 = slice((2 * half + 1) * tk, (2 * half + 2) * tk)
        out.append(acc[0:DIFF_V_DIM, c0] / acc[DIFF_V_DIM:DIFF_V_DIM + 1, c0]
                   - lam * (acc[0:DIFF_V_DIM, c1] / acc[DIFF_V_DIM:DIFF_V_DIM + 1, c1]))
    ot = jnp.concatenate(out, axis=1)
    o_ref[0] = _subln(ot.T, g_ref[...]).astype(o_ref.dtype)


def _prompt_attention(dl, subln_g, qt16, k16, vt16, tq, tk):
    b, s, _ = k16.shape
    assert tq == 2 * tk, "a query block spans two key blocks (two scratch slots)"
    n_q = s // tq
    kernel = functools.partial(_attn_kernel, tq=tq, tk=tk)
    return pl.pallas_call(
        kernel,
        grid=(b, N_DIFF_HEADS, s // tq),
        in_specs=[
            _const_spec(dl.shape),
            _const_spec(subln_g.shape),
            pl.BlockSpec((1, LANES, tq), lambda bi, h, i: (bi, h, i)),
            pl.BlockSpec((1, LANES, tq), lambda bi, h, i: (bi, h, jnp.minimum(i + 1, n_q - 1))),
            pl.BlockSpec((1, s, LANES), lambda bi, h, i: (bi, 0, h)),
            pl.BlockSpec((1, LANES, s), lambda bi, h, i: (bi, h, 0)),
        ],
        out_specs=pl.BlockSpec((1, tq, LANES), lambda bi, h, i: (bi, i, h)),
        out_shape=jax.ShapeDtypeStruct((b, s, DIFF_VW), BF16),
        scratch_shapes=[pltpu.VMEM((DIFF_V_DIM + ACC_PAD, 2 * tq), F32),
                        pltpu.VMEM((2, tk, 2 * tq), F32),
                        pltpu.VMEM((2, tk, 2 * tq), BF16)],
        compiler_params=_params("arbitrary", "arbitrary", "arbitrary"),
        name="prompt_attn",
    )(dl, subln_g, qt16, qt16, k16, vt16)


def _gated_norm(o, gate):
    return jax.nn.silu(gate) * _rms(o, NORM_EPS)


def _retention_step(rq_ref, rkt_ref, rv_ref, rg_ref, dec_ref, qdec_ref, kdec_ref, gc_ref,
                    r_ref, st_ref, state, j):
    n_chunks = rq_ref.shape[0] // RET_CHUNK
    c = RET_CHUNK
    lane = lax.broadcasted_iota(jnp.int32, (c, LANES), 1)
    n_pair = N_RET_HEADS // 2
    st = [jnp.where(j == 0, 0.0, state[hp]) for hp in range(n_pair)]
    for ci in range(n_chunks):
        rows = slice(ci * c, (ci + 1) * c)
        for hp in range(n_pair):
            cols = slice(hp * LANES, (hp + 1) * LANES)
            qp = rq_ref[rows, cols]
            ktp = rkt_ref[cols, rows]
            ktp16 = ktp.astype(BF16)
            stp = st[hp]
            stp16 = stp.astype(BF16)
            q2 = jnp.concatenate(
                [jnp.where((lane >= e * RET_QK_DIM) & (lane < (e + 1) * RET_QK_DIM),
                           qp, jnp.zeros_like(qp)) for e in range(2)], axis=0)
            sc2 = jnp.dot(q2, ktp16, preferred_element_type=F32)
            cross2 = jnp.dot(q2, stp16, preferred_element_type=F32)
            new = []
            for e in range(2):
                h = hp * 2 + e
                hs = slice(e * RET_QK_DIM, (e + 1) * RET_QK_DIM)
                hv = slice(h * RET_V_DIM, (h + 1) * RET_V_DIM)
                sc = sc2[e * c:(e + 1) * c] * dec_ref[h]
                vh = rv_ref[rows, hv]
                kd = (ktp[hs] * kdec_ref[h]).astype(BF16)
                both = jnp.dot(jnp.concatenate([sc.astype(BF16), kd], axis=0), vh,
                               preferred_element_type=F32)
                cross = cross2[e * c:(e + 1) * c] * qdec_ref[h]
                r_ref[rows, hv] = _gated_norm(both[:c] + cross, rg_ref[rows, hv]).astype(r_ref.dtype)
                new.append(gc_ref[h] * stp[hs] + both[c:])
            st[hp] = jnp.concatenate(new, axis=0)
    for hp in range(n_pair):
        state[hp] = st[hp]
        st_ref[hp] = st[hp]


def _sret_kernel(rq_ref, rkt_ref, rv_ref, rg_ref, st_in_ref, dec_ref, qdec_ref, kdec_ref, gc_ref,
                 r_ref, st_out_ref, *, t_seq):
    n = LANES
    nb = n // t_seq
    lane = lax.broadcasted_iota(jnp.int32, (n, LANES), 1)
    tok_r = lax.broadcasted_iota(jnp.int32, (nb, n, LANES), 1) // t_seq
    seq_r = lax.broadcasted_iota(jnp.int32, (nb, n, LANES), 0)
    row_in_seq = tok_r == seq_r
    tok_l = lax.broadcasted_iota(jnp.int32, (nb, RET_QK_DIM, n), 2) // t_seq
    seq_l = lax.broadcasted_iota(jnp.int32, (nb, RET_QK_DIM, n), 0)
    lane_in_seq = tok_l == seq_l
    for hp in range(N_RET_HEADS // 2):
        cols = slice(hp * LANES, (hp + 1) * LANES)
        qp = rq_ref[0, :, cols]
        ktp = rkt_ref[0, cols, :]
        ktp16 = ktp.astype(BF16)
        stp = st_in_ref[:, hp]
        st16 = stp.astype(BF16).reshape(nb * LANES, RET_V_DIM)
        for e in range(2):
            h = hp * 2 + e
            hs = slice(e * RET_QK_DIM, (e + 1) * RET_QK_DIM)
            hv = slice(h * RET_V_DIM, (h + 1) * RET_V_DIM)
            in_head = (lane >= e * RET_QK_DIM) & (lane < (e + 1) * RET_QK_DIM)
            qh = jnp.where(in_head, qp, jnp.zeros_like(qp))
            sc = jnp.dot(qh, ktp16, preferred_element_type=F32) * dec_ref[h]
            vh = rv_ref[0, :, hv]
            inner = jnp.dot(sc.astype(BF16), vh, preferred_element_type=F32)
            qbd = jnp.where(row_in_seq, qh[None], jnp.zeros_like(qh)[None])
            qbd = jnp.concatenate([qbd[b] for b in range(nb)], axis=1)
            cross = jnp.dot(qbd, st16, preferred_element_type=F32) * qdec_ref[h]
            r_ref[0, :, hv] = _gated_norm(inner + cross, rg_ref[0, :, hv]).astype(r_ref.dtype)
            kd = (ktp[hs] * kdec_ref[h]).astype(BF16)
            kds = jnp.where(lane_in_seq, kd[None], jnp.zeros_like(kd)[None])
            upd = jnp.dot(kds.reshape(nb * RET_QK_DIM, n), vh, preferred_element_type=F32)
            st_out_ref[:, hp, hs, :] = (gc_ref[h] * stp[:, hs, :]
                                        + upd.reshape(nb, RET_QK_DIM, RET_V_DIM))


def _sample_retention(rq, rkt, rv, rg, state, t_seq):
    n_tok = rq.shape[1]
    db = state.shape[0]
    nb = LANES // t_seq
    n_pair = N_RET_HEADS // 2
    st_pairs = state.reshape(db, n_pair, LANES, RET_V_DIM)
    tables = _retention_tables(t_seq, LANES)
    row = lambda width: pl.BlockSpec((1, LANES, width), lambda j: (0, j, 0))
    st_spec = pl.BlockSpec((nb, n_pair, LANES, RET_V_DIM), lambda j: (j, 0, 0, 0))
    r, st = pl.pallas_call(
        functools.partial(_sret_kernel, t_seq=t_seq),
        grid=(n_tok // LANES,),
        in_specs=[row(RET_QW), pl.BlockSpec((1, RET_QW, LANES), lambda j: (0, 0, j)),
                  row(RET_VW), row(RET_VW), st_spec] + [_const_spec(t.shape) for t in tables],
        out_specs=(row(RET_VW), st_spec),
        out_shape=(jax.ShapeDtypeStruct((1, n_tok, RET_VW), BF16),
                   jax.ShapeDtypeStruct(st_pairs.shape, F32)),
        compiler_params=_params("parallel"),
        name="sample_ret",
    )(rq, rkt, rv, rg, st_pairs, *tables)
    return r, st.reshape(state.shape)


def _sattn_compute(dl_ref, g_ref, qb, kn, vn, k_page, v_page, n_pages, page):
    t_seq = kn.shape[0]
    nh = N_DIFF_HEADS
    half = nh * t_seq
    qb = qb.astype(F32)
    s = jnp.concatenate(
        [jnp.dot(qb, k_page(p), preferred_element_type=F32) for p in range(n_pages)],
        axis=1)
    kn = kn.astype(F32)
    t_row = lax.broadcasted_iota(jnp.int32, (2 * half, 1), 0) % t_seq
    s_new = []
    for tk in range(t_seq):
        col = jnp.sum(qb * kn[tk:tk + 1, :], axis=-1, keepdims=True)
        s_new.append(jnp.where(tk <= t_row, col, MASK_VALUE))
    m = jnp.max(s, axis=-1, keepdims=True)
    for col in s_new:
        m = jnp.maximum(m, col)
    p = jnp.exp2(s - m)
    p_new = [jnp.exp2(col - m) for col in s_new]
    l = jnp.sum(p, axis=-1, keepdims=True)
    for col in p_new:
        l = l + col
    lam = _lambda_full(dl_ref)
    a = (p[:half] / l[:half] - lam * (p[half:] / l[half:])).astype(BF16)
    a_new = [(c[:half] / l[:half] - lam * (c[half:] / l[half:])).astype(BF16).astype(F32)
             for c in p_new]
    a_pages = jnp.concatenate([a[:, pg * page:(pg + 1) * page] for pg in range(n_pages)],
                              axis=0)
    spread = (lax.broadcasted_iota(jnp.int32, (page, page * nh), 1) // nh
              == lax.broadcasted_iota(jnp.int32, (page, page * nh), 0)).astype(BF16)
    ax = jnp.dot(a_pages, spread, preferred_element_type=F32)
    col_head = lax.broadcasted_iota(jnp.int32, ax.shape, 1) % nh
    row_head = (lax.broadcasted_iota(jnp.int32, ax.shape, 0) % half) // t_seq
    ax = jnp.where(col_head == row_head, ax, 0.0)
    acc = jnp.zeros((half, DIFF_V_DIM), F32)
    for pg in range(n_pages):
        acc = acc + jnp.dot(ax[pg * half:(pg + 1) * half], v_page(pg),
                            preferred_element_type=F32)
    vn = vn.astype(F32)
    out = []
    for h in range(nh):
        hv = slice(h * DIFF_V_DIM, (h + 1) * DIFF_V_DIM)
        rows = slice(h * t_seq, (h + 1) * t_seq)
        o = acc[rows]
        for tk in range(t_seq):
            o = o + a_new[tk][rows] * vn[tk:tk + 1, hv]
        out.append(_subln(o, g_ref[...]))
    return jnp.concatenate(out, axis=1)


def _ffn_stages(x_ref, d_ref, r_ref, wo_ref, gf_ref, wi_ref, wo2_ref, gl_ref, y_ref, act, fc):
    d_ff = wo2_ref.shape[0]
    dm = x_ref.shape[1]
    live = {}

    def head():
        mix = (jnp.dot(d_ref[...], wo_ref[0:DIFF_VW, :], preferred_element_type=F32)
               + jnp.dot(r_ref[...], wo_ref[DIFF_VW:, :], preferred_element_type=F32))
        live["x1"] = x_ref[...] + mix
        live["hb"] = (_rms(live["x1"], NORM_EPS) * gf_ref[...]).astype(BF16)

    def chunk(c):
        def run():
            hb = live["hb"]
            g = jnp.dot(hb, wi_ref[:, c * fc:(c + 1) * fc], preferred_element_type=F32)
            u = jnp.dot(hb, wi_ref[:, d_ff + c * fc:d_ff + (c + 1) * fc],
                        preferred_element_type=F32)
            act[:, c * fc:(c + 1) * fc] = (jax.nn.silu(g) * u).astype(BF16)
        return run

    def tail():
        x2 = live["x1"] + jnp.dot(act[...], wo2_ref[...], preferred_element_type=F32)
        y_ref[...] = _rms(x2, NORM_EPS) * gl_ref[...]

    return ([(wo_ref.shape[0] * dm, head)]
            + [(2 * dm * fc, chunk(c)) for c in range(d_ff // fc)]
            + [(d_ff * dm, tail)])


def _ffn_kernel(*refs, fc):
    for _, stage in _ffn_stages(*refs, fc):
        stage()


def _merge_ffn(x, d, r, w_out, g_ffn, w_ffn_in, w_ffn_out, g_final, tm, fc):
    n, dm = x.shape
    d_ff = w_ffn_out.shape[0]
    row = lambda width: pl.BlockSpec((tm, width), lambda i: (i, 0))
    return pl.pallas_call(
        functools.partial(_ffn_kernel, fc=fc),
        grid=(n // tm,),
        in_specs=[row(dm), row(DIFF_VW), row(RET_VW), _const_spec(w_out.shape),
                  _const_spec(g_ffn.shape), _const_spec(w_ffn_in.shape),
                  _const_spec(w_ffn_out.shape), _const_spec(g_final.shape)],
        out_specs=row(dm),
        out_shape=jax.ShapeDtypeStruct((n, dm), F32),
        scratch_shapes=[pltpu.VMEM((tm, d_ff), BF16)],
        compiler_params=_params("parallel"),
        name="merge_ffn",
    )(x, d, r, w_out, g_ffn, w_ffn_in, w_ffn_out, g_final)


def _ffn_sattn_kernel(pt_ref, x_ref, d_ref, r_ref, wo_ref, gf_ref, wi_ref, wo2_ref, gl_ref,
                      dl_ref, g_ref, qb_ref, kn_ref, vn_ref, kc_hbm, vc_hbm,
                      y_ref, ds_ref, act, kbuf, vbuf, sem, *, fc, n_seq, n_pages):
    step = pl.program_id(0)
    per_step = qb_ref.shape[0]
    page = kbuf.shape[3]

    def page_copies(seq, slot):
        copies = []
        for p in range(n_pages):
            pg = pt_ref[seq * n_pages + p]
            copies.append(pltpu.make_async_copy(kc_hbm.at[pg], kbuf.at[slot, p], sem.at[slot]))
            copies.append(pltpu.make_async_copy(vc_hbm.at[pg], vbuf.at[slot, p], sem.at[slot]))
        return copies

    def start(seq, slot):
        for c in page_copies(seq, slot):
            c.start()

    def wait(seq, slot):
        for c in page_copies(seq, slot):
            c.wait()

    @pl.when(step == 0)
    def _():
        for t in range(N_PAGE_SLOTS):
            start(t, t)

    stages = _ffn_stages(x_ref, d_ref, r_ref, wo_ref, gf_ref, wi_ref, wo2_ref, gl_ref, y_ref, act, fc)
    total = sum(w for w, _ in stages)
    shares = [[] for _ in range(per_step)]
    done = 0
    for w, stage in stages:
        shares[min(per_step - 1, (2 * done + w) * per_step // (2 * total))].append(stage)
        done += w

    for t in range(per_step):
        seq = step * per_step + t
        slot = t % N_PAGE_SLOTS
        wait(seq, slot)
        o = _sattn_compute(dl_ref, g_ref, qb_ref[t], kn_ref[t], vn_ref[t],
                           lambda p: kbuf[slot, p], lambda p: vbuf[slot, p], n_pages, page)
        ds_ref[t] = o.astype(ds_ref.dtype)
        for stage in shares[t]:
            stage()

        @pl.when(seq + N_PAGE_SLOTS < n_seq)
        def _():
            start(seq + N_PAGE_SLOTS, slot)


def _merge_ffn_sample_attention(x, d, r, w_out, g_ffn, w_ffn_in, w_ffn_out, g_final, tm, fc,
                                page_table, dl, subln_g, qblk, k_new, v_new, kc, vc):
    n, dm = x.shape
    d_ff = w_ffn_out.shape[0]
    db, n_pages = page_table.shape
    t_seq = k_new.shape[1]
    page = kc.shape[2]
    n_steps = n // tm
    assert db % n_steps == 0 and (db // n_steps) % N_PAGE_SLOTS == 0, "sequences per FFN step"
    per_step = db // n_steps
    row = lambda width: pl.BlockSpec((tm, width), lambda i, pt: (i, 0))
    const = lambda a: pl.BlockSpec(a.shape, lambda i, pt: (0,) * a.ndim,
                                   pipeline_mode=pl.Buffered(1))
    per_seq = lambda a: pl.BlockSpec((per_step,) + a.shape[1:], lambda i, pt: (i, 0, 0))
    hbm = pl.BlockSpec(memory_space=pl.ANY)
    grid_spec = pltpu.PrefetchScalarGridSpec(
        num_scalar_prefetch=1,
        grid=(n_steps,),
        in_specs=[row(dm), row(DIFF_VW), row(RET_VW), const(w_out), const(g_ffn),
                  const(w_ffn_in), const(w_ffn_out), const(g_final), const(dl), const(subln_g),
                  per_seq(qblk), per_seq(k_new), per_seq(v_new), hbm, hbm],
        out_specs=(row(dm), pl.BlockSpec((per_step, t_seq, DIFF_VW), lambda i, pt: (i, 0, 0))),
        scratch_shapes=[pltpu.VMEM((tm, d_ff), BF16),
                        pltpu.VMEM((N_PAGE_SLOTS, n_pages, DIFF_W, page), F32),
                        pltpu.VMEM((N_PAGE_SLOTS, n_pages, page * N_DIFF_HEADS, DIFF_V_DIM), F32),
                        pltpu.SemaphoreType.DMA((N_PAGE_SLOTS,))],
    )
    return pl.pallas_call(
        functools.partial(_ffn_sattn_kernel, fc=fc, n_seq=db, n_pages=n_pages),
        grid_spec=grid_spec,
        out_shape=(jax.ShapeDtypeStruct((n, dm), F32),
                   jax.ShapeDtypeStruct((db, t_seq, DIFF_VW), BF16)),
        compiler_params=_params("arbitrary"),
        name="merge_ffn_sample_attn",
    )(page_table.reshape(-1), x, d, r, w_out, g_ffn, w_ffn_in, w_ffn_out, g_final,
      dl, subln_g, qblk, k_new, v_new, kc, vc)


def _ffn_chunk(d_ff):
    for fc in (512, 256, 128):
        if d_ff % fc == 0:
            return fc
    return d_ff


def kernel(x_prompt, x_sample, cache_diff_k, cache_diff_v, state_ret, page_table, norm_mix_g, w_in, diff_lambda, diff_subln_g, w_out, norm_ffn_g, w_ffn_in, w_ffn_out, norm_final_g):
    bsz, seq, dm = x_prompt.shape
    db, t_seq, _ = x_sample.shape
    n_pages = page_table.shape[1]
    page = cache_diff_k.shape[2]
    past = n_pages * page
    assert w_in.shape[0] == 1, "single layer"
    assert LANES % t_seq == 0 and (db * t_seq) % LANES == 0

    w_in0 = w_in[0]
    w_out16 = w_out[0].astype(BF16)
    w_ffn_in16 = w_ffn_in[0].astype(BF16)
    w_ffn_out16 = w_ffn_out[0].astype(BF16)
    g_mix, g_ffn = norm_mix_g, norm_ffn_g
    g_final = norm_final_g.reshape(1, dm)
    dl = diff_lambda[0]
    fc = _ffn_chunk(w_ffn_out.shape[1])

    tm = min(512, seq)
    tabs_p = _rotary_tables(jnp.arange(seq))
    qt16, k16, kt32, v32, vt16, r_p, ret_state_p = _inproj(
        x_prompt, g_mix, w_in0, tabs_p, min(1024, seq), with_retention=True)
    d_p = _prompt_attention(dl, diff_subln_g, qt16, k16, vt16, min(1024, seq), min(512, seq))
    k_prompt = jnp.swapaxes(kt32, 1, 2).reshape(1, bsz, seq, N_DIFF_HEADS, 2, DIFF_QK_DIM)
    v_prompt = v32.reshape(1, bsz, seq, N_DIFF_HEADS, DIFF_V_DIM)
    ret_prompt = ret_state_p.reshape(1, bsz, N_RET_HEADS, RET_QK_DIM, RET_V_DIM)

    n_s = db * t_seq
    pos_s = past + (jnp.arange(n_s) % t_seq)
    tabs_s = _rotary_tables(pos_s)
    tm_s = min(512, n_s)
    _, ks16, _, vs32, _, rqs, rkts, rvs, rgs, qs, k_s = _inproj(
        x_sample.reshape(1, n_s, dm), g_mix, w_in0, tabs_s, tm_s, with_retention=False)
    feat_owner = (np.arange(DIFF_W) // DIFF_QK_DIM).reshape(1, 1, 1, DIFF_W)
    row_owner = (np.arange(N_DIFF_HEADS)[None, :, None, None] * 2
                 + np.arange(2)[:, None, None, None])
    qblk = jnp.where((feat_owner == row_owner)[None],
                     qs.reshape(db, 1, 1, t_seq, DIFF_W), jnp.zeros((), BF16))
    qblk = qblk.reshape(db, 2 * N_DIFF_HEADS * t_seq, DIFF_W)
    kc = jnp.transpose(cache_diff_k[0], (0, 2, 3, 4, 1)).reshape(-1, DIFF_W, page)
    y_prompt, d_s = _merge_ffn_sample_attention(
        x_prompt.reshape(bsz * seq, dm), d_p.reshape(bsz * seq, DIFF_VW),
        r_p.reshape(bsz * seq, RET_VW), w_out16, g_ffn, w_ffn_in16, w_ffn_out16, g_final, tm, fc,
        page_table, dl, diff_subln_g, qblk, ks16.reshape(db, t_seq, DIFF_W),
        vs32.astype(BF16).reshape(db, t_seq, DIFF_VW), kc,
        cache_diff_v[0].reshape(-1, page * N_DIFF_HEADS, DIFF_V_DIM))
    y_prompt = y_prompt.reshape(bsz, seq, dm)
    r_s, ret_state_s = _sample_retention(rqs, rkts, rvs, rgs, state_ret[0], t_seq)
    y_sample = _merge_ffn(x_sample.reshape(n_s, dm), d_s.reshape(n_s, DIFF_VW),
                          r_s.reshape(n_s, RET_VW), w_out16, g_ffn, w_ffn_in16,
                          w_ffn_out16, g_final, tm_s, fc).reshape(db, t_seq, dm)
    k_sample = k_s.reshape(1, db, t_seq, N_DIFF_HEADS, 2, DIFF_QK_DIM)
    v_sample = vs32.reshape(1, db, t_seq, N_DIFF_HEADS, DIFF_V_DIM)
    ret_sample = ret_state_s[None]

    return (y_prompt, y_sample, k_prompt, v_prompt, ret_prompt, k_sample, v_sample, ret_sample)
```

```python
import functools
import math

import jax
import jax.numpy as jnp
import numpy as np
from jax import lax
from jax.experimental import pallas as pl
from jax.experimental.pallas import tpu as pltpu

F32 = jnp.float32
BF16 = jnp.bfloat16

N_DIFF_HEADS = 4
DIFF_QK_DIM = 64
DIFF_V_DIM = 128
ROT_DIM = 16
ROPE_THETA = 500000.0
N_RET_HEADS = 4
RET_QK_DIM = 64
RET_V_DIM = 128
RET_THETA = 10000.0
RET_CHUNK = 128
NORM_EPS = 1e-6
SUBLN_EPS = 1e-5
LAM_INIT = 0.8 - 0.6 * math.exp(-0.3 * 0)
MASK_VALUE = -1e30
Q_SCALE = DIFF_QK_DIM ** -0.5 * math.log2(math.e)

DIFF_W = N_DIFF_HEADS * 2 * DIFF_QK_DIM
DIFF_VW = N_DIFF_HEADS * DIFF_V_DIM
RET_QW = N_RET_HEADS * RET_QK_DIM
RET_VW = N_RET_HEADS * RET_V_DIM
W_IN_OFFSETS = tuple(int(o) for o in np.cumsum([0, DIFF_W, DIFF_W, DIFF_VW, RET_QW, RET_QW, RET_VW]))

LANES = 128
ACC_PAD = 16
N_PAGE_SLOTS = 2
VMEM_LIMIT = 59 * 1024 * 1024


def _params(*sem, flags=None):
    return pltpu.CompilerParams(dimension_semantics=sem, vmem_limit_bytes=VMEM_LIMIT, flags=flags)


def _const_spec(shape):
    nd = len(shape)
    return pl.BlockSpec(shape, lambda *_: (0,) * nd, pipeline_mode=pl.Buffered(1))


def _rms(x, eps):
    return x * lax.rsqrt(jnp.mean(x * x, axis=-1, keepdims=True) + eps)


def _inv_freq(freq_idx, dim, theta):
    return 1.0 / (jnp.float32(theta) ** (jnp.asarray(2 * freq_idx, F32) / dim))


def _rotary_tables(pos):
    posf = pos.astype(F32)

    def feature_tables(half, dim, theta, scale):
        ang = _inv_freq(np.arange(half), dim, theta)[:, None] * posf[None, :]
        return jnp.cos(ang) * scale, jnp.sin(ang) * scale

    return (feature_tables(ROT_DIM // 2, ROT_DIM, ROPE_THETA, Q_SCALE)
            + feature_tables(ROT_DIM // 2, ROT_DIM, ROPE_THETA, 1.0)
            + feature_tables(RET_QK_DIM // 2, RET_QK_DIM, RET_THETA, 1.0)
            + feature_tables(RET_QK_DIM // 2, RET_QK_DIM, RET_THETA, RET_QK_DIM ** -0.5))


def _log_gamma():
    return jnp.log(1.0 - 2.0 ** (-5.0 - jnp.arange(N_RET_HEADS, dtype=F32)))


def _retention_tables(chunk, n_tok):
    lg = _log_gamma()
    idx = jnp.arange(n_tok)
    loc = (idx % chunk).astype(F32)
    rel = loc[:, None] - loc[None, :]
    same = (idx[:, None] // chunk) == (idx[None, :] // chunk)
    decay = jnp.where(same[None] & (rel >= 0)[None],
                      jnp.exp(lg[:, None, None] * jnp.maximum(rel, 0.0)[None]), 0.0)
    qdec = jnp.exp(lg[:, None] * (loc[None, :] + 1.0))[:, :, None]
    kdec = jnp.exp(lg[:, None] * (chunk - 1.0 - loc[None, :]))[:, None, :]
    gc = jnp.broadcast_to(jnp.exp(lg * chunk)[:, None, None], (N_RET_HEADS, 1, LANES))
    return decay, qdec, kdec, gc


def _inproj_kernel(x_ref, g_ref, w_ref,
                   cqt_ref, sqt_ref, ckt_ref, skt_ref, crqt_ref, srqt_ref, crkt_ref, srkt_ref,
                   *refs, with_retention):
    if with_retention:
        (dec_ref, qdec_ref, kdec_ref, gc_ref,
         qt_ref, k16_ref, kt32_ref, v32_ref, vt16_ref, r_ref, st_ref,
         rqt_scr, rq_dst, rkt_dst, rv_dst, rg_dst, state) = refs
    else:
        (qt_ref, k16_ref, kt32_ref, v32_ref, vt16_ref, rq_ref, rkt_ref, rv_ref, rg_ref,
         rqt_scr) = refs
        rq_dst, rkt_dst, rv_dst, rg_dst = rq_ref.at[0], rkt_ref.at[0], rv_ref.at[0], rg_ref.at[0]
    x = x_ref[0]
    hb = (_rms(x, NORM_EPS) * g_ref[...]).astype(BF16)

    def mm(lo, width):
        return jnp.dot(hb, w_ref[:, lo:lo + width].astype(BF16), preferred_element_type=F32)

    def rot_t(zt, base, width, half, c, s, rest_scale=None):
        x1, x2 = zt[base:base + half], zt[base + half:base + 2 * half]
        parts = [x1 * c - x2 * s, x2 * c + x1 * s]
        if 2 * half < width:
            rest = zt[base + 2 * half:base + width]
            parts.append(rest if rest_scale is None else rest * rest_scale)
        return jnp.concatenate(parts, axis=0)

    o_dq, o_dk, o_dv, o_rq, o_rk, o_rv, o_rg = W_IN_OFFSETS

    def retention_group():
        zt = mm(o_rq, RET_QW).T
        c, s = crqt_ref[...], srqt_ref[...]
        for h in range(N_RET_HEADS):
            b = h * RET_QK_DIM
            rqt_scr[b:b + RET_QK_DIM, :] = rot_t(zt, b, RET_QK_DIM, RET_QK_DIM // 2, c, s)
        rq_dst[...] = rqt_scr[...].T.astype(BF16)
        zt = mm(o_rk, RET_QW).T
        c, s = crkt_ref[...], srkt_ref[...]
        for h in range(N_RET_HEADS):
            b = h * RET_QK_DIM
            rkt_dst[b:b + RET_QK_DIM, :] = rot_t(zt, b, RET_QK_DIM, RET_QK_DIM // 2, c, s)
        rv_dst[...] = mm(o_rv, RET_VW).astype(BF16)
        rg_dst[...] = mm(o_rg, RET_VW)

    def diff_group():
        hr = ROT_DIM // 2
        zt = mm(o_dq, DIFF_W).T
        c, s = cqt_ref[...], sqt_ref[...]
        for g in range(N_DIFF_HEADS * 2):
            b = g * DIFF_QK_DIM
            qt_ref[0, b:b + DIFF_QK_DIM, :] = rot_t(
                zt, b, DIFF_QK_DIM, hr, c, s, rest_scale=Q_SCALE).astype(BF16)
        zt = mm(o_dk, DIFF_W).T
        c, s = ckt_ref[...], skt_ref[...]
        for g in range(N_DIFF_HEADS * 2):
            b = g * DIFF_QK_DIM
            kt32_ref[0, b:b + DIFF_QK_DIM, :] = rot_t(zt, b, DIFF_QK_DIM, hr, c, s)
        k16_ref[0] = kt32_ref[0].T.astype(BF16)
        zv = mm(o_dv, DIFF_VW)
        for h in range(N_DIFF_HEADS):
            v32_ref[0, pl.ds(h, zv.shape[0], stride=N_DIFF_HEADS), :] = (
                zv[:, h * DIFF_V_DIM:(h + 1) * DIFF_V_DIM])
        vt16_ref[0] = zv.T.astype(BF16)

    retention_group()
    if with_retention:
        _retention_step(rq_dst, rkt_dst, rv_dst, rg_dst, dec_ref, qdec_ref, kdec_ref, gc_ref,
                        r_ref.at[0], st_ref.at[0], state, pl.program_id(1))
    diff_group()


def _inproj(x, g, w_in, tables, tm, with_retention):
    b, s, d = x.shape
    row = lambda width: pl.BlockSpec((1, tm, width), lambda bi, i: (bi, i, 0))
    col = lambda height: pl.BlockSpec((1, height, tm), lambda bi, i: (bi, 0, i))
    ttab = lambda height: pl.BlockSpec((height, tm), lambda bi, i: (0, i))
    n_pair = N_RET_HEADS // 2
    out_shape = [
        jax.ShapeDtypeStruct((b, DIFF_W, s), BF16),
        jax.ShapeDtypeStruct((b, s, DIFF_W), BF16),
        jax.ShapeDtypeStruct((b, DIFF_W, s), F32),
        jax.ShapeDtypeStruct((b, s * N_DIFF_HEADS, DIFF_V_DIM), F32),
        jax.ShapeDtypeStruct((b, DIFF_VW, s), BF16),
    ]
    v_rows = pl.BlockSpec((1, tm * N_DIFF_HEADS, DIFF_V_DIM), lambda bi, i: (bi, i, 0))
    out_specs = [col(DIFF_W), row(DIFF_W), col(DIFF_W), v_rows, col(DIFF_VW)]
    in_specs = ([row(d), _const_spec((1, d)), _const_spec(w_in.shape)]
                + [ttab(ROT_DIM // 2)] * 4 + [ttab(RET_QK_DIM // 2)] * 4)
    scratch = [pltpu.VMEM((RET_QW, tm), F32)]
    operands = [x, g, w_in, *tables]
    ret_inputs = [((tm, RET_QW), BF16), ((RET_QW, tm), F32), ((tm, RET_VW), BF16), ((tm, RET_VW), F32)]
    if with_retention:
        ret_tables = _retention_tables(RET_CHUNK, RET_CHUNK)
        in_specs += [_const_spec(t.shape) for t in ret_tables]
        operands += list(ret_tables)
        out_shape += [jax.ShapeDtypeStruct((b, s, RET_VW), BF16),
                      jax.ShapeDtypeStruct((b, n_pair, LANES, RET_V_DIM), F32)]
        out_specs += [row(RET_VW),
                      pl.BlockSpec((1, n_pair, LANES, RET_V_DIM), lambda bi, i: (bi, 0, 0, 0))]
        scratch += [pltpu.VMEM(shape, dt) for shape, dt in ret_inputs]
        scratch += [pltpu.VMEM((n_pair, LANES, RET_V_DIM), F32)]
    else:
        out_shape += [jax.ShapeDtypeStruct((b, s, RET_QW), BF16),
                      jax.ShapeDtypeStruct((b, RET_QW, s), F32),
                      jax.ShapeDtypeStruct((b, s, RET_VW), BF16),
                      jax.ShapeDtypeStruct((b, s, RET_VW), F32)]
        out_specs += [row(RET_QW), col(RET_QW), row(RET_VW), row(RET_VW)]
    return pl.pallas_call(
        functools.partial(_inproj_kernel, with_retention=with_retention),
        grid=(b, s // tm),
        in_specs=in_specs,
        out_specs=out_specs,
        out_shape=out_shape,
        scratch_shapes=scratch,
        compiler_params=_params("parallel", "arbitrary" if with_retention else "parallel"),
        name="inproj_ret" if with_retention else "inproj",
    )(*operands)


def _lambda_full(dl_ref):
    lp = dl_ref[...]
    a = jnp.sum(lp[0:1] * lp[1:2], axis=-1, keepdims=True)
    b = jnp.sum(lp[2:3] * lp[3:4], axis=-1, keepdims=True)
    return jnp.exp(a) - jnp.exp(b) + LAM_INIT


def _subln(o, g):
    return _rms(o, SUBLN_EPS) * g * (1.0 - LAM_INIT)


def _attn_kernel(dl_ref, g_ref, qt_ref, qt_next_ref, k_ref, vt_ref, o_ref, acc, s_scr, p_scr,
                 *, tq, tk):
    i = pl.program_id(2)
    first_step = (pl.program_id(0) == 0) & (pl.program_id(1) == 0) & (i == 0)

    def score_columns(qt):
        row = lax.broadcasted_iota(jnp.int32, (qt.shape[0], tk), 0)
        groups = []
        for half in range(2):
            qh = qt[:, half * tk:(half + 1) * tk]
            groups += [jnp.where(row < DIFF_QK_DIM, qh, jnp.zeros_like(qh)),
                       jnp.where(row >= DIFF_QK_DIM, qh, jnp.zeros_like(qh))]
        return jnp.concatenate(groups, axis=1)

    qtb = score_columns(qt_ref[0])
    n_full = 2 * i
    all_cols, lo, hi = slice(0, 2 * tq), slice(0, tq), slice(tq, 2 * tq)

    def issue_scores(j, par, cols, q=None):
        off = pl.multiple_of(j * tk, tk)
        q = qtb if q is None else q
        s_scr[par, :, cols] = jnp.dot(k_ref[0, pl.ds(off, tk), :], q[:, cols],
                                      preferred_element_type=F32)

    def softmax_stage(par, m, cols, causal):
        s = s_scr[par, :, cols]
        if causal:
            c = lax.broadcasted_iota(jnp.int32, (tk, tq), 1)
            c = jnp.where(c >= tk, c - tk, c)
            s = jnp.where(lax.broadcasted_iota(jnp.int32, (tk, tq), 0) <= c, s, MASK_VALUE)
        m_new = jnp.maximum(m, jnp.max(s, axis=0, keepdims=True))
        alpha = jnp.exp2(m - m_new)
        p_scr[par, :, cols] = jnp.exp2(s - m_new).astype(BF16)
        return m_new, alpha

    ones_rows = (lax.broadcasted_iota(jnp.int32, (ACC_PAD, tk), 0) == 0).astype(BF16)

    def value_stage(j, par, alpha, cols):
        off = pl.multiple_of(j * tk, tk)
        vt = jnp.concatenate([vt_ref[0, :, pl.ds(off, tk)], ones_rows], axis=0)
        acc[:, cols] = alpha * acc[:, cols] + jnp.dot(vt, p_scr[par, :, cols],
                                                      preferred_element_type=F32)

    def substep(j, par, m, alpha_prev):
        issue_scores(j + 1, 1 - par, all_cols)
        value_stage(jnp.maximum(j - 1, 0), 1 - par, alpha_prev, all_cols)
        return softmax_stage(par, m, all_cols, False)

    def body(t, carry):
        m, alpha = substep(2 * t, 0, *carry)
        return substep(2 * t + 1, 1, m, alpha)

    @pl.when(i == 0)
    def _():
        issue_scores(0, 0, all_cols)

    @pl.when(first_step)
    def _():
        acc[...] = jnp.zeros_like(acc)
        p_scr[1] = jnp.zeros((tk, 2 * tq), BF16)

    carry = (jnp.full((1, 2 * tq), MASK_VALUE, F32), jnp.ones((1, 2 * tq), F32))
    m, alpha = lax.fori_loop(0, i, body, carry)
    issue_scores(n_full + 1, 1, hi)
    value_stage(jnp.maximum(n_full - 1, 0), 1, alpha, all_cols)
    _, alpha_lo = softmax_stage(0, m[:, lo], lo, True)
    m_hi, alpha_hi = softmax_stage(0, m[:, hi], hi, False)
    value_stage(n_full, 0, jnp.concatenate([alpha_lo, alpha_hi], axis=1), all_cols)
    _, alpha_hi = softmax_stage(1, m_hi, hi, True)
    value_stage(n_full + 1, 1, alpha_hi, hi)
    issue_scores(0, 0, all_cols, q=score_columns(qt_next_ref[0]))

    lam = _lambda_full(dl_ref)
    out = []
    for half in range(2):
        c0 = slice(2 * half * tk, (2 * half + 1) * tk)
        c1 = slice((2 * half + 1) * tk, (2 * half + 2) * tk)
        out.append(acc[0:DIFF_V_DIM, c0] / acc[DIFF_V_DIM:DIFF_V_DIM + 1, c0]
                   - lam * (acc[0:DIFF_V_DIM, c1] / acc[DIFF_V_DIM:DIFF_V_DIM + 1, c1]))
    ot = jnp.concatenate(out, axis=1)
    o_ref[0] = _subln(ot.T, g_ref[...]).astype(o_ref.dtype)


def _prompt_attention(dl, subln_g, qt16, k16, vt16, tq, tk):
    b, s, _ = k16.shape
    assert tq == 2 * tk, "a query block spans two key blocks (two scratch slots)"
    n_q = s // tq
    kernel = functools.partial(_attn_kernel, tq=tq, tk=tk)
    return pl.pallas_call(
        kernel,
        grid=(b, N_DIFF_HEADS, s // tq),
        in_specs=[
            _const_spec(dl.shape),
            _const_spec(subln_g.shape),
            pl.BlockSpec((1, LANES, tq), lambda bi, h, i: (bi, h, i)),
            pl.BlockSpec((1, LANES, tq), lambda bi, h, i: (bi, h, jnp.minimum(i + 1, n_q - 1))),
            pl.BlockSpec((1, s, LANES), lambda bi, h, i: (bi, 0, h)),
            pl.BlockSpec((1, LANES, s), lambda bi, h, i: (bi, h, 0)),
        ],
        out_specs=pl.BlockSpec((1, tq, LANES), lambda bi, h, i: (bi, i, h)),
        out_shape=jax.ShapeDtypeStruct((b, s, DIFF_VW), BF16),
        scratch_shapes=[pltpu.VMEM((DIFF_V_DIM + ACC_PAD, 2 * tq), F32),
                        pltpu.VMEM((2, tk, 2 * tq), F32),
                        pltpu.VMEM((2, tk, 2 * tq), BF16)],
        compiler_params=_params("arbitrary", "arbitrary", "arbitrary"),
        name="prompt_attn",
    )(dl, subln_g, qt16, qt16, k16, vt16)


def _gated_norm(o, gate):
    return jax.nn.silu(gate) * _rms(o, NORM_EPS)


def _retention_step(rq_ref, rkt_ref, rv_ref, rg_ref, dec_ref, qdec_ref, kdec_ref, gc_ref,
                    r_ref, st_ref, state, j):
    n_chunks = rq_ref.shape[0] // RET_CHUNK
    c = RET_CHUNK
    lane = lax.broadcasted_iota(jnp.int32, (c, LANES), 1)
    n_pair = N_RET_HEADS // 2
    st = [jnp.where(j == 0, 0.0, state[hp]) for hp in range(n_pair)]
    for ci in range(n_chunks):
        rows = slice(ci * c, (ci + 1) * c)
        for hp in range(n_pair):
            cols = slice(hp * LANES, (hp + 1) * LANES)
            qp = rq_ref[rows, cols]
            ktp = rkt_ref[cols, rows]
            ktp16 = ktp.astype(BF16)
            stp = st[hp]
            stp16 = stp.astype(BF16)
            q2 = jnp.concatenate(
                [jnp.where((lane >= e * RET_QK_DIM) & (lane < (e + 1) * RET_QK_DIM),
                           qp, jnp.zeros_like(qp)) for e in range(2)], axis=0)
            sc2 = jnp.dot(q2, ktp16, preferred_element_type=F32)
            cross2 = jnp.dot(q2, stp16, preferred_element_type=F32)
            new = []
            for e in range(2):
                h = hp * 2 + e
                hs = slice(e * RET_QK_DIM, (e + 1) * RET_QK_DIM)
                hv = slice(h * RET_V_DIM, (h + 1) * RET_V_DIM)
                sc = sc2[e * c:(e + 1) * c] * dec_ref[h]
                vh = rv_ref[rows, hv]
                kd = (ktp[hs] * kdec_ref[h]).astype(BF16)
                both = jnp.dot(jnp.concatenate([sc.astype(BF16), kd], axis=0), vh,
                               preferred_element_type=F32)
                cross = cross2[e * c:(e + 1) * c] * qdec_ref[h]
                r_ref[rows, hv] = _gated_norm(both[:c] + cross, rg_ref[rows, hv]).astype(r_ref.dtype)
                new.append(gc_ref[h] * stp[hs] + both[c:])
            st[hp] = jnp.concatenate(new, axis=0)
    for hp in range(n_pair):
        state[hp] = st[hp]
        st_ref[hp] = st[hp]


def _sret_kernel(rq_ref, rkt_ref, rv_ref, rg_ref, st_in_ref, dec_ref, qdec_ref, kdec_ref, gc_ref,
                 r_ref, st_out_ref, *, t_seq):
    n = LANES
    nb = n // t_seq
    lane = lax.broadcasted_iota(jnp.int32, (n, LANES), 1)
    tok_r = lax.broadcasted_iota(jnp.int32, (nb, n, LANES), 1) // t_seq
    seq_r = lax.broadcasted_iota(jnp.int32, (nb, n, LANES), 0)
    row_in_seq = tok_r == seq_r
    tok_l = lax.broadcasted_iota(jnp.int32, (nb, RET_QK_DIM, n), 2) // t_seq
    seq_l = lax.broadcasted_iota(jnp.int32, (nb, RET_QK_DIM, n), 0)
    lane_in_seq = tok_l == seq_l
    for hp in range(N_RET_HEADS // 2):
        cols = slice(hp * LANES, (hp + 1) * LANES)
        qp = rq_ref[0, :, cols]
        ktp = rkt_ref[0, cols, :]
        ktp16 = ktp.astype(BF16)
        stp = st_in_ref[:, hp]
        st16 = stp.astype(BF16).reshape(nb * LANES, RET_V_DIM)
        for e in range(2):
            h = hp * 2 + e
            hs = slice(e * RET_QK_DIM, (e + 1) * RET_QK_DIM)
            hv = slice(h * RET_V_DIM, (h + 1) * RET_V_DIM)
            in_head = (lane >= e * RET_QK_DIM) & (lane < (e + 1) * RET_QK_DIM)
            qh = jnp.where(in_head, qp, jnp.zeros_like(qp))
            sc = jnp.dot(qh, ktp16, preferred_element_type=F32) * dec_ref[h]
            vh = rv_ref[0, :, hv]
            inner = jnp.dot(sc.astype(BF16), vh, preferred_element_type=F32)
            qbd = jnp.where(row_in_seq, qh[None], jnp.zeros_like(qh)[None])
            qbd = jnp.concatenate([qbd[b] for b in range(nb)], axis=1)
            cross = jnp.dot(qbd, st16, preferred_element_type=F32) * qdec_ref[h]
            r_ref[0, :, hv] = _gated_norm(inner + cross, rg_ref[0, :, hv]).astype(r_ref.dtype)
            kd = (ktp[hs] * kdec_ref[h]).astype(BF16)
            kds = jnp.where(lane_in_seq, kd[None], jnp.zeros_like(kd)[None])
            upd = jnp.dot(kds.reshape(nb * RET_QK_DIM, n), vh, preferred_element_type=F32)
            st_out_ref[:, hp, hs, :] = (gc_ref[h] * stp[:, hs, :]
                                        + upd.reshape(nb, RET_QK_DIM, RET_V_DIM))


def _sample_retention(rq, rkt, rv, rg, state, t_seq):
    n_tok = rq.shape[1]
    db = state.shape[0]
    nb = LANES // t_seq
    n_pair = N_RET_HEADS // 2
    st_pairs = state.reshape(db, n_pair, LANES, RET_V_DIM)
    tables = _retention_tables(t_seq, LANES)
    row = lambda width: pl.BlockSpec((1, LANES, width), lambda j: (0, j, 0))
    st_spec = pl.BlockSpec((nb, n_pair, LANES, RET_V_DIM), lambda j: (j, 0, 0, 0))
    r, st = pl.pallas_call(
        functools.partial(_sret_kernel, t_seq=t_seq),
        grid=(n_tok // LANES,),
        in_specs=[row(RET_QW), pl.BlockSpec((1, RET_QW, LANES), lambda j: (0, 0, j)),
                  row(RET_VW), row(RET_VW), st_spec] + [_const_spec(t.shape) for t in tables],
        out_specs=(row(RET_VW), st_spec),
        out_shape=(jax.ShapeDtypeStruct((1, n_tok, RET_VW), BF16),
                   jax.ShapeDtypeStruct(st_pairs.shape, F32)),
        compiler_params=_params("parallel"),
        name="sample_ret",
    )(rq, rkt, rv, rg, st_pairs, *tables)
    return r, st.reshape(state.shape)


def _sattn_compute(dl_ref, g_ref, qb, kn, vn, k_page, v_page, n_pages, page):
    t_seq = kn.shape[0]
    nh = N_DIFF_HEADS
    half = nh * t_seq
    qb = qb.astype(F32)
    s = jnp.concatenate(
        [jnp.dot(qb, k_page(p), preferred_element_type=F32) for p in range(n_pages)],
        axis=1)
    kn = kn.astype(F32)
    t_row = lax.broadcasted_iota(jnp.int32, (2 * half, 1), 0) % t_seq
    s_new = []
    for tk in range(t_seq):
        col = jnp.sum(qb * kn[tk:tk + 1, :], axis=-1, keepdims=True)
        s_new.append(jnp.where(tk <= t_row, col, MASK_VALUE))
    m = jnp.max(s, axis=-1, keepdims=True)
    for col in s_new:
        m = jnp.maximum(m, col)
    p = jnp.exp2(s - m)
    p_new = [jnp.exp2(col - m) for col in s_new]
    l = jnp.sum(p, axis=-1, keepdims=True)
    for col in p_new:
        l = l + col
    lam = _lambda_full(dl_ref)
    a = (p[:half] / l[:half] - lam * (p[half:] / l[half:])).astype(BF16)
    a_new = [(c[:half] / l[:half] - lam * (c[half:] / l[half:])).astype(BF16).astype(F32)
             for c in p_new]
    a_pages = jnp.concatenate([a[:, pg * page:(pg + 1) * page] for pg in range(n_pages)],
                              axis=0)
    spread = (lax.broadcasted_iota(jnp.int32, (page, page * nh), 1) // nh
              == lax.broadcasted_iota(jnp.int32, (page, page * nh), 0)).astype(BF16)
    ax = jnp.dot(a_pages, spread, preferred_element_type=F32)
    col_head = lax.broadcasted_iota(jnp.int32, ax.shape, 1) % nh
    row_head = (lax.broadcasted_iota(jnp.int32, ax.shape, 0) % half) // t_seq
    ax = jnp.where(col_head == row_head, ax, 0.0)
    acc = jnp.zeros((half, DIFF_V_DIM), F32)
    for pg in range(n_pages):
        acc = acc + jnp.dot(ax[pg * half:(pg + 1) * half], v_page(pg),
                            preferred_element_type=F32)
    vn = vn.astype(F32)
    out = []
    for h in range(nh):
        hv = slice(h * DIFF_V_DIM, (h + 1) * DIFF_V_DIM)
        rows = slice(h * t_seq, (h + 1) * t_seq)
        o = acc[rows]
        for tk in range(t_seq):
            o = o + a_new[tk][rows] * vn[tk:tk + 1, hv]
        out.append(_subln(o, g_ref[...]))
    return jnp.concatenate(out, axis=1)


def _ffn_stages(x_ref, d_ref, r_ref, wo_ref, gf_ref, wi_ref, wo2_ref, gl_ref, y_ref, act, fc):
    d_ff = wo2_ref.shape[0]
    dm = x_ref.shape[1]
    live = {}

    def head():
        mix = (jnp.dot(d_ref[...], wo_ref[0:DIFF_VW, :], preferred_element_type=F32)
               + jnp.dot(r_ref[...], wo_ref[DIFF_VW:, :], preferred_element_type=F32))
        live["x1"] = x_ref[...] + mix
        live["hb"] = (_rms(live["x1"], NORM_EPS) * gf_ref[...]).astype(BF16)

    def chunk(c):
        def run():
            hb = live["hb"]
            g = jnp.dot(hb, wi_ref[:, c * fc:(c + 1) * fc], preferred_element_type=F32)
            u = jnp.dot(hb, wi_ref[:, d_ff + c * fc:d_ff + (c + 1) * fc],
                        preferred_element_type=F32)
            act[:, c * fc:(c + 1) * fc] = (jax.nn.silu(g) * u).astype(BF16)
        return run

    def tail():
        x2 = live["x1"] + jnp.dot(act[...], wo2_ref[...].astype(BF16),
                                  preferred_element_type=F32)
        y_ref[...] = _rms(x2, NORM_EPS) * gl_ref[...]

    return ([(wo_ref.shape[0] * dm, head)]
            + [(2 * dm * fc, chunk(c)) for c in range(d_ff // fc)]
            + [(d_ff * dm, tail)])


def _ffn_kernel(*refs, fc):
    for _, stage in _ffn_stages(*refs, fc):
        stage()


def _merge_ffn(x, d, r, w_out, g_ffn, w_ffn_in, w_ffn_out, g_final, tm, fc):
    n, dm = x.shape
    d_ff = w_ffn_out.shape[0]
    row = lambda width: pl.BlockSpec((tm, width), lambda i: (i, 0))
    return pl.pallas_call(
        functools.partial(_ffn_kernel, fc=fc),
        grid=(n // tm,),
        in_specs=[row(dm), row(DIFF_VW), row(RET_VW), _const_spec(w_out.shape),
                  _const_spec(g_ffn.shape), _const_spec(w_ffn_in.shape),
                  _const_spec(w_ffn_out.shape), _const_spec(g_final.shape)],
        out_specs=row(dm),
        out_shape=jax.ShapeDtypeStruct((n, dm), F32),
        scratch_shapes=[pltpu.VMEM((tm, d_ff), BF16)],
        compiler_params=_params("parallel"),
        name="merge_ffn",
    )(x, d, r, w_out, g_ffn, w_ffn_in, w_ffn_out, g_final)


def _ffn_sattn_kernel(pt_ref, x_ref, d_ref, r_ref, wo_ref, gf_ref, wi_ref, wo2_ref, gl_ref,
                      dl_ref, g_ref, qb_ref, kn_ref, vn_ref, kc_hbm, vc_hbm,
                      y_ref, ds_ref, act, kbuf, vbuf, sem, *, fc, n_seq, n_pages):
    step = pl.program_id(0)
    per_step = qb_ref.shape[0]
    page = kbuf.shape[3]

    def page_copies(seq, slot):
        copies = []
        for p in range(n_pages):
            pg = pt_ref[seq * n_pages + p]
            copies.append(pltpu.make_async_copy(kc_hbm.at[pg], kbuf.at[slot, p], sem.at[slot]))
            copies.append(pltpu.make_async_copy(vc_hbm.at[pg], vbuf.at[slot, p], sem.at[slot]))
        return copies

    def start(seq, slot):
        for c in page_copies(seq, slot):
            c.start()

    def wait(seq, slot):
        for c in page_copies(seq, slot):
            c.wait()

    @pl.when(step == 0)
    def _():
        for t in range(N_PAGE_SLOTS):
            start(t, t)

    stages = _ffn_stages(x_ref, d_ref, r_ref, wo_ref, gf_ref, wi_ref, wo2_ref, gl_ref, y_ref, act, fc)
    total = sum(w for w, _ in stages)
    shares = [[] for _ in range(per_step)]
    done = 0
    for w, stage in stages:
        shares[min(per_step - 1, (2 * done + w) * per_step // (2 * total))].append(stage)
        done += w

    for t in range(per_step):
        seq = step * per_step + t
        slot = t % N_PAGE_SLOTS
        wait(seq, slot)
        o = _sattn_compute(dl_ref, g_ref, qb_ref[t], kn_ref[t], vn_ref[t],
                           lambda p: kbuf[slot, p], lambda p: vbuf[slot, p], n_pages, page)
        ds_ref[t] = o.astype(ds_ref.dtype)
        for stage in shares[t]:
            stage()

        @pl.when(seq + N_PAGE_SLOTS < n_seq)
        def _():
            start(seq + N_PAGE_SLOTS, slot)


def _merge_ffn_sample_attention(x, d, r, w_out, g_ffn, w_ffn_in, w_ffn_out, g_final, tm, fc,
                                page_table, dl, subln_g, qblk, k_new, v_new, kc, vc):
    n, dm = x.shape
    d_ff = w_ffn_out.shape[0]
    db, n_pages = page_table.shape
    t_seq = k_new.shape[1]
    page = kc.shape[2]
    n_steps = n // tm
    assert db % n_steps == 0 and (db // n_steps) % N_PAGE_SLOTS == 0, "sequences per FFN step"
    per_step = db // n_steps
    row = lambda width: pl.BlockSpec((tm, width), lambda i, pt: (i, 0))
    const = lambda a: pl.BlockSpec(a.shape, lambda i, pt: (0,) * a.ndim,
                                   pipeline_mode=pl.Buffered(1))
    per_seq = lambda a: pl.BlockSpec((per_step,) + a.shape[1:], lambda i, pt: (i, 0, 0))
    hbm = pl.BlockSpec(memory_space=pl.ANY)
    grid_spec = pltpu.PrefetchScalarGridSpec(
        num_scalar_prefetch=1,
        grid=(n_steps,),
        in_specs=[row(dm), row(DIFF_VW), row(RET_VW), const(w_out), const(g_ffn),
                  const(w_ffn_in), const(w_ffn_out), const(g_final), const(dl), const(subln_g),
                  per_seq(qblk), per_seq(k_new), per_seq(v_new), hbm, hbm],
        out_specs=(row(dm), pl.BlockSpec((per_step, t_seq, DIFF_VW), lambda i, pt: (i, 0, 0))),
        scratch_shapes=[pltpu.VMEM((tm, d_ff), BF16),
                        pltpu.VMEM((N_PAGE_SLOTS, n_pages, DIFF_W, page), F32),
                        pltpu.VMEM((N_PAGE_SLOTS, n_pages, page * N_DIFF_HEADS, DIFF_V_DIM), F32),
                        pltpu.SemaphoreType.DMA((N_PAGE_SLOTS,))],
    )
    return pl.pallas_call(
        functools.partial(_ffn_sattn_kernel, fc=fc, n_seq=db, n_pages=n_pages),
        grid_spec=grid_spec,
        out_shape=(jax.ShapeDtypeStruct((n, dm), F32),
                   jax.ShapeDtypeStruct((db, t_seq, DIFF_VW), BF16)),
        compiler_params=_params("arbitrary"),
        name="merge_ffn_sample_attn",
    )(page_table.reshape(-1), x, d, r, w_out, g_ffn, w_ffn_in, w_ffn_out, g_final,
      dl, subln_g, qblk, k_new, v_new, kc, vc)


def _ffn_chunk(d_ff):
    for fc in (512, 256, 128):
        if d_ff % fc == 0:
            return fc
    return d_ff


def kernel(x_prompt, x_sample, cache_diff_k, cache_diff_v, state_ret, page_table, norm_mix_g, w_in, diff_lambda, diff_subln_g, w_out, norm_ffn_g, w_ffn_in, w_ffn_out, norm_final_g):
    bsz, seq, dm = x_prompt.shape
    db, t_seq, _ = x_sample.shape
    n_pages = page_table.shape[1]
    page = cache_diff_k.shape[2]
    past = n_pages * page
    assert w_in.shape[0] == 1, "single layer"
    assert LANES % t_seq == 0 and (db * t_seq) % LANES == 0

    w_in0 = w_in[0]
    w_out16 = w_out[0].astype(BF16)
    w_ffn_in16 = w_ffn_in[0].astype(BF16)
    w_ffn_out16 = w_ffn_out[0]
    g_mix, g_ffn = norm_mix_g, norm_ffn_g
    g_final = norm_final_g.reshape(1, dm)
    dl = diff_lambda[0]
    fc = _ffn_chunk(w_ffn_out.shape[1])

    tm = min(512, seq)
    tabs_p = _rotary_tables(jnp.arange(seq))
    qt16, k16, kt32, v32, vt16, r_p, ret_state_p = _inproj(
        x_prompt, g_mix, w_in0, tabs_p, min(1024, seq), with_retention=True)
    d_p = _prompt_attention(dl, diff_subln_g, qt16, k16, vt16, min(1024, seq), min(512, seq))
    k_prompt = jnp.swapaxes(kt32, 1, 2).reshape(1, bsz, seq, N_DIFF_HEADS, 2, DIFF_QK_DIM)
    v_prompt = v32.reshape(1, bsz, seq, N_DIFF_HEADS, DIFF_V_DIM)
    ret_prompt = ret_state_p.reshape(1, bsz, N_RET_HEADS, RET_QK_DIM, RET_V_DIM)

    n_s = db * t_seq
    pos_s = past + (jnp.arange(n_s) % t_seq)
    tabs_s = _rotary_tables(pos_s)
    tm_s = min(512, n_s)
    qts, ks16, kts32, vs32, _, rqs, rkts, rvs, rgs = _inproj(
        x_sample.reshape(1, n_s, dm), g_mix, w_in0, tabs_s, tm_s, with_retention=False)
    k_s = kts32[0].T
    qs = qts[0].T
    r_idx = np.arange(2 * N_DIFF_HEADS * t_seq)
    r_map, r_head, r_tok = r_idx // (N_DIFF_HEADS * t_seq), (r_idx // t_seq) % N_DIFF_HEADS, r_idx % t_seq
    feat_owner = np.arange(DIFF_W) // DIFF_QK_DIM
    sel = (feat_owner[None, :] == (r_head * 2 + r_map)[:, None])
    qs3 = qs.reshape(db, t_seq, DIFF_W)
    qblk = jnp.where(sel[None], qs3[:, r_tok, :], jnp.zeros((), BF16))
    kc = jnp.transpose(cache_diff_k[0], (0, 2, 3, 4, 1)).reshape(-1, DIFF_W, page)
    y_prompt, d_s = _merge_ffn_sample_attention(
        x_prompt.reshape(bsz * seq, dm), d_p.reshape(bsz * seq, DIFF_VW),
        r_p.reshape(bsz * seq, RET_VW), w_out16, g_ffn, w_ffn_in16, w_ffn_out16, g_final, tm, fc,
        page_table, dl, diff_subln_g, qblk, ks16.reshape(db, t_seq, DIFF_W),
        vs32.astype(BF16).reshape(db, t_seq, DIFF_VW), kc,
        cache_diff_v[0].reshape(-1, page * N_DIFF_HEADS, DIFF_V_DIM))
    y_prompt = y_prompt.reshape(bsz, seq, dm)
    r_s, ret_state_s = _sample_retention(rqs, rkts, rvs, rgs, state_ret[0], t_seq)
    y_sample = _merge_ffn(x_sample.reshape(n_s, dm), d_s.reshape(n_s, DIFF_VW),
                          r_s.reshape(n_s, RET_VW), w_out16, g_ffn, w_ffn_in16,
                          w_ffn_out16, g_final, tm_s, fc).reshape(db, t_seq, dm)
    k_sample = k_s.reshape(1, db, t_seq, N_DIFF_HEADS, 2, DIFF_QK_DIM)
    v_sample = vs32.reshape(1, db, t_seq, N_DIFF_HEADS, DIFF_V_DIM)
    ret_sample = ret_state_s[None]

    return (y_prompt, y_sample, k_prompt, v_prompt, ret_prompt, k_sample, v_sample, ret_sample)
```

```python
import functools
import math

import jax
import jax.numpy as jnp
import numpy as np
from jax import lax
from jax.experimental import pallas as pl
from jax.experimental.pallas import tpu as pltpu

F32 = jnp.float32
BF16 = jnp.bfloat16

N_DIFF_HEADS = 4
DIFF_QK_DIM = 64
DIFF_V_DIM = 128
ROT_DIM = 16
ROPE_THETA = 500000.0
N_RET_HEADS = 4
RET_QK_DIM = 64
RET_V_DIM = 128
RET_THETA = 10000.0
RET_CHUNK = 128
NORM_EPS = 1e-6
SUBLN_EPS = 1e-5
LAM_INIT = 0.8 - 0.6 * math.exp(-0.3 * 0)
MASK_VALUE = -1e30
Q_SCALE = DIFF_QK_DIM ** -0.5 * math.log2(math.e)

DIFF_W = N_DIFF_HEADS * 2 * DIFF_QK_DIM
DIFF_VW = N_DIFF_HEADS * DIFF_V_DIM
RET_QW = N_RET_HEADS * RET_QK_DIM
RET_VW = N_RET_HEADS * RET_V_DIM
W_IN_OFFSETS = tuple(int(o) for o in np.cumsum([0, DIFF_W, DIFF_W, DIFF_VW, RET_QW, RET_QW, RET_VW]))

LANES = 128
ACC_PAD = 16
N_PAGE_SLOTS = 2
VMEM_LIMIT = 56 * 1024 * 1024


def _params(*sem, flags=None):
    return pltpu.CompilerParams(dimension_semantics=sem, vmem_limit_bytes=VMEM_LIMIT, flags=flags)


def _const_spec(shape):
    nd = len(shape)
    return pl.BlockSpec(shape, lambda *_: (0,) * nd, pipeline_mode=pl.Buffered(1))


def _rms(x, eps):
    return x * lax.rsqrt(jnp.mean(x * x, axis=-1, keepdims=True) + eps)


def _inv_freq(freq_idx, dim, theta):
    return 1.0 / (jnp.float32(theta) ** (jnp.asarray(2 * freq_idx, F32) / dim))


def _rotary_tables(pos):
    posf = pos.astype(F32)

    def feature_tables(half, dim, theta, scale):
        ang = _inv_freq(np.arange(half), dim, theta)[:, None] * posf[None, :]
        return jnp.cos(ang) * scale, jnp.sin(ang) * scale

    return (feature_tables(ROT_DIM // 2, ROT_DIM, ROPE_THETA, Q_SCALE)
            + feature_tables(ROT_DIM // 2, ROT_DIM, ROPE_THETA, 1.0)
            + feature_tables(RET_QK_DIM // 2, RET_QK_DIM, RET_THETA, 1.0)
            + feature_tables(RET_QK_DIM // 2, RET_QK_DIM, RET_THETA, RET_QK_DIM ** -0.5))


def _log_gamma():
    return jnp.log(1.0 - 2.0 ** (-5.0 - jnp.arange(N_RET_HEADS, dtype=F32)))


def _retention_tables(chunk, n_tok):
    lg = _log_gamma()
    idx = jnp.arange(n_tok)
    loc = (idx % chunk).astype(F32)
    rel = loc[:, None] - loc[None, :]
    same = (idx[:, None] // chunk) == (idx[None, :] // chunk)
    decay = jnp.where(same[None] & (rel >= 0)[None],
                      jnp.exp(lg[:, None, None] * jnp.maximum(rel, 0.0)[None]), 0.0)
    qdec = jnp.exp(lg[:, None] * (loc[None, :] + 1.0))[:, :, None]
    kdec = jnp.exp(lg[:, None] * (chunk - 1.0 - loc[None, :]))[:, None, :]
    gc = jnp.broadcast_to(jnp.exp(lg * chunk)[:, None, None], (N_RET_HEADS, 1, LANES))
    return decay, qdec, kdec, gc


def _inproj_kernel(x_ref, g_ref, w_ref,
                   cqt_ref, sqt_ref, ckt_ref, skt_ref, crqt_ref, srqt_ref, crkt_ref, srkt_ref,
                   *refs, with_retention):
    if with_retention:
        (dec_ref, qdec_ref, kdec_ref, gc_ref,
         qt_ref, k16_ref, kt32_ref, v32_ref, vt16_ref, r_ref, st_ref,
         rqt_scr, rq_dst, rkt_dst, rv_dst, rg_dst, state) = refs
    else:
        (qt_ref, k16_ref, kt32_ref, v32_ref, vt16_ref, rq_ref, rkt_ref, rv_ref, rg_ref,
         rqt_scr) = refs
        rq_dst, rkt_dst, rv_dst, rg_dst = rq_ref.at[0], rkt_ref.at[0], rv_ref.at[0], rg_ref.at[0]
    x = x_ref[0]
    hb = (_rms(x, NORM_EPS) * g_ref[...]).astype(BF16)

    def mm(lo, width):
        return jnp.dot(hb, w_ref[:, lo:lo + width].astype(BF16), preferred_element_type=F32)

    def rot_t(zt, base, width, half, c, s, rest_scale=None):
        x1, x2 = zt[base:base + half], zt[base + half:base + 2 * half]
        parts = [x1 * c - x2 * s, x2 * c + x1 * s]
        if 2 * half < width:
            rest = zt[base + 2 * half:base + width]
            parts.append(rest if rest_scale is None else rest * rest_scale)
        return jnp.concatenate(parts, axis=0)

    o_dq, o_dk, o_dv, o_rq, o_rk, o_rv, o_rg = W_IN_OFFSETS

    def retention_group():
        zt = mm(o_rq, RET_QW).T
        c, s = crqt_ref[...], srqt_ref[...]
        for h in range(N_RET_HEADS):
            b = h * RET_QK_DIM
            rqt_scr[b:b + RET_QK_DIM, :] = rot_t(zt, b, RET_QK_DIM, RET_QK_DIM // 2, c, s)
        rq_dst[...] = rqt_scr[...].T.astype(BF16)
        zt = mm(o_rk, RET_QW).T
        c, s = crkt_ref[...], srkt_ref[...]
        for h in range(N_RET_HEADS):
            b = h * RET_QK_DIM
            rkt_dst[b:b + RET_QK_DIM, :] = rot_t(zt, b, RET_QK_DIM, RET_QK_DIM // 2, c, s)
        rv_dst[...] = mm(o_rv, RET_VW).astype(BF16)
        rg_dst[...] = mm(o_rg, RET_VW)

    def diff_group():
        hr = ROT_DIM // 2
        zt = mm(o_dq, DIFF_W).T
        c, s = cqt_ref[...], sqt_ref[...]
        for g in range(N_DIFF_HEADS * 2):
            b = g * DIFF_QK_DIM
            qt_ref[0, b:b + DIFF_QK_DIM, :] = rot_t(
                zt, b, DIFF_QK_DIM, hr, c, s, rest_scale=Q_SCALE).astype(BF16)
        zt = mm(o_dk, DIFF_W).T
        c, s = ckt_ref[...], skt_ref[...]
        for g in range(N_DIFF_HEADS * 2):
            b = g * DIFF_QK_DIM
            kt32_ref[0, b:b + DIFF_QK_DIM, :] = rot_t(zt, b, DIFF_QK_DIM, hr, c, s)
        k16_ref[0] = kt32_ref[0].T.astype(BF16)
        zv = mm(o_dv, DIFF_VW)
        for h in range(N_DIFF_HEADS):
            v32_ref[0, pl.ds(h, zv.shape[0], stride=N_DIFF_HEADS), :] = (
                zv[:, h * DIFF_V_DIM:(h + 1) * DIFF_V_DIM])
        vt16_ref[0] = zv.T.astype(BF16)

    retention_group()
    if with_retention:
        _retention_step(rq_dst, rkt_dst, rv_dst, rg_dst, dec_ref, qdec_ref, kdec_ref, gc_ref,
                        r_ref.at[0], st_ref.at[0], state, pl.program_id(1))
    diff_group()


def _inproj(x, g, w_in, tables, tm, with_retention):
    b, s, d = x.shape
    row = lambda width: pl.BlockSpec((1, tm, width), lambda bi, i: (bi, i, 0))
    col = lambda height: pl.BlockSpec((1, height, tm), lambda bi, i: (bi, 0, i))
    ttab = lambda height: pl.BlockSpec((height, tm), lambda bi, i: (0, i))
    n_pair = N_RET_HEADS // 2
    out_shape = [
        jax.ShapeDtypeStruct((b, DIFF_W, s), BF16),
        jax.ShapeDtypeStruct((b, s, DIFF_W), BF16),
        jax.ShapeDtypeStruct((b, DIFF_W, s), F32),
        jax.ShapeDtypeStruct((b, s * N_DIFF_HEADS, DIFF_V_DIM), F32),
        jax.ShapeDtypeStruct((b, DIFF_VW, s), BF16),
    ]
    v_rows = pl.BlockSpec((1, tm * N_DIFF_HEADS, DIFF_V_DIM), lambda bi, i: (bi, i, 0))
    out_specs = [col(DIFF_W), row(DIFF_W), col(DIFF_W), v_rows, col(DIFF_VW)]
    in_specs = ([row(d), _const_spec((1, d)), _const_spec(w_in.shape)]
                + [ttab(ROT_DIM // 2)] * 4 + [ttab(RET_QK_DIM // 2)] * 4)
    scratch = [pltpu.VMEM((RET_QW, tm), F32)]
    operands = [x, g, w_in, *tables]
    ret_inputs = [((tm, RET_QW), BF16), ((RET_QW, tm), F32), ((tm, RET_VW), BF16), ((tm, RET_VW), F32)]
    if with_retention:
        ret_tables = _retention_tables(RET_CHUNK, RET_CHUNK)
        in_specs += [_const_spec(t.shape) for t in ret_tables]
        operands += list(ret_tables)
        out_shape += [jax.ShapeDtypeStruct((b, s, RET_VW), BF16),
                      jax.ShapeDtypeStruct((b, n_pair, LANES, RET_V_DIM), F32)]
        out_specs += [row(RET_VW),
                      pl.BlockSpec((1, n_pair, LANES, RET_V_DIM), lambda bi, i: (bi, 0, 0, 0))]
        scratch += [pltpu.VMEM(shape, dt) for shape, dt in ret_inputs]
        scratch += [pltpu.VMEM((n_pair, LANES, RET_V_DIM), F32)]
    else:
        out_shape += [jax.ShapeDtypeStruct((b, s, RET_QW), BF16),
                      jax.ShapeDtypeStruct((b, RET_QW, s), F32),
                      jax.ShapeDtypeStruct((b, s, RET_VW), BF16),
                      jax.ShapeDtypeStruct((b, s, RET_VW), F32)]
        out_specs += [row(RET_QW), col(RET_QW), row(RET_VW), row(RET_VW)]
    return pl.pallas_call(
        functools.partial(_inproj_kernel, with_retention=with_retention),
        grid=(b, s // tm),
        in_specs=in_specs,
        out_specs=out_specs,
        out_shape=out_shape,
        scratch_shapes=scratch,
        compiler_params=_params("parallel", "arbitrary" if with_retention else "parallel"),
        name="inproj_ret" if with_retention else "inproj",
    )(*operands)


def _lambda_full(dl_ref):
    lp = dl_ref[...]
    a = jnp.sum(lp[0:1] * lp[1:2], axis=-1, keepdims=True)
    b = jnp.sum(lp[2:3] * lp[3:4], axis=-1, keepdims=True)
    return jnp.exp(a) - jnp.exp(b) + LAM_INIT


def _subln(o, g):
    return _rms(o, SUBLN_EPS) * g * (1.0 - LAM_INIT)


def _attn_kernel(dl_ref, g_ref, qt_ref, qt_next_ref, k_ref, vt_ref, o_ref, acc, s_scr, p_scr,
                 *, tq, tk):
    i = pl.program_id(2)
    first_step = (pl.program_id(0) == 0) & (pl.program_id(1) == 0) & (i == 0)

    def score_columns(qt):
        row = lax.broadcasted_iota(jnp.int32, (qt.shape[0], tk), 0)
        groups = []
        for half in range(2):
            qh = qt[:, half * tk:(half + 1) * tk]
            groups += [jnp.where(row < DIFF_QK_DIM, qh, jnp.zeros_like(qh)),
                       jnp.where(row >= DIFF_QK_DIM, qh, jnp.zeros_like(qh))]
        return jnp.concatenate(groups, axis=1)

    qtb = score_columns(qt_ref[0])
    n_full = 2 * i
    all_cols, lo, hi = slice(0, 2 * tq), slice(0, tq), slice(tq, 2 * tq)

    def issue_scores(j, par, cols, q=None):
        off = pl.multiple_of(j * tk, tk)
        q = qtb if q is None else q
        s_scr[par, :, cols] = jnp.dot(k_ref[0, pl.ds(off, tk), :], q[:, cols],
                                      preferred_element_type=F32)

    def softmax_stage(par, m, cols, causal):
        s = s_scr[par, :, cols]
        if causal:
            c = lax.broadcasted_iota(jnp.int32, (tk, tq), 1)
            c = jnp.where(c >= tk, c - tk, c)
            s = jnp.where(lax.broadcasted_iota(jnp.int32, (tk, tq), 0) <= c, s, MASK_VALUE)
        m_new = jnp.maximum(m, jnp.max(s, axis=0, keepdims=True))
        alpha = jnp.exp2(m - m_new)
        p_scr[par, :, cols] = jnp.exp2(s - m_new).astype(BF16)
        return m_new, alpha

    ones_rows = (lax.broadcasted_iota(jnp.int32, (ACC_PAD, tk), 0) == 0).astype(BF16)

    def value_stage(j, par, alpha, cols):
        off = pl.multiple_of(j * tk, tk)
        vt = jnp.concatenate([vt_ref[0, :, pl.ds(off, tk)], ones_rows], axis=0)
        acc[:, cols] = alpha * acc[:, cols] + jnp.dot(vt, p_scr[par, :, cols],
                                                      preferred_element_type=F32)

    def substep(j, par, m, alpha_prev):
        issue_scores(j + 1, 1 - par, all_cols)
        value_stage(jnp.maximum(j - 1, 0), 1 - par, alpha_prev, all_cols)
        return softmax_stage(par, m, all_cols, False)

    def body(t, carry):
        m, alpha = substep(2 * t, 0, *carry)
        return substep(2 * t + 1, 1, m, alpha)

    @pl.when(i == 0)
    def _():
        issue_scores(0, 0, all_cols)

    @pl.when(first_step)
    def _():
        acc[...] = jnp.zeros_like(acc)
        p_scr[1] = jnp.zeros((tk, 2 * tq), BF16)

    carry = (jnp.full((1, 2 * tq), MASK_VALUE, F32), jnp.ones((1, 2 * tq), F32))
    m, alpha = lax.fori_loop(0, i, body, carry)
    issue_scores(n_full + 1, 1, hi)
    value_stage(jnp.maximum(n_full - 1, 0), 1, alpha, all_cols)
    _, alpha_lo = softmax_stage(0, m[:, lo], lo, True)
    m_hi, alpha_hi = softmax_stage(0, m[:, hi], hi, False)
    value_stage(n_full, 0, jnp.concatenate([alpha_lo, alpha_hi], axis=1), all_cols)
    _, alpha_hi = softmax_stage(1, m_hi, hi, True)
    value_stage(n_full + 1, 1, alpha_hi, hi)
    issue_scores(0, 0, all_cols, q=score_columns(qt_next_ref[0]))

    lam = _lambda_full(dl_ref)
    out = []
    for half in range(2):
        c0 = slice(2 * half * tk, (2 * half + 1) * tk)
        c1 = slice((2 * half + 1) * tk, (2 * half + 2) * tk)
        out.append(acc[0:DIFF_V_DIM, c0] / acc[DIFF_V_DIM:DIFF_V_DIM + 1, c0]
                   - lam * (acc[0:DIFF_V_DIM, c1] / acc[DIFF_V_DIM:DIFF_V_DIM + 1, c1]))
    ot = jnp.concatenate(out, axis=1)
    o_ref[0] = _subln(ot.T, g_ref[...]).astype(o_ref.dtype)


def _prompt_attention(dl, subln_g, qt16, k16, vt16, tq, tk):
    b, s, _ = k16.shape
    assert tq == 2 * tk, "a query block spans two key blocks (two scratch slots)"
    n_q = s // tq
    kernel = functools.partial(_attn_kernel, tq=tq, tk=tk)
    return pl.pallas_call(
        kernel,
        grid=(b, N_DIFF_HEADS, s // tq),
        in_specs=[
            _const_spec(dl.shape),
            _const_spec(subln_g.shape),
            pl.BlockSpec((1, LANES, tq), lambda bi, h, i: (bi, h, i)),
            pl.BlockSpec((1, LANES, tq), lambda bi, h, i: (bi, h, jnp.minimum(i + 1, n_q - 1))),
            pl.BlockSpec((1, s, LANES), lambda bi, h, i: (bi, 0, h)),
            pl.BlockSpec((1, LANES, s), lambda bi, h, i: (bi, h, 0)),
        ],
        out_specs=pl.BlockSpec((1, tq, LANES), lambda bi, h, i: (bi, i, h)),
        out_shape=jax.ShapeDtypeStruct((b, s, DIFF_VW), BF16),
        scratch_shapes=[pltpu.VMEM((DIFF_V_DIM + ACC_PAD, 2 * tq), F32),
                        pltpu.VMEM((2, tk, 2 * tq), F32),
                        pltpu.VMEM((2, tk, 2 * tq), BF16)],
        compiler_params=_params("arbitrary", "arbitrary", "arbitrary"),
        name="prompt_attn",
    )(dl, subln_g, qt16, qt16, k16, vt16)


def _gated_norm(o, gate):
    return jax.nn.silu(gate) * _rms(o, NORM_EPS)


def _retention_step(rq_ref, rkt_ref, rv_ref, rg_ref, dec_ref, qdec_ref, kdec_ref, gc_ref,
                    r_ref, st_ref, state, j):
    n_chunks = rq_ref.shape[0] // RET_CHUNK
    c = RET_CHUNK
    lane = lax.broadcasted_iota(jnp.int32, (c, LANES), 1)
    n_pair = N_RET_HEADS // 2
    st = [jnp.where(j == 0, 0.0, state[hp]) for hp in range(n_pair)]
    for ci in range(n_chunks):
        rows = slice(ci * c, (ci + 1) * c)
        for hp in range(n_pair):
            cols = slice(hp * LANES, (hp + 1) * LANES)
            qp = rq_ref[rows, cols]
            ktp = rkt_ref[cols, rows]
            ktp16 = ktp.astype(BF16)
            stp = st[hp]
            stp16 = stp.astype(BF16)
            q2 = jnp.concatenate(
                [jnp.where((lane >= e * RET_QK_DIM) & (lane < (e + 1) * RET_QK_DIM),
                           qp, jnp.zeros_like(qp)) for e in range(2)], axis=0)
            sc2 = jnp.dot(q2, ktp16, preferred_element_type=F32)
            cross2 = jnp.dot(q2, stp16, preferred_element_type=F32)
            new = []
            for e in range(2):
                h = hp * 2 + e
                hs = slice(e * RET_QK_DIM, (e + 1) * RET_QK_DIM)
                hv = slice(h * RET_V_DIM, (h + 1) * RET_V_DIM)
                sc = sc2[e * c:(e + 1) * c] * dec_ref[h]
                vh = rv_ref[rows, hv]
                kd = (ktp[hs] * kdec_ref[h]).astype(BF16)
                both = jnp.dot(jnp.concatenate([sc.astype(BF16), kd], axis=0), vh,
                               preferred_element_type=F32)
                cross = cross2[e * c:(e + 1) * c] * qdec_ref[h]
                r_ref[rows, hv] = _gated_norm(both[:c] + cross, rg_ref[rows, hv]).astype(r_ref.dtype)
                new.append(gc_ref[h] * stp[hs] + both[c:])
            st[hp] = jnp.concatenate(new, axis=0)
    for hp in range(n_pair):
        state[hp] = st[hp]
        st_ref[hp] = st[hp]


def _sret_kernel(rq_ref, rkt_ref, rv_ref, rg_ref, st_in_ref, dec_ref, qdec_ref, kdec_ref, gc_ref,
                 r_ref, st_out_ref, *, t_seq):
    n = LANES
    nb = n // t_seq
    lane = lax.broadcasted_iota(jnp.int32, (n, LANES), 1)
    tok_r = lax.broadcasted_iota(jnp.int32, (nb, n, LANES), 1) // t_seq
    seq_r = lax.broadcasted_iota(jnp.int32, (nb, n, LANES), 0)
    row_in_seq = tok_r == seq_r
    tok_l = lax.broadcasted_iota(jnp.int32, (nb, RET_QK_DIM, n), 2) // t_seq
    seq_l = lax.broadcasted_iota(jnp.int32, (nb, RET_QK_DIM, n), 0)
    lane_in_seq = tok_l == seq_l
    for hp in range(N_RET_HEADS // 2):
        cols = slice(hp * LANES, (hp + 1) * LANES)
        qp = rq_ref[0, :, cols]
        ktp = rkt_ref[0, cols, :]
        ktp16 = ktp.astype(BF16)
        stp = st_in_ref[:, hp]
        st16 = stp.astype(BF16).reshape(nb * LANES, RET_V_DIM)
        for e in range(2):
            h = hp * 2 + e
            hs = slice(e * RET_QK_DIM, (e + 1) * RET_QK_DIM)
            hv = slice(h * RET_V_DIM, (h + 1) * RET_V_DIM)
            in_head = (lane >= e * RET_QK_DIM) & (lane < (e + 1) * RET_QK_DIM)
            qh = jnp.where(in_head, qp, jnp.zeros_like(qp))
            sc = jnp.dot(qh, ktp16, preferred_element_type=F32) * dec_ref[h]
            vh = rv_ref[0, :, hv]
            inner = jnp.dot(sc.astype(BF16), vh, preferred_element_type=F32)
            qbd = jnp.where(row_in_seq, qh[None], jnp.zeros_like(qh)[None])
            qbd = jnp.concatenate([qbd[b] for b in range(nb)], axis=1)
            cross = jnp.dot(qbd, st16, preferred_element_type=F32) * qdec_ref[h]
            r_ref[0, :, hv] = _gated_norm(inner + cross, rg_ref[0, :, hv]).astype(r_ref.dtype)
            kd = (ktp[hs] * kdec_ref[h]).astype(BF16)
            kds = jnp.where(lane_in_seq, kd[None], jnp.zeros_like(kd)[None])
            upd = jnp.dot(kds.reshape(nb * RET_QK_DIM, n), vh, preferred_element_type=F32)
            st_out_ref[:, hp, hs, :] = (gc_ref[h] * stp[:, hs, :]
                                        + upd.reshape(nb, RET_QK_DIM, RET_V_DIM))


def _sample_retention(rq, rkt, rv, rg, state, t_seq):
    n_tok = rq.shape[1]
    db = state.shape[0]
    nb = LANES // t_seq
    n_pair = N_RET_HEADS // 2
    st_pairs = state.reshape(db, n_pair, LANES, RET_V_DIM)
    tables = _retention_tables(t_seq, LANES)
    row = lambda width: pl.BlockSpec((1, LANES, width), lambda j: (0, j, 0))
    st_spec = pl.BlockSpec((nb, n_pair, LANES, RET_V_DIM), lambda j: (j, 0, 0, 0))
    r, st = pl.pallas_call(
        functools.partial(_sret_kernel, t_seq=t_seq),
        grid=(n_tok // LANES,),
        in_specs=[row(RET_QW), pl.BlockSpec((1, RET_QW, LANES), lambda j: (0, 0, j)),
                  row(RET_VW), row(RET_VW), st_spec] + [_const_spec(t.shape) for t in tables],
        out_specs=(row(RET_VW), st_spec),
        out_shape=(jax.ShapeDtypeStruct((1, n_tok, RET_VW), BF16),
                   jax.ShapeDtypeStruct(st_pairs.shape, F32)),
        compiler_params=_params("parallel"),
        name="sample_ret",
    )(rq, rkt, rv, rg, st_pairs, *tables)
    return r, st.reshape(state.shape)


def _sattn_compute(dl_ref, g_ref, qb, kn, vn, k_page, v_page, n_pages, page):
    t_seq = kn.shape[0]
    nh = N_DIFF_HEADS
    half = nh * t_seq
    if qb.shape[0] == t_seq:
        rows = jnp.concatenate([qb.astype(F32)] * (2 * nh), axis=0)
        group = lax.broadcasted_iota(jnp.int32, rows.shape, 0) // t_seq
        owner = (group % nh) * 2 + group // nh
        feat = lax.broadcasted_iota(jnp.int32, rows.shape, 1) // DIFF_QK_DIM
        qb = jnp.where(feat == owner, rows, 0.0)
    qb = qb.astype(F32)
    s = jnp.concatenate(
        [jnp.dot(qb, k_page(p), preferred_element_type=F32) for p in range(n_pages)],
        axis=1)
    kn = kn.astype(F32)
    t_row = lax.broadcasted_iota(jnp.int32, (2 * half, 1), 0) % t_seq
    s_new = []
    for tk in range(t_seq):
        col = jnp.sum(qb * kn[tk:tk + 1, :], axis=-1, keepdims=True)
        s_new.append(jnp.where(tk <= t_row, col, MASK_VALUE))
    m = jnp.max(s, axis=-1, keepdims=True)
    for col in s_new:
        m = jnp.maximum(m, col)
    p = jnp.exp2(s - m)
    p_new = [jnp.exp2(col - m) for col in s_new]
    l = jnp.sum(p, axis=-1, keepdims=True)
    for col in p_new:
        l = l + col
    lam = _lambda_full(dl_ref)
    a = (p[:half] / l[:half] - lam * (p[half:] / l[half:])).astype(BF16)
    a_new = [(c[:half] / l[:half] - lam * (c[half:] / l[half:])).astype(BF16).astype(F32)
             for c in p_new]
    a_pages = jnp.concatenate([a[:, pg * page:(pg + 1) * page] for pg in range(n_pages)],
                              axis=0)
    spread = (lax.broadcasted_iota(jnp.int32, (page, page * nh), 1) // nh
              == lax.broadcasted_iota(jnp.int32, (page, page * nh), 0)).astype(BF16)
    ax = jnp.dot(a_pages, spread, preferred_element_type=F32)
    col_head = lax.broadcasted_iota(jnp.int32, ax.shape, 1) % nh
    row_head = (lax.broadcasted_iota(jnp.int32, ax.shape, 0) % half) // t_seq
    ax = jnp.where(col_head == row_head, ax, 0.0)
    acc = jnp.zeros((half, DIFF_V_DIM), F32)
    for pg in range(n_pages):
        acc = acc + jnp.dot(ax[pg * half:(pg + 1) * half], v_page(pg),
                            preferred_element_type=F32)
    vn = vn.astype(F32)
    out = []
    for h in range(nh):
        hv = slice(h * DIFF_V_DIM, (h + 1) * DIFF_V_DIM)
        rows = slice(h * t_seq, (h + 1) * t_seq)
        o = acc[rows]
        for tk in range(t_seq):
            o = o + a_new[tk][rows] * vn[tk:tk + 1, hv]
        out.append(_subln(o, g_ref[...]))
    return jnp.concatenate(out, axis=1)


def _ffn_stages(x_ref, d_ref, r_ref, wo_ref, gf_ref, wi_ref, wo2_ref, gl_ref, y_ref, act, fc):
    d_ff = wo2_ref.shape[0]
    dm = x_ref.shape[1]
    live = {}

    def head():
        mix = (jnp.dot(d_ref[...], wo_ref[0:DIFF_VW, :], preferred_element_type=F32)
               + jnp.dot(r_ref[...], wo_ref[DIFF_VW:, :], preferred_element_type=F32))
        live["x1"] = x_ref[...] + mix
        live["hb"] = (_rms(live["x1"], NORM_EPS) * gf_ref[...]).astype(BF16)

    def chunk(c):
        def run():
            hb = live["hb"]
            g = jnp.dot(hb, wi_ref[:, c * fc:(c + 1) * fc], preferred_element_type=F32)
            u = jnp.dot(hb, wi_ref[:, d_ff + c * fc:d_ff + (c + 1) * fc],
                        preferred_element_type=F32)
            act[:, c * fc:(c + 1) * fc] = (jax.nn.silu(g) * u).astype(BF16)
        return run

    def tail():
        x2 = live["x1"] + jnp.dot(act[...], wo2_ref[...], preferred_element_type=F32)
        y_ref[...] = _rms(x2, NORM_EPS) * gl_ref[...]

    return ([(wo_ref.shape[0] * dm, head)]
            + [(2 * dm * fc, chunk(c)) for c in range(d_ff // fc)]
            + [(d_ff * dm, tail)])


def _ffn_kernel(*refs, fc):
    for _, stage in _ffn_stages(*refs, fc):
        stage()


def _merge_ffn(x, d, r, w_out, g_ffn, w_ffn_in, w_ffn_out, g_final, tm, fc):
    n, dm = x.shape
    d_ff = w_ffn_out.shape[0]
    row = lambda width: pl.BlockSpec((tm, width), lambda i: (i, 0))
    return pl.pallas_call(
        functools.partial(_ffn_kernel, fc=fc),
        grid=(n // tm,),
        in_specs=[row(dm), row(DIFF_VW), row(RET_VW), _const_spec(w_out.shape),
                  _const_spec(g_ffn.shape), _const_spec(w_ffn_in.shape),
                  _const_spec(w_ffn_out.shape), _const_spec(g_final.shape)],
        out_specs=row(dm),
        out_shape=jax.ShapeDtypeStruct((n, dm), F32),
        scratch_shapes=[pltpu.VMEM((tm, d_ff), BF16)],
        compiler_params=_params("parallel"),
        name="merge_ffn",
    )(x, d, r, w_out, g_ffn, w_ffn_in, w_ffn_out, g_final)


def _ffn_sattn_kernel(pt_ref, x_ref, d_ref, r_ref, wo_ref, gf_ref, wi_ref, wo2_ref, gl_ref,
                      dl_ref, g_ref, qb_ref, kn_ref, vn_ref, kc_hbm, vc_hbm,
                      y_ref, ds_ref, act, kbuf, vbuf, sem, *, fc, n_seq, n_pages):
    step = pl.program_id(0)
    per_step = qb_ref.shape[0]
    page = kbuf.shape[3]

    def page_copies(seq, slot):
        copies = []
        for p in range(n_pages):
            pg = pt_ref[seq * n_pages + p]
            copies.append(pltpu.make_async_copy(kc_hbm.at[pg], kbuf.at[slot, p], sem.at[slot]))
            copies.append(pltpu.make_async_copy(vc_hbm.at[pg], vbuf.at[slot, p], sem.at[slot]))
        return copies

    def start(seq, slot):
        for n, c in enumerate(page_copies(seq, slot)):
            c.start(priority=n % 2)

    def wait(seq, slot):
        for c in page_copies(seq, slot):
            c.wait()

    @pl.when(step == 0)
    def _():
        for t in range(N_PAGE_SLOTS):
            start(t, t)

    stages = _ffn_stages(x_ref, d_ref, r_ref, wo_ref, gf_ref, wi_ref, wo2_ref, gl_ref, y_ref, act, fc)
    total = sum(w for w, _ in stages)
    shares = [[] for _ in range(per_step)]
    done = 0
    for w, stage in stages:
        shares[min(per_step - 1, (2 * done + w) * per_step // (2 * total))].append(stage)
        done += w

    for t in range(per_step):
        seq = step * per_step + t
        slot = t % N_PAGE_SLOTS
        wait(seq, slot)
        o = _sattn_compute(dl_ref, g_ref, qb_ref[t], kn_ref[t], vn_ref[t],
                           lambda p: kbuf[slot, p], lambda p: vbuf[slot, p], n_pages, page)
        ds_ref[t] = o.astype(ds_ref.dtype)
        for stage in shares[t]:
            stage()

        @pl.when(seq + N_PAGE_SLOTS < n_seq)
        def _():
            start(seq + N_PAGE_SLOTS, slot)


def _merge_ffn_sample_attention(x, d, r, w_out, g_ffn, w_ffn_in, w_ffn_out, g_final, tm, fc,
                                page_table, dl, subln_g, qblk, k_new, v_new, kc, vc):
    n, dm = x.shape
    d_ff = w_ffn_out.shape[0]
    db, n_pages = page_table.shape
    t_seq = k_new.shape[1]
    page = kc.shape[2]
    n_steps = n // tm
    assert db % n_steps == 0 and (db // n_steps) % N_PAGE_SLOTS == 0, "sequences per FFN step"
    per_step = db // n_steps
    row = lambda width: pl.BlockSpec((tm, width), lambda i, pt: (i, 0))
    const = lambda a: pl.BlockSpec(a.shape, lambda i, pt: (0,) * a.ndim,
                                   pipeline_mode=pl.Buffered(1))
    per_seq = lambda a: pl.BlockSpec((per_step,) + a.shape[1:], lambda i, pt: (i, 0, 0))
    hbm = pl.BlockSpec(memory_space=pl.ANY)
    grid_spec = pltpu.PrefetchScalarGridSpec(
        num_scalar_prefetch=1,
        grid=(n_steps,),
        in_specs=[row(dm), row(DIFF_VW), row(RET_VW), const(w_out), const(g_ffn),
                  const(w_ffn_in), const(w_ffn_out), const(g_final), const(dl), const(subln_g),
                  per_seq(qblk), per_seq(k_new), per_seq(v_new), hbm, hbm],
        out_specs=(row(dm), pl.BlockSpec((per_step, t_seq, DIFF_VW), lambda i, pt: (i, 0, 0))),
        scratch_shapes=[pltpu.VMEM((tm, d_ff), BF16),
                        pltpu.VMEM((N_PAGE_SLOTS, n_pages, DIFF_W, page), F32),
                        pltpu.VMEM((N_PAGE_SLOTS, n_pages, page * N_DIFF_HEADS, DIFF_V_DIM), F32),
                        pltpu.SemaphoreType.DMA((N_PAGE_SLOTS,))],
    )
    return pl.pallas_call(
        functools.partial(_ffn_sattn_kernel, fc=fc, n_seq=db, n_pages=n_pages),
        grid_spec=grid_spec,
        out_shape=(jax.ShapeDtypeStruct((n, dm), F32),
                   jax.ShapeDtypeStruct((db, t_seq, DIFF_VW), BF16)),
        compiler_params=_params("arbitrary"),
        name="merge_ffn_sample_attn",
    )(page_table.reshape(-1), x, d, r, w_out, g_ffn, w_ffn_in, w_ffn_out, g_final,
      dl, subln_g, qblk, k_new, v_new, kc, vc)


def _ffn_chunk(d_ff):
    for fc in (512, 256, 128):
        if d_ff % fc == 0:
            return fc
    return d_ff


def kernel(x_prompt, x_sample, cache_diff_k, cache_diff_v, state_ret, page_table, norm_mix_g, w_in, diff_lambda, diff_subln_g, w_out, norm_ffn_g, w_ffn_in, w_ffn_out, norm_final_g):
    bsz, seq, dm = x_prompt.shape
    db, t_seq, _ = x_sample.shape
    n_pages = page_table.shape[1]
    page = cache_diff_k.shape[2]
    past = n_pages * page
    assert w_in.shape[0] == 1, "single layer"
    assert LANES % t_seq == 0 and (db * t_seq) % LANES == 0

    w_in0 = w_in[0]
    w_out16 = w_out[0].astype(BF16)
    w_ffn_in16 = w_ffn_in[0].astype(BF16)
    w_ffn_out16 = w_ffn_out[0].astype(BF16)
    g_mix, g_ffn = norm_mix_g, norm_ffn_g
    g_final = norm_final_g.reshape(1, dm)
    dl = diff_lambda[0]
    fc = _ffn_chunk(w_ffn_out.shape[1])

    tm = min(512, seq)
    tabs_p = _rotary_tables(jnp.arange(seq))
    qt16, k16, kt32, v32, vt16, r_p, ret_state_p = _inproj(
        x_prompt, g_mix, w_in0, tabs_p, min(1024, seq), with_retention=True)
    d_p = _prompt_attention(dl, diff_subln_g, qt16, k16, vt16, min(1024, seq), min(512, seq))
    k_prompt = jnp.swapaxes(kt32, 1, 2).reshape(1, bsz, seq, N_DIFF_HEADS, 2, DIFF_QK_DIM)
    v_prompt = v32.reshape(1, bsz, seq, N_DIFF_HEADS, DIFF_V_DIM)
    ret_prompt = ret_state_p.reshape(1, bsz, N_RET_HEADS, RET_QK_DIM, RET_V_DIM)

    n_s = db * t_seq
    pos_s = past + (jnp.arange(n_s) % t_seq)
    tabs_s = _rotary_tables(pos_s)
    tm_s = min(512, n_s)
    qts, ks16, kts32, vs32, _, rqs, rkts, rvs, rgs = _inproj(
        x_sample.reshape(1, n_s, dm), g_mix, w_in0, tabs_s, tm_s, with_retention=False)
    k_s = kts32[0].T
    qblk = qts[0].T.reshape(db, t_seq, DIFF_W)
    kc = jnp.transpose(cache_diff_k[0], (0, 2, 3, 4, 1)).reshape(-1, DIFF_W, page)
    y_prompt, d_s = _merge_ffn_sample_attention(
        x_prompt.reshape(bsz * seq, dm), d_p.reshape(bsz * seq, DIFF_VW),
        r_p.reshape(bsz * seq, RET_VW), w_out16, g_ffn, w_ffn_in16, w_ffn_out16, g_final, tm, fc,
        page_table, dl, diff_subln_g, qblk, ks16.reshape(db, t_seq, DIFF_W),
        vs32.astype(BF16).reshape(db, t_seq, DIFF_VW), kc,
        cache_diff_v[0].reshape(-1, page * N_DIFF_HEADS, DIFF_V_DIM))
    y_prompt = y_prompt.reshape(bsz, seq, dm)
    r_s, ret_state_s = _sample_retention(rqs, rkts, rvs, rgs, state_ret[0], t_seq)
    y_sample = _merge_ffn(x_sample.reshape(n_s, dm), d_s.reshape(n_s, DIFF_VW),
                          r_s.reshape(n_s, RET_VW), w_out16, g_ffn, w_ffn_in16,
                          w_ffn_out16, g_final, tm_s, fc).reshape(db, t_seq, dm)
    k_sample = k_s.reshape(1, db, t_seq, N_DIFF_HEADS, 2, DIFF_QK_DIM)
    v_sample = vs32.reshape(1, db, t_seq, N_DIFF_HEADS, DIFF_V_DIM)
    ret_sample = ret_state_s[None]

    return (y_prompt, y_sample, k_prompt, v_prompt, ret_prompt, k_sample, v_sample, ret_sample)
```
